```python
import math
import jax, jax.numpy as jnp
from jax import lax
import numpy as np

D_MODEL = 1024
BATCH = 8
SEQ = 4096
DEPTH = 2

N_A_LAYERS = DEPTH // 2
N_B_LAYERS = DEPTH - N_A_LAYERS
S5_GROUP = 16
S5_GROUPS = D_MODEL // S5_GROUP
S5_STATE = 64
DT_MIN = 1e-3
DT_MAX = 1e-1
N_HEADS = 16
HEAD_DIM = D_MODEL // N_HEADS
Q_BLOCK = 128
N_EXPERT_GROUPS = 4
EXPERTS_PER_GROUP = 8
N_EXPERTS = N_EXPERT_GROUPS * EXPERTS_PER_GROUP
TOP_K = 2
D_EXPERT = D_MODEL // 4
EPS = 1e-6
NEG = -1e30

kernel_name = "yoco_s5_fox_hier_moe_adaln"


def rmsnorm(x, g):
    x32 = x.astype(jnp.float32)
    y = x32 * lax.rsqrt(jnp.mean(x32 * x32, axis=-1, keepdims=True) + EPS)
    return (y * g.astype(jnp.float32)).astype(x.dtype)


def adaln(c, w, b, n):
    mod = jax.nn.silu(c) @ w + b
    return jnp.split(mod, n, axis=-1)


def modulate(h, shift, scale):
    return h * (1.0 + scale[:, None, :]) + shift[:, None, :]


def _complex_combine(e1, e2):
    ar1, ai1, br1, bi1 = e1
    ar2, ai2, br2, bi2 = e2
    ar = ar2 * ar1 - ai2 * ai1
    ai = ar2 * ai1 + ai2 * ar1
    br = ar2 * br1 - ai2 * bi1 + br2
    bi = ar2 * bi1 + ai2 * br1 + bi2
    return (ar, ai, br, bi)


def s5_mixer(h, w_in, lam_re, lam_im, log_dt, b_re, b_im, c_re, c_im, d_skip, w_out):
    bsz, seq, _ = h.shape
    u = (h @ w_in).astype(jnp.float32)
    ug = u.reshape(bsz, seq, S5_GROUPS, S5_GROUP)
    dt = jnp.exp(log_dt.astype(jnp.float32))[:, None]
    lr = lam_re.astype(jnp.float32)
    li = lam_im.astype(jnp.float32)
    mag = jnp.exp(lr * dt)
    a_re = mag * jnp.cos(li * dt)
    a_im = mag * jnp.sin(li * dt)
    den = lr * lr + li * li
    coef_re = ((a_re - 1.0) * lr + a_im * li) / den
    coef_im = (a_im * lr - (a_re - 1.0) * li) / den
    br_, bi_ = b_re.astype(jnp.float32), b_im.astype(jnp.float32)
    bbar_re = coef_re[..., None] * br_ - coef_im[..., None] * bi_
    bbar_im = coef_re[..., None] * bi_ + coef_im[..., None] * br_
    bu_re = jnp.einsum('blgc,gpc->blgp', ug, bbar_re)
    bu_im = jnp.einsum('blgc,gpc->blgp', ug, bbar_im)
    shape_a = (1, seq, S5_GROUPS, S5_STATE)
    a_re_l = jnp.broadcast_to(a_re[None, None], shape_a)
    a_im_l = jnp.broadcast_to(a_im[None, None], shape_a)
    _, _, s_re, s_im = lax.associative_scan(_complex_combine, (a_re_l, a_im_l, bu_re, bu_im), axis=1)
    y = (jnp.einsum('blgp,gcp->blgc', s_re, c_re.astype(jnp.float32))
         - jnp.einsum('blgp,gcp->blgc', s_im, c_im.astype(jnp.float32)))
    y = y.reshape(bsz, seq, D_MODEL) + d_skip.astype(jnp.float32) * u
    y = jax.nn.gelu(y).astype(h.dtype)
    val, gate = jnp.split(y @ w_out, 2, axis=-1)
    return val * jax.nn.sigmoid(gate)


def shared_kv(x, c, kv_g, kv_ada_w, kv_ada_b, kv_w, kv_fb, k_norm_g):
    bsz, seq, _ = x.shape
    shift, scale = adaln(c, kv_ada_w, kv_ada_b, 2)
    h = modulate(rmsnorm(x, kv_g), shift, scale)
    k, v, fz = jnp.split(h @ kv_w, [D_MODEL, 2 * D_MODEL], axis=-1)
    k = rmsnorm(k.reshape(bsz, seq, N_HEADS, HEAD_DIM), k_norm_g).transpose(0, 2, 1, 3)
    v = v.reshape(bsz, seq, N_HEADS, HEAD_DIM).transpose(0, 2, 1, 3)
    logf = jax.nn.log_sigmoid(fz.astype(jnp.float32) + kv_fb.astype(jnp.float32))
    fcum = jnp.cumsum(logf, axis=1).transpose(0, 2, 1)
    return k, v, fcum


def fox_mixer(h, w_qg, q_norm_g, w_o, k, v, fcum):
    bsz, seq, _ = h.shape
    q, og = jnp.split(h @ w_qg, 2, axis=-1)
    q = rmsnorm(q.reshape(bsz, seq, N_HEADS, HEAD_DIM), q_norm_g).transpose(0, 2, 1, 3)
    scale = HEAD_DIM ** -0.5
    outs = []
    for i in range(seq // Q_BLOCK):
        s0 = i * Q_BLOCK
        s1 = s0 + Q_BLOCK
        qb = q[:, :, s0:s1]
        kb = k[:, :, :s1]
        vb = v[:, :, :s1]
        logits = (jnp.einsum('bhqd,bhkd->bhqk', qb, kb).astype(jnp.float32) * scale
                  + fcum[:, :, s0:s1, None] - fcum[:, :, None, :s1])
        mask = (s0 + jnp.arange(Q_BLOCK))[:, None] >= jnp.arange(s1)[None, :]
        p = jax.nn.softmax(jnp.where(mask, logits, NEG), axis=-1)
        outs.append(jnp.einsum('bhqk,bhkd->bhqd', p.astype(vb.dtype), vb))
    o = jnp.concatenate(outs, axis=2).transpose(0, 2, 1, 3).reshape(bsz, seq, D_MODEL)
    return (o * jax.nn.sigmoid(og)) @ w_o


def hier_moe(h, wg, bg, we, be, w1, w3, w2):
    bsz, seq, d = h.shape
    t = h.reshape(-1, d)
    g_logits = (t @ wg + bg).astype(jnp.float32)
    g_prob = jax.nn.softmax(g_logits, axis=-1)
    g_top_v, g_idx = lax.top_k(g_logits, 1)
    p_g = jnp.take_along_axis(g_prob, g_idx, axis=1)
    e_logits = (t @ we + be).astype(jnp.float32).reshape(-1, N_EXPERT_GROUPS, EXPERTS_PER_GROUP)
    e_sel = jnp.take_along_axis(e_logits, g_idx[:, :, None], axis=1)[:, 0]
    top_v, top_i = lax.top_k(e_sel, TOP_K)
    w = jax.nn.softmax(top_v, axis=-1) * p_g
    eid = g_idx * EXPERTS_PER_GROUP + top_i
    gates = jnp.sum(jax.nn.one_hot(eid, N_EXPERTS, dtype=jnp.float32) * w[..., None], axis=1)
    out = jnp.zeros(t.shape, jnp.float32)
    for e in range(N_EXPERTS):
        y = (jax.nn.silu(t @ w1[e]) * (t @ w3[e])) @ w2[e]
        out = out + gates[:, e:e + 1] * y.astype(jnp.float32)
    return out.astype(h.dtype).reshape(bsz, seq, d)


def setup_inputs(seed: int = 0) -> dict:
    key = jax.random.key(seed)
    ks = iter(jax.random.split(key, 40))
    f32 = jnp.float32

    def nrm(shape, std):
        return std * jax.random.normal(next(ks), shape, f32)

    D, G, P, GC = D_MODEL, S5_GROUPS, S5_STATE, S5_GROUP
    H, HD, E, F, NG = N_HEADS, HEAD_DIM, N_EXPERTS, D_EXPERT, N_EXPERT_GROUPS
    NA, NB = N_A_LAYERS, N_B_LAYERS
    x = nrm((BATCH, SEQ, D), 1.0)
    c = nrm((BATCH, D), 1.0)
    ln_g = 1.0 + nrm((DEPTH, 2, D), 0.02)
    ada_w = nrm((DEPTH, 2, D, 3 * D), 0.5 * D ** -0.5)
    ada_b = nrm((DEPTH, 2, 3 * D), 0.02)
    s5_w_in = nrm((NA, D, D), D ** -0.5)
    n_idx = jnp.arange(P, dtype=f32)
    s5_lambda_re = -0.5 * jnp.exp(nrm((NA, G, P), 0.05))
    s5_lambda_im = math.pi * n_idx + nrm((NA, G, P), 0.01)
    s5_log_dt = jax.random.uniform(next(ks), (NA, G), f32, math.log(DT_MIN), math.log(DT_MAX))
    s5_b_re = nrm((NA, G, P, GC), (2 * GC) ** -0.5)
    s5_b_im = nrm((NA, G, P, GC), (2 * GC) ** -0.5)
    s5_c_re = nrm((NA, G, GC, P), P ** -0.5)
    s5_c_im = nrm((NA, G, GC, P), P ** -0.5)
    s5_d = nrm((NA, D), 0.5)
    s5_w_out = nrm((NA, D, 2 * D), D ** -0.5)
    kv_g = 1.0 + nrm((D,), 0.02)
    kv_ada_w = nrm((D, 2 * D), 0.5 * D ** -0.5)
    kv_ada_b = nrm((2 * D,), 0.02)
    kv_w = nrm((D, 2 * D + H), D ** -0.5)
    kv_fb = jax.random.uniform(next(ks), (H,), f32, 1.0, 6.0)
    k_norm_g = 1.0 + nrm((HD,), 0.02)
    fox_w_qg = nrm((NB, D, 2 * D), D ** -0.5)
    fox_q_norm_g = 1.0 + nrm((NB, HD), 0.02)
    fox_w_o = nrm((NB, D, D), D ** -0.5)
    moe_wg = nrm((DEPTH, D, NG), D ** -0.5)
    moe_bg = nrm((DEPTH, NG), 0.01)
    moe_we = nrm((DEPTH, D, E), D ** -0.5)
    moe_be = nrm((DEPTH, E), 0.01)
    moe_w1 = nrm((DEPTH, E, D, F), D ** -0.5)
    moe_w3 = nrm((DEPTH, E, D, F), D ** -0.5)
    moe_w2 = nrm((DEPTH, E, F, D), F ** -0.5)
    return {"x": x, "c": c, "ln_g": ln_g, "ada_w": ada_w, "ada_b": ada_b,
            "s5_w_in": s5_w_in, "s5_lambda_re": s5_lambda_re, "s5_lambda_im": s5_lambda_im,
            "s5_log_dt": s5_log_dt, "s5_b_re": s5_b_re, "s5_b_im": s5_b_im,
            "s5_c_re": s5_c_re, "s5_c_im": s5_c_im, "s5_d": s5_d, "s5_w_out": s5_w_out,
            "kv_g": kv_g, "kv_ada_w": kv_ada_w, "kv_ada_b": kv_ada_b, "kv_w": kv_w,
            "kv_fb": kv_fb, "k_norm_g": k_norm_g,
            "fox_w_qg": fox_w_qg, "fox_q_norm_g": fox_q_norm_g, "fox_w_o": fox_w_o,
            "moe_wg": moe_wg, "moe_bg": moe_bg, "moe_we": moe_we, "moe_be": moe_be,
            "moe_w1": moe_w1, "moe_w3": moe_w3, "moe_w2": moe_w2}


def reference(x, c, ln_g, ada_w, ada_b,
              s5_w_in, s5_lambda_re, s5_lambda_im, s5_log_dt, s5_b_re, s5_b_im,
              s5_c_re, s5_c_im, s5_d, s5_w_out,
              kv_g, kv_ada_w, kv_ada_b, kv_w, kv_fb, k_norm_g,
              fox_w_qg, fox_q_norm_g, fox_w_o,
              moe_wg, moe_bg, moe_we, moe_be, moe_w1, moe_w3, moe_w2):
    h = x
    k = v = fcum = None
    for l in range(DEPTH):
        shift, scale, gate = adaln(c, ada_w[l, 0], ada_b[l, 0], 3)
        hn = modulate(rmsnorm(h, ln_g[l, 0]), shift, scale)
        if l < N_A_LAYERS:
            mix = s5_mixer(hn, s5_w_in[l], s5_lambda_re[l], s5_lambda_im[l], s5_log_dt[l],
                           s5_b_re[l], s5_b_im[l], s5_c_re[l], s5_c_im[l], s5_d[l], s5_w_out[l])
        else:
            j = l - N_A_LAYERS
            mix = fox_mixer(hn, fox_w_qg[j], fox_q_norm_g[j], fox_w_o[j], k, v, fcum)
        h = h + gate[:, None, :] * mix
        shift, scale, gate = adaln(c, ada_w[l, 1], ada_b[l, 1], 3)
        hn = modulate(rmsnorm(h, ln_g[l, 1]), shift, scale)
        h = h + gate[:, None, :] * hier_moe(hn, moe_wg[l], moe_bg[l], moe_we[l], moe_be[l],
                                            moe_w1[l], moe_w3[l], moe_w2[l])
        if l == N_A_LAYERS - 1:
            k, v, fcum = shared_kv(h, c, kv_g, kv_ada_w, kv_ada_b, kv_w, kv_fb, k_norm_g)
    return h
```

```python
import functools

import jax
import jax.numpy as jnp
from jax import lax
from jax.experimental import pallas as pl
from jax.experimental.pallas import tpu as pltpu

F32 = jnp.float32
BF16 = jnp.bfloat16
I32 = jnp.int32

EPS = 1e-6
NEG = -1e30
LANES = 128
SUBLANES = 8
VMEM_LIMIT_BYTES = 56 * 1024 * 1024

S5_GROUPS_PER_BLOCK = 16
N_EXPERT_GROUPS = 4
EXPERTS_PER_GROUP = 8
N_EXPERTS = N_EXPERT_GROUPS * EXPERTS_PER_GROUP
N_BUCKETS = N_EXPERT_GROUPS * EXPERTS_PER_GROUP * EXPERTS_PER_GROUP
N_PAIR_BUCKETS = N_EXPERT_GROUPS * (EXPERTS_PER_GROUP * (EXPERTS_PER_GROUP - 1) // 2)
META_LANES = LANES
HEAD_DIM = 64


def _params(*sem):
    return pltpu.CompilerParams(dimension_semantics=sem, vmem_limit_bytes=VMEM_LIMIT_BYTES)


def _rms_mod(x, g, shift, scale):
    ms = jnp.mean(x * x, axis=-1, keepdims=True)
    y = x * lax.rsqrt(ms + EPS) * g
    return y * (1.0 + scale) + shift


def _head_rms(x, g):
    tm, d = x.shape
    lane = lax.broadcasted_iota(I32, (tm, LANES), 1)
    lo = lane < HEAD_DIM
    outs = []
    for j in range(d // LANES):
        s = x[:, j * LANES:(j + 1) * LANES]
        sq = s * s
        s_lo = jnp.sum(jnp.where(lo, sq, 0.0), axis=-1, keepdims=True)
        s_hi = jnp.sum(jnp.where(lo, 0.0, sq), axis=-1, keepdims=True)
        r = jnp.where(lo, lax.rsqrt(s_lo / HEAD_DIM + EPS), lax.rsqrt(s_hi / HEAD_DIM + EPS))
        outs.append(s * r)
    return jnp.concatenate(outs, axis=-1) * g


def _adaln_body(c_ref, w_ref, b_ref, o_ref):
    c = c_ref[...]
    s = c * jax.nn.sigmoid(c)
    o_ref[0] = jnp.dot(s, w_ref[0], preferred_element_type=F32) + b_ref[0]


def _adaln(c, w, b):
    n_sets, d, n = w.shape
    bsz = c.shape[0]
    tn = 512 if n % 512 == 0 else n
    return pl.pallas_call(
        _adaln_body,
        grid=(n_sets, n // tn),
        in_specs=[pl.BlockSpec((bsz, d), lambda s, j: (0, 0)),
                  pl.BlockSpec((1, d, tn), lambda s, j: (s, 0, j)),
                  pl.BlockSpec((1, 1, tn), lambda s, j: (s, 0, j))],
        out_specs=pl.BlockSpec((1, bsz, tn), lambda s, j: (s, 0, j)),
        out_shape=jax.ShapeDtypeStruct((n_sets, bsz, n), F32),
        compiler_params=_params("parallel", "parallel"),
        name="adaln",
    )(c, w, b)


def _s5_in_body(x_ref, g_ref, sh_ref, sc_ref, w_ref, u_ref):
    hn = _rms_mod(x_ref[...], g_ref[...], sh_ref[0], sc_ref[0])
    u_ref[...] = jnp.dot(hn.astype(BF16), w_ref[...], preferred_element_type=F32)


def _s5_in(x2, g, shift, scale, w_in, bsz, seq, tm):
    d = x2.shape[1]
    nt = seq // tm
    return pl.pallas_call(
        _s5_in_body,
        grid=(bsz, nt),
        in_specs=[pl.BlockSpec((tm, d), lambda b, i: (b * nt + i, 0)),
                  pl.BlockSpec((1, d), lambda b, i: (0, 0)),
                  pl.BlockSpec((1, 1, d), lambda b, i: (b, 0, 0)),
                  pl.BlockSpec((1, 1, d), lambda b, i: (b, 0, 0)),
                  pl.BlockSpec((d, d), lambda b, i: (0, 0))],
        out_specs=pl.BlockSpec((tm, d), lambda b, i: (i, b)),
        out_shape=jax.ShapeDtypeStruct((seq, bsz * d), F32),
        compiler_params=_params("parallel", "parallel"),
        name="s5_in",
    )(x2, g, shift, scale, w_in)


def _s5_scan_body(u_ref, bb_ref, cb_ref, are_ref, aim_ref, y_ref, bu_ref, st_ref, *, tc, nblk, sw):
    cw = S5_GROUPS_PER_BLOCK * 16

    @pl.when(pl.program_id(0) == 0)
    def _():
        st_ref[...] = jnp.zeros_like(st_ref)

    u2 = u_ref[...].astype(BF16)
    for k in range(nblk):
        bu_ref[:, k * 2 * sw:(k + 1) * 2 * sw] = jnp.dot(
            u2[:, k * cw:(k + 1) * cw], bb_ref[k], preferred_element_type=F32)

    for k in range(nblk):
        re0, im0 = k * 2 * sw, k * 2 * sw + sw
        a_re = jnp.broadcast_to(are_ref[k], (SUBLANES, sw))
        a_im = jnp.broadcast_to(aim_ref[k], (SUBLANES, sw))

        def step(t, carry, re0=re0, im0=im0, a_re=a_re, a_im=a_im):
            s_re, s_im = carry
            r0 = pl.multiple_of(t * SUBLANES, SUBLANES)
            n_re = a_re * s_re - a_im * s_im + bu_ref[pl.ds(r0, SUBLANES), re0:re0 + sw]
            n_im = a_re * s_im + a_im * s_re + bu_ref[pl.ds(r0, SUBLANES), im0:im0 + sw]
            bu_ref[pl.ds(r0, SUBLANES), re0:re0 + sw] = n_re
            bu_ref[pl.ds(r0, SUBLANES), im0:im0 + sw] = n_im
            return n_re, n_im

        s_re, s_im = lax.fori_loop(
            0, tc, step, (st_ref[:, re0:re0 + sw], st_ref[:, im0:im0 + sw]), unroll=2)
        st_ref[:, re0:re0 + sw] = s_re
        st_ref[:, im0:im0 + sw] = s_im

    for k in range(nblk):
        s2 = bu_ref[:, k * 2 * sw:(k + 1) * 2 * sw].astype(BF16)
        y_ref[:, k * cw:(k + 1) * cw] = jnp.dot(s2, cb_ref[k], preferred_element_type=F32)


def _s5_scan(u_tb, bblk, cblk, a_re, a_im, bsz, tc):
    assert bsz == SUBLANES, "the scan keeps the batch on the 8 sublanes of a vreg"
    rows, d = u_tb.shape
    nblk, cw, sw2 = bblk.shape
    sw = sw2 // 2
    tr = tc * bsz
    body = functools.partial(_s5_scan_body, tc=tc, nblk=nblk, sw=sw)
    return pl.pallas_call(
        body,
        grid=(rows // tr,),
        in_specs=[pl.BlockSpec((tr, d), lambda i: (i, 0)),
                  pl.BlockSpec((nblk, cw, sw2), lambda i: (0, 0, 0)),
                  pl.BlockSpec((nblk, sw2, cw), lambda i: (0, 0, 0)),
                  pl.BlockSpec((nblk, 1, sw), lambda i: (0, 0, 0)),
                  pl.BlockSpec((nblk, 1, sw), lambda i: (0, 0, 0))],
        out_specs=pl.BlockSpec((tr, d), lambda i: (i, 0)),
        out_shape=jax.ShapeDtypeStruct((rows, d), F32),
        scratch_shapes=[pltpu.VMEM((tr, nblk * sw2), F32), pltpu.VMEM((bsz, nblk * sw2), F32)],
        compiler_params=_params("arbitrary"),
        name="s5_scan",
    )(u_tb, bblk, cblk, a_re, a_im)


def _s5_out_body(y_ref, u_ref, h_ref, d_ref, w_ref, gate_ref, o_ref):
    z = y_ref[...] + d_ref[...] * u_ref[...]
    act = jax.nn.gelu(z)
    vg = jnp.dot(act.astype(BF16), w_ref[...], preferred_element_type=F32)
    d = z.shape[-1]
    mix = vg[:, :d] * jax.nn.sigmoid(vg[:, d:])
    o_ref[...] = h_ref[...] + gate_ref[0] * mix


def _s5_out(y_t, u_t, h2, d_skip, w_out, gate, bsz, seq, tm):
    d = h2.shape[1]
    nt = seq // tm
    return pl.pallas_call(
        _s5_out_body,
        grid=(bsz, nt),
        in_specs=[pl.BlockSpec((tm, d), lambda b, i: (i, b)),
                  pl.BlockSpec((tm, d), lambda b, i: (i, b)),
                  pl.BlockSpec((tm, d), lambda b, i: (b * nt + i, 0)),
                  pl.BlockSpec((1, d), lambda b, i: (0, 0)),
                  pl.BlockSpec((d, 2 * d), lambda b, i: (0, 0)),
                  pl.BlockSpec((1, 1, d), lambda b, i: (b, 0, 0))],
        out_specs=pl.BlockSpec((tm, d), lambda b, i: (b * nt + i, 0)),
        out_shape=jax.ShapeDtypeStruct((bsz * seq, d), F32),
        compiler_params=_params("parallel", "parallel"),
        name="s5_out",
    )(y_t, u_t, h2, d_skip, w_out, gate)


def _s5_tables(lam_re, lam_im, log_dt, b_re, b_im, c_re, c_im):
    dt = jnp.exp(log_dt.astype(F32))[:, None]
    lr, li = lam_re.astype(F32), lam_im.astype(F32)
    mag = jnp.exp(lr * dt)
    a_re = mag * jnp.cos(li * dt)
    a_im = mag * jnp.sin(li * dt)
    den = lr * lr + li * li
    coef_re = ((a_re - 1.0) * lr + a_im * li) / den
    coef_im = (a_im * lr - (a_re - 1.0) * li) / den
    br_, bi_ = b_re.astype(F32), b_im.astype(F32)
    bbar_re = coef_re[..., None] * br_ - coef_im[..., None] * bi_
    bbar_im = coef_re[..., None] * bi_ + coef_im[..., None] * br_
    g, p, c = bbar_re.shape
    gb = S5_GROUPS_PER_BLOCK
    nblk = g // gb
    eye = jnp.eye(gb, dtype=F32)

    def in_blocks(m):
        return jnp.einsum('kgpc,gh->kgchp', m.reshape(nblk, gb, p, c), eye).reshape(nblk, gb * c, gb * p)

    def out_blocks(m):
        return jnp.einsum('kgcp,gh->kgphc', m.reshape(nblk, gb, c, p), eye).reshape(nblk, gb * p, gb * c)

    bblk = jnp.concatenate([in_blocks(bbar_re), in_blocks(bbar_im)], axis=-1).astype(BF16)
    cblk = jnp.concatenate([out_blocks(c_re.astype(F32)), -out_blocks(c_im.astype(F32))], axis=1).astype(BF16)
    return bblk, cblk, a_re.reshape(nblk, 1, gb * p), a_im.reshape(nblk, 1, gb * p)


def _router_body(h_ref, g_ref, sh_ref, sc_ref, wr_ref, br_ref, tri_ref, x_ref, cnt_ref, carry_ref):
    tm, d = h_ref.shape
    ne, ng, epg = N_EXPERTS, N_EXPERT_GROUPS, EXPERTS_PER_GROUP

    @pl.when(pl.program_id(0) == 0)
    def _():
        carry_ref[...] = jnp.zeros_like(carry_ref)

    hn = _rms_mod(h_ref[...], g_ref[...], sh_ref[0], sc_ref[0])
    logits = jnp.dot(hn, wr_ref[...], precision=lax.Precision.HIGHEST,
                     preferred_element_type=F32) + br_ref[...]
    lane = lax.broadcasted_iota(I32, logits.shape, 1).astype(F32)
    big = jnp.float32(1e9)
    ninf = jnp.float32(-jnp.inf)

    gmask = (lane >= ne) & (lane < ne + ng)
    gmax = jnp.max(jnp.where(gmask, logits, ninf), axis=-1, keepdims=True)
    gsum = jnp.sum(jnp.where(gmask, jnp.exp(logits - gmax), 0.0), axis=-1, keepdims=True)
    p_g = 1.0 / gsum
    gidx = jnp.min(jnp.where(gmask & (logits == gmax), lane - ne, big), axis=-1, keepdims=True)

    emask = (lane < ne) & (jnp.floor(lane / epg) == gidx)
    v1 = jnp.max(jnp.where(emask, logits, ninf), axis=-1, keepdims=True)
    i1 = jnp.min(jnp.where(emask & (logits == v1), lane, big), axis=-1, keepdims=True)
    emask2 = emask & (lane != i1)
    v2 = jnp.max(jnp.where(emask2, logits, ninf), axis=-1, keepdims=True)
    i2 = jnp.min(jnp.where(emask2 & (logits == v2), lane, big), axis=-1, keepdims=True)
    e21 = jnp.exp(v2 - v1)
    w1 = p_g / (1.0 + e21)
    w2 = p_g * e21 / (1.0 + e21)

    first_lo = i1 < i2
    e_lo = jnp.where(first_lo, i1, i2)
    e_hi = jnp.where(first_lo, i2, i1)
    w_lo = jnp.where(first_lo, w1, w2)
    w_hi = jnp.where(first_lo, w2, w1)
    bucket = gidx * (epg * epg) + (e_lo - gidx * epg) * epg + (e_hi - gidx * epg)

    lane_b = lax.broadcasted_iota(I32, (tm, N_BUCKETS), 1).astype(F32)
    onehot = (lane_b == bucket).astype(F32)
    prefix = jnp.dot(tri_ref[...], onehot.astype(BF16), preferred_element_type=F32)
    carry = carry_ref[...]
    rank = jnp.sum(onehot * (prefix + carry), axis=-1, keepdims=True) - 1.0
    new_carry = carry + prefix[tm - 1:tm, :]
    carry_ref[...] = new_carry
    cnt_ref[...] = new_carry

    mlane = lax.broadcasted_iota(I32, (tm, META_LANES), 1)
    meta = jnp.where(mlane == 0, bucket,
           jnp.where(mlane == 1, rank,
           jnp.where(mlane == 2, w_lo,
           jnp.where(mlane == 3, w_hi, 0.0))))
    x_ref[:, :d] = hn
    x_ref[:, d:] = meta


def _router(h2, g, shift, scale, w_r, b_r, tri, nt_per_batch, tm):
    t, d = h2.shape
    return pl.pallas_call(
        _router_body,
        grid=(t // tm,),
        in_specs=[pl.BlockSpec((tm, d), lambda i: (i, 0)),
                  pl.BlockSpec((1, d), lambda i: (0, 0)),
                  pl.BlockSpec((1, 1, d), lambda i: (i // nt_per_batch, 0, 0)),
                  pl.BlockSpec((1, 1, d), lambda i: (i // nt_per_batch, 0, 0)),
                  pl.BlockSpec((d, LANES), lambda i: (0, 0)),
                  pl.BlockSpec((1, LANES), lambda i: (0, 0)),
                  pl.BlockSpec((tm, tm), lambda i: (0, 0))],
        out_specs=[pl.BlockSpec((tm, d + META_LANES), lambda i: (i, 0)),
                   pl.BlockSpec((1, N_BUCKETS), lambda i: (0, 0))],
        out_shape=[jax.ShapeDtypeStruct((t, d + META_LANES), F32),
                   jax.ShapeDtypeStruct((1, N_BUCKETS), F32)],
        scratch_shapes=[pltpu.VMEM((1, N_BUCKETS), F32)],
        compiler_params=_params("arbitrary"),
        name="moe_router",
    )(h2, g, shift, scale, w_r, b_r, tri)


def _row_copy(src_ref, src_row, dst_ref, dst_row, sem):
    return pltpu.make_async_copy(src_ref.at[pl.ds(src_row, 1)], dst_ref.at[pl.ds(dst_row, 1)], sem)


def _dispatch_body(pos_ref, x_ref, xs_in_ref, xs_ref, sem):
    del xs_in_ref
    tm = x_ref.shape[0]

    def issue(r, c):
        _row_copy(x_ref, r, xs_ref, pos_ref[0, 0, r], sem).start()
        return c

    lax.fori_loop(0, tm, issue, 0)
    pltpu.make_async_copy(x_ref, xs_ref.at[pl.ds(0, tm)], sem).wait()


def _dispatch(xrow, pos3, t_pad, tm):
    t, w = xrow.shape
    zeros = jnp.zeros((t_pad, w), F32)
    return pl.pallas_call(
        _dispatch_body,
        grid=(t // tm,),
        in_specs=[pl.BlockSpec((1, 1, tm), lambda i: (i, 0, 0), memory_space=pltpu.SMEM),
                  pl.BlockSpec((tm, w), lambda i: (i, 0)),
                  pl.BlockSpec(memory_space=pl.ANY)],
        out_specs=pl.BlockSpec(memory_space=pl.ANY),
        out_shape=jax.ShapeDtypeStruct((t_pad, w), F32),
        scratch_shapes=[pltpu.SemaphoreType.DMA(())],
        input_output_aliases={2: 0},
        compiler_params=_params("arbitrary"),
        name="moe_dispatch",
    )(pos3, xrow, zeros)


def _expert_body(tb_ref, elo_ref, ehi_ref, valid_ref, x_ref,
                 w1a_ref, w3a_ref, w2a_ref, w1b_ref, w3b_ref, w2b_ref, o_ref):
    del tb_ref, elo_ref, ehi_ref
    d = o_ref.shape[1]

    @pl.when(valid_ref[pl.program_id(0)] == 0)
    def _():
        o_ref[...] = jnp.zeros_like(o_ref)

    @pl.when(valid_ref[pl.program_id(0)] == 1)
    def _():
        x = x_ref[:, :d].astype(BF16)

        def ffn(w1_ref, w3_ref, w2_ref, wt):
            a = jnp.dot(x, w1_ref[...].astype(BF16), preferred_element_type=F32)
            b = jnp.dot(x, w3_ref[...].astype(BF16), preferred_element_type=F32)
            mid = (a * jax.nn.sigmoid(a)) * b
            y = jnp.dot(mid.astype(BF16), w2_ref[...].astype(BF16), preferred_element_type=F32)
            return wt * y

        o_ref[...] = (ffn(w1a_ref, w3a_ref, w2a_ref, x_ref[:, d + 2:d + 3])
                      + ffn(w1b_ref, w3b_ref, w2b_ref, x_ref[:, d + 3:d + 4]))


def _experts(xs, w1, w3, w2, tile_block, e_lo, e_hi, valid, te):
    t_pad, w = xs.shape
    n_e, d, f = w1.shape
    nt = t_pad // te
    x_spec = pl.BlockSpec((te, w), lambda j, tb, lo, hi, v: (tb[j], 0))
    up_lo = pl.BlockSpec((None, d, f), lambda j, tb, lo, hi, v: (lo[j], 0, 0))
    dn_lo = pl.BlockSpec((None, f, d), lambda j, tb, lo, hi, v: (lo[j], 0, 0))
    up_hi = pl.BlockSpec((None, d, f), lambda j, tb, lo, hi, v: (hi[j], 0, 0))
    dn_hi = pl.BlockSpec((None, f, d), lambda j, tb, lo, hi, v: (hi[j], 0, 0))
    return pl.pallas_call(
        _expert_body,
        grid_spec=pltpu.PrefetchScalarGridSpec(
            num_scalar_prefetch=4,
            grid=(nt,),
            in_specs=[x_spec, up_lo, up_lo, dn_lo, up_hi, up_hi, dn_hi],
            out_specs=pl.BlockSpec((te, d), lambda j, tb, lo, hi, v: (j, 0))),
        out_shape=jax.ShapeDtypeStruct((t_pad, d), F32),
        compiler_params=_params("arbitrary"),
        name="moe_experts",
    )(tile_block, e_lo, e_hi, valid, xs, w1, w3, w2, w1, w3, w2)


def _combine_body(pos_ref, ys_ref, h_ref, gate_ref, o_ref, ybuf, sem):
    tm = h_ref.shape[0]

    def issue(r, c):
        _row_copy(ys_ref, pos_ref[0, 0, r], ybuf, r, sem).start()
        return c

    lax.fori_loop(0, tm, issue, 0)
    pltpu.make_async_copy(ys_ref.at[pl.ds(0, tm)], ybuf, sem).wait()
    o_ref[...] = h_ref[...] + gate_ref[0] * ybuf[...]


def _combine(ys, pos3, h2, gate, nt_per_batch, tm):
    t, d = h2.shape
    return pl.pallas_call(
        _combine_body,
        grid=(t // tm,),
        in_specs=[pl.BlockSpec((1, 1, tm), lambda i: (i, 0, 0), memory_space=pltpu.SMEM),
                  pl.BlockSpec(memory_space=pl.ANY),
                  pl.BlockSpec((tm, d), lambda i: (i, 0)),
                  pl.BlockSpec((1, 1, d), lambda i: (i // nt_per_batch, 0, 0))],
        out_specs=pl.BlockSpec((tm, d), lambda i: (i, 0)),
        out_shape=jax.ShapeDtypeStruct((t, d), F32),
        scratch_shapes=[pltpu.VMEM((tm, d), F32), pltpu.SemaphoreType.DMA(())],
        compiler_params=_params("arbitrary"),
        name="moe_combine",
    )(pos3, ys, h2, gate)


def _moe(h2, g, shift, scale, gate, wg, bg, we, be, w1, w3, w2, bsz, seq, tm, te):
    t, d = h2.shape
    nt_per_batch = seq // tm
    ne, ng = N_EXPERTS, N_EXPERT_GROUPS
    w_r = jnp.zeros((d, LANES), F32).at[:, :ne].set(we).at[:, ne:ne + ng].set(wg)
    b_r = jnp.zeros((1, LANES), F32).at[0, :ne].set(be).at[0, ne:ne + ng].set(bg)
    tri = (jnp.arange(tm)[:, None] >= jnp.arange(tm)[None, :]).astype(BF16)
    xrow, counts = _router(h2, g, shift, scale, w_r, b_r, tri, nt_per_batch, tm)

    counts = counts[0].astype(I32)
    bucket = xrow[:, d].astype(I32)
    rank = xrow[:, d + 1].astype(I32)
    tiles_per_bucket = (counts + te - 1) // te
    tile_end = jnp.cumsum(tiles_per_bucket)
    tile_start = tile_end - tiles_per_bucket
    n_tiles = tile_end[-1]
    pos = tile_start[bucket] * te + rank
    nt_max = t // te + N_PAIR_BUCKETS
    tile_block = jnp.minimum(jnp.arange(nt_max, dtype=I32), n_tiles - 1)
    tile_bucket = jnp.sum((tile_end[None, :] <= tile_block[:, None]).astype(I32), axis=1)
    epg = EXPERTS_PER_GROUP
    grp = tile_bucket // (epg * epg)
    e_lo = grp * epg + (tile_bucket % (epg * epg)) // epg
    e_hi = grp * epg + tile_bucket % epg
    valid = (jnp.arange(nt_max, dtype=I32) < n_tiles).astype(I32)
    pos3 = pos.reshape(t // tm, 1, tm)

    xs = _dispatch(xrow, pos3, nt_max * te, tm)
    ys = _experts(xs, w1, w3, w2, tile_block, e_lo, e_hi, valid, te)
    return _combine(ys, pos3, h2, gate, nt_per_batch, tm)


def _log_sigmoid(x):
    return jnp.minimum(x, 0.0) - jnp.log1p(jnp.exp(-jnp.abs(x)))


def _kv_body(h_ref, g_ref, sh_ref, sc_ref, wk_ref, wv_ref, wf_ref, fb_ref, kng_ref,
             k_ref, v_ref, f_ref, carry_ref):
    tm = h_ref.shape[0]

    @pl.when(pl.program_id(1) == 0)
    def _():
        carry_ref[...] = jnp.zeros_like(carry_ref)

    hn = _rms_mod(h_ref[...], g_ref[...], sh_ref[0], sc_ref[0])
    hb = hn.astype(BF16)
    k = jnp.dot(hb, wk_ref[...], preferred_element_type=F32)
    k_ref[...] = _head_rms(k, kng_ref[...]).astype(BF16)
    v_ref[...] = jnp.dot(hb, wv_ref[...], preferred_element_type=F32).astype(BF16)
    fz = jnp.dot(hn, wf_ref[...], precision=lax.Precision.HIGHEST,
                 preferred_element_type=F32) + fb_ref[...]
    c = _log_sigmoid(fz)
    row = lax.broadcasted_iota(I32, c.shape, 0)
    shift = 1
    while shift < tm:
        c = c + jnp.where(row >= shift, pltpu.roll(c, shift, 0), 0.0)
        shift *= 2
    f = c + carry_ref[...]
    f_ref[...] = f
    carry_ref[...] = f[tm - 1:tm, :]


def _shared_kv(h2, g, shift, scale, wk, wv, wf, fb, kng, bsz, seq, tm):
    t, d = h2.shape
    nt = seq // tm
    row = lambda b, i: (b * nt + i, 0)
    const = lambda b, i: (0, 0)
    return pl.pallas_call(
        _kv_body,
        grid=(bsz, nt),
        in_specs=[pl.BlockSpec((tm, d), row),
                  pl.BlockSpec((1, d), const),
                  pl.BlockSpec((1, 1, d), lambda b, i: (b, 0, 0)),
                  pl.BlockSpec((1, 1, d), lambda b, i: (b, 0, 0)),
                  pl.BlockSpec((d, d), const),
                  pl.BlockSpec((d, d), const),
                  pl.BlockSpec((d, LANES), const),
                  pl.BlockSpec((1, LANES), const),
                  pl.BlockSpec((1, d), const)],
        out_specs=[pl.BlockSpec((tm, d), row), pl.BlockSpec((tm, d), row), pl.BlockSpec((tm, LANES), row)],
        out_shape=[jax.ShapeDtypeStruct((t, d), BF16), jax.ShapeDtypeStruct((t, d), BF16),
                   jax.ShapeDtypeStruct((t, LANES), F32)],
        scratch_shapes=[pltpu.VMEM((1, LANES), F32)],
        compiler_params=_params("parallel", "arbitrary"),
        name="shared_kv",
    )(h2, g, shift, scale, wk, wv, wf, fb, kng)


def _qg_body(h_ref, g_ref, sh_ref, sc_ref, wq_ref, wg_ref, qng_ref, q_ref, og_ref):
    hn = _rms_mod(h_ref[...], g_ref[...], sh_ref[0], sc_ref[0])
    hb = hn.astype(BF16)
    q = jnp.dot(hb, wq_ref[...], preferred_element_type=F32)
    q_ref[...] = (_head_rms(q, qng_ref[...]) * (HEAD_DIM ** -0.5)).astype(BF16)
    og_ref[...] = jnp.dot(hb, wg_ref[...], preferred_element_type=F32)


def _fox_qg(h2, g, shift, scale, wq, wg, qng, bsz, seq, tm):
    t, d = h2.shape
    nt = seq // tm
    row = lambda b, i: (b * nt + i, 0)
    const = lambda b, i: (0, 0)
    return pl.pallas_call(
        _qg_body,
        grid=(bsz, nt),
        in_specs=[pl.BlockSpec((tm, d), row),
                  pl.BlockSpec((1, d), const),
                  pl.BlockSpec((1, 1, d), lambda b, i: (b, 0, 0)),
                  pl.BlockSpec((1, 1, d), lambda b, i: (b, 0, 0)),
                  pl.BlockSpec((d, d), const),
                  pl.BlockSpec((d, d), const),
                  pl.BlockSpec((1, d), const)],
        out_specs=[pl.BlockSpec((tm, d), row), pl.BlockSpec((tm, d), row)],
        out_shape=[jax.ShapeDtypeStruct((t, d), BF16), jax.ShapeDtypeStruct((t, d), F32)],
        compiler_params=_params("parallel", "parallel"),
        name="fox_qg",
    )(h2, g, shift, scale, wq, wg, qng)


def _attn_body(q_ref, k_ref, v_ref, frow_ref, ftok_ref, o_ref, *, tq):
    pair = pl.program_id(1)
    qi = pl.program_id(2)
    q2 = q_ref[...]
    ftok = ftok_ref[...]
    lane = lax.broadcasted_iota(I32, (tq, LANES), 1)
    rows = qi * tq + lax.broadcasted_iota(I32, (tq, tq), 0)
    cols_local = lax.broadcasted_iota(I32, (tq, tq), 1)
    outs = []
    for hh in range(2):
        in_head = (lane >= hh * HEAD_DIM) & (lane < (hh + 1) * HEAD_DIM)
        qm = jnp.where(in_head, q2, jnp.zeros_like(q2))
        f_t = jnp.sum(jnp.where(lane == pair * 2 + hh, ftok, 0.0), axis=-1, keepdims=True)

        def block(kb, carry, masked, hh=hh, qm=qm, f_t=f_t):
            m, l, acc = carry
            s0 = pl.multiple_of(kb * tq, tq)
            kblk = k_ref[pl.ds(s0, tq), :]
            vblk = v_ref[pl.ds(s0, tq), :]
            s = lax.dot_general(qm, kblk, (((1,), (1,)), ((), ())), preferred_element_type=F32)
            s = s + (f_t - frow_ref[0, hh:hh + 1, pl.ds(s0, tq)])
            if masked:
                s = jnp.where(rows >= s0 + cols_local, s, NEG)
            m_new = jnp.maximum(m, jnp.max(s, axis=-1, keepdims=True))
            alpha = jnp.exp(m - m_new)
            p = jnp.exp(s - m_new)
            l = alpha * l + jnp.sum(p, axis=-1, keepdims=True)
            acc = alpha * acc + jnp.dot(p.astype(BF16), vblk, preferred_element_type=F32)
            return m_new, l, acc

        init = (jnp.full((tq, 1), NEG, F32), jnp.zeros((tq, 1), F32), jnp.zeros((tq, LANES), F32))
        carry = lax.fori_loop(0, qi, functools.partial(block, masked=False), init)
        _, l, acc = block(qi, carry, True)
        outs.append(acc / l)
    o_ref[...] = jnp.where(lane < HEAD_DIM, outs[0], outs[1])


def _fox_attention(q, k, v, frow, ftok, bsz, seq, tq):
    t, d = q.shape
    n_pairs = d // LANES
    nq = seq // tq
    body = functools.partial(_attn_body, tq=tq)
    return pl.pallas_call(
        body,
        grid=(bsz, n_pairs, nq),
        in_specs=[pl.BlockSpec((tq, LANES), lambda b, j, i: (b * nq + i, j)),
                  pl.BlockSpec((seq, LANES), lambda b, j, i: (b, j)),
                  pl.BlockSpec((seq, LANES), lambda b, j, i: (b, j)),
                  pl.BlockSpec((1, 2, seq), lambda b, j, i: (b * n_pairs + j, 0, 0)),
                  pl.BlockSpec((tq, LANES), lambda b, j, i: (b * nq + i, 0))],
        out_specs=pl.BlockSpec((tq, LANES), lambda b, j, i: (b * nq + i, j)),
        out_shape=jax.ShapeDtypeStruct((t, d), F32),
        compiler_params=_params("parallel", "parallel", "arbitrary"),
        name="fox_attention",
    )(q, k, v, frow, ftok)


def _fox_out_body(o_ref, og_ref, h_ref, w_ref, gate_ref, out_ref):
    z = o_ref[...] * jax.nn.sigmoid(og_ref[...])
    out_ref[...] = h_ref[...] + gate_ref[0] * jnp.dot(z.astype(BF16), w_ref[...], preferred_element_type=F32)


def _fox_out(o, og, h2, w_o, gate, bsz, seq, tm):
    t, d = h2.shape
    nt = seq // tm
    row = lambda b, i: (b * nt + i, 0)
    return pl.pallas_call(
        _fox_out_body,
        grid=(bsz, nt),
        in_specs=[pl.BlockSpec((tm, d), row), pl.BlockSpec((tm, d), row), pl.BlockSpec((tm, d), row),
                  pl.BlockSpec((d, d), lambda b, i: (0, 0)),
                  pl.BlockSpec((1, 1, d), lambda b, i: (b, 0, 0))],
        out_specs=pl.BlockSpec((tm, d), row),
        out_shape=jax.ShapeDtypeStruct((t, d), F32),
        compiler_params=_params("parallel", "parallel"),
        name="fox_out",
    )(o, og, h2, w_o, gate)


def _tiles(seq):
    tm = min(512, seq)
    te = min(256, seq)
    tq = min(256, seq)
    tc = min(32, seq)
    return tm, te, tq, tc


def kernel(x, c, ln_g, ada_w, ada_b, s5_w_in, s5_lambda_re, s5_lambda_im, s5_log_dt, s5_b_re, s5_b_im,
           s5_c_re, s5_c_im, s5_d, s5_w_out, kv_g, kv_ada_w, kv_ada_b, kv_w, kv_fb, k_norm_g,
           fox_w_qg, fox_q_norm_g, fox_w_o, moe_wg, moe_bg, moe_we, moe_be, moe_w1, moe_w3, moe_w2):
    bsz, seq, d = x.shape
    depth = ln_g.shape[0]
    n_a = s5_w_in.shape[0]
    n_heads = d // HEAD_DIM
    tm, te, tq, tc = _tiles(seq)

    mods = _adaln(c, ada_w.reshape(depth * 2, d, 3 * d), ada_b.reshape(depth * 2, 1, 3 * d))
    mods = mods.reshape(depth, 2, bsz, 3, 1, d)
    kv_mods = _adaln(c, kv_ada_w[None], kv_ada_b[None, None]).reshape(bsz, 2, 1, d)

    h = x.reshape(bsz * seq, d)
    k = v = frow = ftok = None
    for l in range(depth):
        shift, scale, gate = mods[l, 0, :, 0], mods[l, 0, :, 1], mods[l, 0, :, 2]
        g = ln_g[l, 0][None]
        if l < n_a:
            u_t = _s5_in(h, g, shift, scale, s5_w_in[l].astype(BF16), bsz, seq, tm)
            bblk, cblk, a_re, a_im = _s5_tables(s5_lambda_re[l], s5_lambda_im[l], s5_log_dt[l],
                                                s5_b_re[l], s5_b_im[l], s5_c_re[l], s5_c_im[l])
            y_t = _s5_scan(u_t.reshape(seq * bsz, d), bblk, cblk, a_re, a_im, bsz, tc)
            h = _s5_out(y_t.reshape(seq, bsz * d), u_t, h, s5_d[l][None], s5_w_out[l].astype(BF16),
                        gate, bsz, seq, tm)
        else:
            j = l - n_a
            qng = jnp.tile(fox_q_norm_g[j], n_heads)[None]
            q, og = _fox_qg(h, g, shift, scale, fox_w_qg[j][:, :d].astype(BF16),
                            fox_w_qg[j][:, d:].astype(BF16), qng, bsz, seq, tm)
            o = _fox_attention(q, k, v, frow, ftok, bsz, seq, tq)
            h = _fox_out(o, og, h, fox_w_o[j].astype(BF16), gate, bsz, seq, tm)

        shift, scale, gate = mods[l, 1, :, 0], mods[l, 1, :, 1], mods[l, 1, :, 2]
        h = _moe(h, ln_g[l, 1][None], shift, scale, gate, moe_wg[l], moe_bg[l], moe_we[l], moe_be[l],
                 moe_w1[l], moe_w3[l], moe_w2[l], bsz, seq, tm, te)

        if l == n_a - 1:
            wf = jnp.zeros((d, LANES), F32).at[:, :n_heads].set(kv_w[:, 2 * d:])
            fb = jnp.zeros((1, LANES), F32).at[0, :n_heads].set(kv_fb)
            kng = jnp.tile(k_norm_g, n_heads)[None]
            k, v, ftok = _shared_kv(h, kv_g[None], kv_mods[:, 0], kv_mods[:, 1], kv_w[:, :d].astype(BF16),
                                    kv_w[:, d:2 * d].astype(BF16), wf, fb, kng, bsz, seq, tm)
            frow = ftok[:, :n_heads].reshape(bsz, seq, n_heads).transpose(0, 2, 1).reshape(
                bsz * n_heads // 2, 2, seq)
    return h.reshape(bsz, seq, d)
```

```python
import functools
import math

import numpy as np
import jax
import jax.numpy as jnp
from jax import lax
from jax.experimental import pallas as pl
from jax.experimental.pallas import tpu as pltpu

F32 = jnp.float32
BF16 = jnp.bfloat16
I32 = jnp.int32

EPS = 1e-6
NEG = -1e30
LOG2E = math.log2(math.e)
LANES = 128
SUBLANES = 8
VMEM_LIMIT_BYTES = 56 * 1024 * 1024

S5_GROUPS_PER_BLOCK = 16
N_EXPERT_GROUPS = 4
EXPERTS_PER_GROUP = 8
N_EXPERTS = N_EXPERT_GROUPS * EXPERTS_PER_GROUP
N_BUCKETS = N_EXPERT_GROUPS * EXPERTS_PER_GROUP * EXPERTS_PER_GROUP
N_PAIR_BUCKETS = N_EXPERT_GROUPS * (EXPERTS_PER_GROUP * (EXPERTS_PER_GROUP - 1) // 2)
META_LANES = LANES
META_ROWS = SUBLANES
HEAD_DIM = 64
ATTN_HEADS_PER_STEP = 4
AUG_LANES_PER_HEAD = 8
F_SPLIT = 3


def _params(*sem):
    return pltpu.CompilerParams(dimension_semantics=sem, vmem_limit_bytes=VMEM_LIMIT_BYTES)


def _rms_mod(x, g, shift, scale):
    ms = jnp.mean(x * x, axis=-1, keepdims=True)
    y = x * lax.rsqrt(ms + EPS) * g
    return y * (1.0 + scale) + shift


def _head_rms(x, g):
    tm, d = x.shape
    lane = lax.broadcasted_iota(I32, (tm, LANES), 1)
    lo = lane < HEAD_DIM
    outs = []
    for j in range(d // LANES):
        s = x[:, j * LANES:(j + 1) * LANES]
        sq = s * s
        s_lo = jnp.sum(jnp.where(lo, sq, 0.0), axis=-1, keepdims=True)
        s_hi = jnp.sum(jnp.where(lo, 0.0, sq), axis=-1, keepdims=True)
        r = jnp.where(lo, lax.rsqrt(s_lo / HEAD_DIM + EPS), lax.rsqrt(s_hi / HEAD_DIM + EPS))
        outs.append(s * r)
    return jnp.concatenate(outs, axis=-1) * g


def _adaln_body(c_ref, w_ref, b_ref, o_ref):
    c = c_ref[...]
    s = c * jax.nn.sigmoid(c)
    o_ref[0] = jnp.dot(s, w_ref[0], preferred_element_type=F32) + b_ref[0]


def _adaln(c, w, b):
    n_sets, d, n = w.shape
    bsz = c.shape[0]
    tn = 512 if n % 512 == 0 else n
    return pl.pallas_call(
        _adaln_body,
        grid=(n_sets, n // tn),
        in_specs=[pl.BlockSpec((bsz, d), lambda s, j: (0, 0)),
                  pl.BlockSpec((1, d, tn), lambda s, j: (s, 0, j)),
                  pl.BlockSpec((1, 1, tn), lambda s, j: (s, 0, j))],
        out_specs=pl.BlockSpec((1, bsz, tn), lambda s, j: (s, 0, j)),
        out_shape=jax.ShapeDtypeStruct((n_sets, bsz, n), F32),
        compiler_params=_params("parallel", "parallel"),
        name="adaln",
    )(c, w, b)


def _s5_in_body(x_ref, g_ref, sh_ref, sc_ref, w_ref, u_ref):
    hn = _rms_mod(x_ref[...], g_ref[...], sh_ref[0], sc_ref[0])
    u_ref[...] = jnp.dot(hn.astype(BF16), w_ref[...], preferred_element_type=F32)


def _s5_in(x2, g, shift, scale, w_in, bsz, seq, tm):
    d = x2.shape[1]
    nt = seq // tm
    row = lambda b, i: (b * nt + i, 0)
    return pl.pallas_call(
        _s5_in_body,
        grid=(bsz, nt),
        in_specs=[pl.BlockSpec((tm, d), row),
                  pl.BlockSpec((1, d), lambda b, i: (0, 0)),
                  pl.BlockSpec((1, 1, d), lambda b, i: (b, 0, 0)),
                  pl.BlockSpec((1, 1, d), lambda b, i: (b, 0, 0)),
                  pl.BlockSpec((d, d), lambda b, i: (0, 0))],
        out_specs=pl.BlockSpec((tm, d), row),
        out_shape=jax.ShapeDtypeStruct((bsz * seq, d), F32),
        compiler_params=_params("parallel", "parallel"),
        name="s5_in",
    )(x2, g, shift, scale, w_in)


def _s5_scan_body(u_hbm, bb_ref, cb_ref, are_ref, aim_ref, y_hbm,
                  ubuf, ybuf, bu_ref, st_ref, sem_in, sem_out, *, tc, nblk, sw, seq, n_chunks):
    bsz = SUBLANES
    cw = S5_GROUPS_PER_BLOCK * 16
    i = pl.program_id(0)
    slot = lax.rem(i, 2)

    def in_copy(chunk, sl, b):
        return pltpu.make_async_copy(u_hbm.at[pl.ds(b * seq + chunk * tc, tc)],
                                     ubuf.at[sl, :, b, :], sem_in.at[sl])

    def out_copy(chunk, sl, b):
        return pltpu.make_async_copy(ybuf.at[sl, :, b, :],
                                     y_hbm.at[pl.ds(b * seq + chunk * tc, tc)], sem_out.at[sl])

    @pl.when(i == 0)
    def _():
        st_ref[...] = jnp.zeros_like(st_ref)
        for b in range(bsz):
            in_copy(0, 0, b).start()

    @pl.when(i + 1 < n_chunks)
    def _():
        for b in range(bsz):
            in_copy(i + 1, 1 - slot, b).start()

    for b in range(bsz):
        in_copy(i, slot, b).wait()

    @pl.when(i >= 2)
    def _():
        for b in range(bsz):
            out_copy(i - 2, slot, b).wait()

    d = ubuf.shape[-1]
    u2 = ubuf[slot].reshape(tc * bsz, d).astype(BF16)
    for k in range(nblk):
        bu_ref[:, k * 2 * sw:(k + 1) * 2 * sw] = jnp.dot(
            u2[:, k * cw:(k + 1) * cw], bb_ref[k], preferred_element_type=F32)

    for k in range(nblk):
        re0, im0 = k * 2 * sw, k * 2 * sw + sw
        a_re = jnp.broadcast_to(are_ref[k], (SUBLANES, sw))
        a_im = jnp.broadcast_to(aim_ref[k], (SUBLANES, sw))

        def step(t, carry, re0=re0, im0=im0, a_re=a_re, a_im=a_im):
            s_re, s_im = carry
            r0 = pl.multiple_of(t * SUBLANES, SUBLANES)
            n_re = a_re * s_re - a_im * s_im + bu_ref[pl.ds(r0, SUBLANES), re0:re0 + sw]
            n_im = a_re * s_im + a_im * s_re + bu_ref[pl.ds(r0, SUBLANES), im0:im0 + sw]
            bu_ref[pl.ds(r0, SUBLANES), re0:re0 + sw] = n_re
            bu_ref[pl.ds(r0, SUBLANES), im0:im0 + sw] = n_im
            return n_re, n_im

        s_re, s_im = lax.fori_loop(
            0, tc, step, (st_ref[:, re0:re0 + sw], st_ref[:, im0:im0 + sw]), unroll=2)
        st_ref[:, re0:re0 + sw] = s_re
        st_ref[:, im0:im0 + sw] = s_im

    for k in range(nblk):
        s2 = bu_ref[:, k * 2 * sw:(k + 1) * 2 * sw].astype(BF16)
        yk = jnp.dot(s2, cb_ref[k], preferred_element_type=F32)
        ybuf[slot, :, :, k * cw:(k + 1) * cw] = yk.reshape(tc, bsz, cw)

    for b in range(bsz):
        out_copy(i, slot, b).start()

    @pl.when(i == n_chunks - 1)
    def _():
        if n_chunks >= 2:
            for b in range(bsz):
                out_copy(i - 1, 1 - slot, b).wait()
        for b in range(bsz):
            out_copy(i, slot, b).wait()


def _s5_scan(u2, bblk, cblk, a_re, a_im, bsz, seq, tc):
    assert bsz == SUBLANES, "the scan keeps the batch on the 8 sublanes of a vreg"
    rows, d = u2.shape
    nblk, cw, sw2 = bblk.shape
    sw = sw2 // 2
    n_chunks = seq // tc
    body = functools.partial(_s5_scan_body, tc=tc, nblk=nblk, sw=sw, seq=seq, n_chunks=n_chunks)
    return pl.pallas_call(
        body,
        grid=(n_chunks,),
        in_specs=[pl.BlockSpec(memory_space=pl.ANY),
                  pl.BlockSpec((nblk, cw, sw2), lambda i: (0, 0, 0)),
                  pl.BlockSpec((nblk, sw2, cw), lambda i: (0, 0, 0)),
                  pl.BlockSpec((nblk, 1, sw), lambda i: (0, 0, 0)),
                  pl.BlockSpec((nblk, 1, sw), lambda i: (0, 0, 0))],
        out_specs=pl.BlockSpec(memory_space=pl.ANY),
        out_shape=jax.ShapeDtypeStruct((rows, d), F32),
        scratch_shapes=[pltpu.VMEM((2, tc, bsz, d), F32), pltpu.VMEM((2, tc, bsz, d), F32),
                        pltpu.VMEM((tc * bsz, nblk * sw2), F32), pltpu.VMEM((bsz, nblk * sw2), F32),
                        pltpu.SemaphoreType.DMA((2,)), pltpu.SemaphoreType.DMA((2,))],
        compiler_params=_params("arbitrary"),
        name="s5_scan",
    )(u2, bblk, cblk, a_re, a_im)


def _s5_out_body(y_ref, u_ref, h_ref, d_ref, w_ref, gate_ref, o_ref):
    z = y_ref[...] + d_ref[...] * u_ref[...]
    act = jax.nn.gelu(z)
    vg = jnp.dot(act.astype(BF16), w_ref[...], preferred_element_type=F32)
    d = z.shape[-1]
    mix = vg[:, :d] * jax.nn.sigmoid(vg[:, d:])
    o_ref[...] = h_ref[...] + gate_ref[0] * mix


def _s5_out(y2, u2, h2, d_skip, w_out, gate, bsz, seq, tm):
    d = h2.shape[1]
    nt = seq // tm
    row = lambda b, i: (b * nt + i, 0)
    return pl.pallas_call(
        _s5_out_body,
        grid=(bsz, nt),
        in_specs=[pl.BlockSpec((tm, d), row),
                  pl.BlockSpec((tm, d), row),
                  pl.BlockSpec((tm, d), row),
                  pl.BlockSpec((1, d), lambda b, i: (0, 0)),
                  pl.BlockSpec((d, 2 * d), lambda b, i: (0, 0)),
                  pl.BlockSpec((1, 1, d), lambda b, i: (b, 0, 0))],
        out_specs=pl.BlockSpec((tm, d), row),
        out_shape=jax.ShapeDtypeStruct((bsz * seq, d), F32),
        compiler_params=_params("parallel", "parallel"),
        name="s5_out",
    )(y2, u2, h2, d_skip, w_out, gate)


def _s5_tables(lam_re, lam_im, log_dt, b_re, b_im, c_re, c_im):
    dt = jnp.exp(log_dt.astype(F32))[:, None]
    lr, li = lam_re.astype(F32), lam_im.astype(F32)
    mag = jnp.exp(lr * dt)
    a_re = mag * jnp.cos(li * dt)
    a_im = mag * jnp.sin(li * dt)
    den = lr * lr + li * li
    coef_re = ((a_re - 1.0) * lr + a_im * li) / den
    coef_im = (a_im * lr - (a_re - 1.0) * li) / den
    br_, bi_ = b_re.astype(F32), b_im.astype(F32)
    bbar_re = coef_re[..., None] * br_ - coef_im[..., None] * bi_
    bbar_im = coef_re[..., None] * bi_ + coef_im[..., None] * br_
    g, p, c = bbar_re.shape
    gb = S5_GROUPS_PER_BLOCK
    nblk = g // gb
    eye = jnp.eye(gb, dtype=F32)

    def in_blocks(m):
        return jnp.einsum('kgpc,gh->kgchp', m.reshape(nblk, gb, p, c), eye).reshape(nblk, gb * c, gb * p)

    def out_blocks(m):
        return jnp.einsum('kgcp,gh->kgphc', m.reshape(nblk, gb, c, p), eye).reshape(nblk, gb * p, gb * c)

    bblk = jnp.concatenate([in_blocks(bbar_re), in_blocks(bbar_im)], axis=-1).astype(BF16)
    cblk = jnp.concatenate([out_blocks(c_re.astype(F32)), -out_blocks(c_im.astype(F32))], axis=1).astype(BF16)
    return bblk, cblk, a_re.reshape(nblk, 1, gb * p), a_im.reshape(nblk, 1, gb * p)


def _router_body(h_ref, g_ref, sh_ref, sc_ref, wr_ref, br_ref, tri_ref,
                 x_ref, mt_ref, cnt_ref, carry_ref):
    tm, d = h_ref.shape
    ne, ng, epg = N_EXPERTS, N_EXPERT_GROUPS, EXPERTS_PER_GROUP

    @pl.when(pl.program_id(0) == 0)
    def _():
        carry_ref[...] = jnp.zeros_like(carry_ref)

    hn = _rms_mod(h_ref[...], g_ref[...], sh_ref[0], sc_ref[0])
    logits = jnp.dot(hn, wr_ref[...], precision=lax.Precision.HIGHEST,
                     preferred_element_type=F32) + br_ref[...]
    lane = lax.broadcasted_iota(I32, logits.shape, 1).astype(F32)
    big = jnp.float32(1e9)
    ninf = jnp.float32(-jnp.inf)

    gmask = (lane >= ne) & (lane < ne + ng)
    gmax = jnp.max(jnp.where(gmask, logits, ninf), axis=-1, keepdims=True)
    gsum = jnp.sum(jnp.where(gmask, jnp.exp(logits - gmax), 0.0), axis=-1, keepdims=True)
    p_g = 1.0 / gsum
    gidx = jnp.min(jnp.where(gmask & (logits == gmax), lane - ne, big), axis=-1, keepdims=True)

    emask = (lane < ne) & (jnp.floor(lane / epg) == gidx)
    v1 = jnp.max(jnp.where(emask, logits, ninf), axis=-1, keepdims=True)
    i1 = jnp.min(jnp.where(emask & (logits == v1), lane, big), axis=-1, keepdims=True)
    emask2 = emask & (lane != i1)
    v2 = jnp.max(jnp.where(emask2, logits, ninf), axis=-1, keepdims=True)
    i2 = jnp.min(jnp.where(emask2 & (logits == v2), lane, big), axis=-1, keepdims=True)
    e21 = jnp.exp(v2 - v1)
    w1 = p_g / (1.0 + e21)
    w2 = p_g * e21 / (1.0 + e21)

    first_lo = i1 < i2
    e_lo = jnp.where(first_lo, i1, i2)
    e_hi = jnp.where(first_lo, i2, i1)
    w_lo = jnp.where(first_lo, w1, w2)
    w_hi = jnp.where(first_lo, w2, w1)
    bucket = gidx * (epg * epg) + (e_lo - gidx * epg) * epg + (e_hi - gidx * epg)

    lane_b = lax.broadcasted_iota(I32, (tm, N_BUCKETS), 1).astype(F32)
    onehot = (lane_b == bucket).astype(F32)
    prefix = jnp.dot(tri_ref[...], onehot.astype(BF16), preferred_element_type=F32)
    carry = carry_ref[...]
    rank = jnp.sum(onehot * (prefix + carry), axis=-1, keepdims=True) - 1.0
    new_carry = carry + prefix[tm - 1:tm, :]
    carry_ref[...] = new_carry
    cnt_ref[...] = new_carry

    mlane = lax.broadcasted_iota(I32, (tm, META_LANES), 1)
    meta = jnp.where(mlane == 0, bucket,
           jnp.where(mlane == 1, rank,
           jnp.where(mlane == 2, w_lo,
           jnp.where(mlane == 3, w_hi, 0.0))))
    x_ref[:, :d] = hn
    x_ref[:, d:] = meta
    mt_ref[0] = meta.T[:META_ROWS, :]


def _router(h2, g, shift, scale, w_r, b_r, tri, nt_per_batch, tm):
    t, d = h2.shape
    return pl.pallas_call(
        _router_body,
        grid=(t // tm,),
        in_specs=[pl.BlockSpec((tm, d), lambda i: (i, 0)),
                  pl.BlockSpec((1, d), lambda i: (0, 0)),
                  pl.BlockSpec((1, 1, d), lambda i: (i // nt_per_batch, 0, 0)),
                  pl.BlockSpec((1, 1, d), lambda i: (i // nt_per_batch, 0, 0)),
                  pl.BlockSpec((d, LANES), lambda i: (0, 0)),
                  pl.BlockSpec((1, LANES), lambda i: (0, 0)),
                  pl.BlockSpec((tm, tm), lambda i: (0, 0))],
        out_specs=[pl.BlockSpec((tm, d + META_LANES), lambda i: (i, 0)),
                   pl.BlockSpec((1, META_ROWS, tm), lambda i: (i, 0, 0)),
                   pl.BlockSpec((1, N_BUCKETS), lambda i: (0, 0))],
        out_shape=[jax.ShapeDtypeStruct((t, d + META_LANES), F32),
                   jax.ShapeDtypeStruct((t // tm, META_ROWS, tm), F32),
                   jax.ShapeDtypeStruct((1, N_BUCKETS), F32)],
        scratch_shapes=[pltpu.VMEM((1, N_BUCKETS), F32)],
        compiler_params=_params("arbitrary"),
        name="moe_router",
    )(h2, g, shift, scale, w_r, b_r, tri)


def _plan_body(cnt_ref, mt_ref, pos_ref, maps_ref, start_ref, *, te, ntp):
    nb = N_BUCKETS
    epg = EXPERTS_PER_GROUP

    @pl.when(pl.program_id(0) == 0)
    def _():
        tiles = jnp.floor((cnt_ref[...] + (te - 1)) / te)
        tiles8 = jnp.broadcast_to(tiles, (SUBLANES, nb)).astype(BF16)
        r = lax.broadcasted_iota(I32, (nb, nb), 0)
        c = lax.broadcasted_iota(I32, (nb, nb), 1)
        nt_dims = (((1,), (1,)), ((), ()))
        start = lax.dot_general((c < r).astype(BF16), tiles8, nt_dims, preferred_element_type=F32)[:, :1]
        end = lax.dot_general((c <= r).astype(BF16), tiles8, nt_dims, preferred_element_type=F32)[:, :1]
        start_ref[...] = start * te
        n_tiles = end[nb - 1:nb, :]
        j = lax.broadcasted_iota(I32, (1, ntp), 1).astype(F32)
        tb = jnp.minimum(j, jnp.maximum(n_tiles - 1.0, 0.0))
        bucket = jnp.sum((end <= tb).astype(F32), axis=0, keepdims=True)
        grp = jnp.floor(bucket / (epg * epg))
        within = bucket - grp * (epg * epg)
        lo = jnp.floor(within / epg)
        e_lo = grp * epg + lo
        e_hi = grp * epg + (within - lo * epg)
        valid = (j < n_tiles).astype(F32)
        row = lax.broadcasted_iota(I32, (SUBLANES, ntp), 0)
        maps = jnp.where(row == 0, tb, jnp.where(row == 1, e_lo, jnp.where(row == 2, e_hi,
               jnp.where(row == 3, valid, 0.0))))
        maps_ref[...] = maps.astype(I32)

    tm = mt_ref.shape[-1]
    bucket_row = mt_ref[0, 0:1, :]
    rank_row = mt_ref[0, 1:2, :]
    rb = lax.broadcasted_iota(I32, (nb, tm), 0).astype(F32)
    pos = jnp.sum(jnp.where(rb == bucket_row, start_ref[...], 0.0), axis=0, keepdims=True) + rank_row
    pos_ref[0] = pos.astype(I32)


def _plan(counts, meta_t, te, nt_max):
    n_tt, _, tm = meta_t.shape
    ntp = ((nt_max + LANES - 1) // LANES) * LANES
    body = functools.partial(_plan_body, te=te, ntp=ntp)
    return pl.pallas_call(
        body,
        grid=(n_tt,),
        in_specs=[pl.BlockSpec((1, N_BUCKETS), lambda i: (0, 0)),
                  pl.BlockSpec((1, META_ROWS, tm), lambda i: (i, 0, 0))],
        out_specs=[pl.BlockSpec((1, 1, tm), lambda i: (i, 0, 0)),
                   pl.BlockSpec((SUBLANES, ntp), lambda i: (0, 0))],
        out_shape=[jax.ShapeDtypeStruct((n_tt, 1, tm), I32),
                   jax.ShapeDtypeStruct((SUBLANES, ntp), I32)],
        scratch_shapes=[pltpu.VMEM((N_BUCKETS, 1), F32)],
        compiler_params=_params("arbitrary"),
        name="moe_plan",
    )(counts, meta_t)


def _row_copy(src_ref, src_row, dst_ref, dst_row, sem):
    return pltpu.make_async_copy(src_ref.at[pl.ds(src_row, 1)], dst_ref.at[pl.ds(dst_row, 1)], sem)


def _dispatch_body(pos_ref, x_ref, xs_in_ref, xs_ref, sem):
    del xs_in_ref
    tm = x_ref.shape[0]

    def issue(r, c):
        _row_copy(x_ref, r, xs_ref, pos_ref[0, 0, r], sem).start()
        return c

    lax.fori_loop(0, tm, issue, 0)
    pltpu.make_async_copy(x_ref, xs_ref.at[pl.ds(0, tm)], sem).wait()


def _dispatch(xrow, pos3, t_pad, tm):
    t, w = xrow.shape
    zeros = jnp.zeros((t_pad, w), F32)
    return pl.pallas_call(
        _dispatch_body,
        grid=(t // tm,),
        in_specs=[pl.BlockSpec((1, 1, tm), lambda i: (i, 0, 0), memory_space=pltpu.SMEM),
                  pl.BlockSpec((tm, w), lambda i: (i, 0)),
                  pl.BlockSpec(memory_space=pl.ANY)],
        out_specs=pl.BlockSpec(memory_space=pl.ANY),
        out_shape=jax.ShapeDtypeStruct((t_pad, w), F32),
        scratch_shapes=[pltpu.SemaphoreType.DMA(())],
        input_output_aliases={2: 0},
        compiler_params=_params("arbitrary"),
        name="moe_dispatch",
    )(pos3, xrow, zeros)


def _expert_body(tb_ref, elo_ref, ehi_ref, valid_ref, x_ref,
                 w1a_ref, w3a_ref, w2a_ref, w1b_ref, w3b_ref, w2b_ref, o_ref):
    del tb_ref, elo_ref, ehi_ref
    d = o_ref.shape[1]

    @pl.when(valid_ref[pl.program_id(0)] == 0)
    def _():
        o_ref[...] = jnp.zeros_like(o_ref)

    @pl.when(valid_ref[pl.program_id(0)] == 1)
    def _():
        x = x_ref[:, :d].astype(BF16)

        def ffn(w1_ref, w3_ref, w2_ref, wt):
            a = jnp.dot(x, w1_ref[...].astype(BF16), preferred_element_type=F32)
            b = jnp.dot(x, w3_ref[...].astype(BF16), preferred_element_type=F32)
            mid = (a * jax.nn.sigmoid(a)) * b
            y = jnp.dot(mid.astype(BF16), w2_ref[...].astype(BF16), preferred_element_type=F32)
            return wt * y

        o_ref[...] = (ffn(w1a_ref, w3a_ref, w2a_ref, x_ref[:, d + 2:d + 3])
                      + ffn(w1b_ref, w3b_ref, w2b_ref, x_ref[:, d + 3:d + 4]))


def _experts(xs, w1, w3, w2, tile_block, e_lo, e_hi, valid, te):
    t_pad, w = xs.shape
    n_e, d, f = w1.shape
    nt = t_pad // te
    x_spec = pl.BlockSpec((te, w), lambda j, tb, lo, hi, v: (tb[j], 0))
    up_lo = pl.BlockSpec((None, d, f), lambda j, tb, lo, hi, v: (lo[j], 0, 0))
    dn_lo = pl.BlockSpec((None, f, d), lambda j, tb, lo, hi, v: (lo[j], 0, 0))
    up_hi = pl.BlockSpec((None, d, f), lambda j, tb, lo, hi, v: (hi[j], 0, 0))
    dn_hi = pl.BlockSpec((None, f, d), lambda j, tb, lo, hi, v: (hi[j], 0, 0))
    return pl.pallas_call(
        _expert_body,
        grid_spec=pltpu.PrefetchScalarGridSpec(
            num_scalar_prefetch=4,
            grid=(nt,),
            in_specs=[x_spec, up_lo, up_lo, dn_lo, up_hi, up_hi, dn_hi],
            out_specs=pl.BlockSpec((te, d), lambda j, tb, lo, hi, v: (j, 0))),
        out_shape=jax.ShapeDtypeStruct((t_pad, d), F32),
        compiler_params=_params("arbitrary"),
        name="moe_experts",
    )(tile_block, e_lo, e_hi, valid, xs, w1, w3, w2, w1, w3, w2)


def _combine_body(pos_ref, ys_ref, h_ref, gate_ref, o_ref, ybuf, sem):
    tm = h_ref.shape[0]

    def issue(r, c):
        _row_copy(ys_ref, pos_ref[0, 0, r], ybuf, r, sem).start()
        return c

    lax.fori_loop(0, tm, issue, 0)
    pltpu.make_async_copy(ys_ref.at[pl.ds(0, tm)], ybuf, sem).wait()
    o_ref[...] = h_ref[...] + gate_ref[0] * ybuf[...]


def _combine(ys, pos3, h2, gate, nt_per_batch, tm):
    t, d = h2.shape
    return pl.pallas_call(
        _combine_body,
        grid=(t // tm,),
        in_specs=[pl.BlockSpec((1, 1, tm), lambda i: (i, 0, 0), memory_space=pltpu.SMEM),
                  pl.BlockSpec(memory_space=pl.ANY),
                  pl.BlockSpec((tm, d), lambda i: (i, 0)),
                  pl.BlockSpec((1, 1, d), lambda i: (i // nt_per_batch, 0, 0))],
        out_specs=pl.BlockSpec((tm, d), lambda i: (i, 0)),
        out_shape=jax.ShapeDtypeStruct((t, d), F32),
        scratch_shapes=[pltpu.VMEM((tm, d), F32), pltpu.SemaphoreType.DMA(())],
        compiler_params=_params("arbitrary"),
        name="moe_combine",
    )(pos3, ys, h2, gate)


def _moe(h2, g, shift, scale, gate, wg, bg, we, be, w1, w3, w2, bsz, seq, tm, te):
    t, d = h2.shape
    nt_per_batch = seq // tm
    ne, ng = N_EXPERTS, N_EXPERT_GROUPS
    w_r = jnp.zeros((d, LANES), F32).at[:, :ne].set(we).at[:, ne:ne + ng].set(wg)
    b_r = jnp.zeros((1, LANES), F32).at[0, :ne].set(be).at[0, ne:ne + ng].set(bg)
    tri = jnp.asarray(np.tril(np.ones((tm, tm), np.float32)), BF16)
    xrow, meta_t, counts = _router(h2, g, shift, scale, w_r, b_r, tri, nt_per_batch, tm)
    nt_max = t // te + N_PAIR_BUCKETS
    pos3, maps = _plan(counts, meta_t, te, nt_max)
    xs = _dispatch(xrow, pos3, nt_max * te, tm)
    ys = _experts(xs, w1, w3, w2, maps[0, :nt_max], maps[1, :nt_max], maps[2, :nt_max], maps[3, :nt_max], te)
    return _combine(ys, pos3, h2, gate, nt_per_batch, tm)


def _log_sigmoid(x):
    return jnp.minimum(x, 0.0) - jnp.log1p(jnp.exp(-jnp.abs(x)))


def _aug_tables(n_heads):
    width = (n_heads // 2) * LANES
    pk = np.zeros((F_SPLIT * LANES, width), np.float32)
    pq = np.zeros((F_SPLIT * LANES, width), np.float32)
    ck = np.zeros((1, width), np.float32)
    cq = np.zeros((1, width), np.float32)
    for h in range(n_heads):
        base = (h // 2) * LANES + (h % 2) * AUG_LANES_PER_HEAD
        for j in range(F_SPLIT):
            pk[j * LANES + h, base + j] = -1.0
            pq[j * LANES + h, base + F_SPLIT + j] = 1.0
            ck[0, base + F_SPLIT + j] = 1.0
            cq[0, base + j] = 1.0
    return jnp.asarray(pk, BF16), jnp.asarray(pq, BF16), jnp.asarray(ck), jnp.asarray(cq)


def _kv_body(h_ref, g_ref, sh_ref, sc_ref, wk_ref, wvt_ref, wf_ref, fb_ref, kng_ref,
             pk_ref, pq_ref, ck_ref, cq_ref, k_ref, vt_ref, ka_ref, qa_ref, carry_ref):
    tm = h_ref.shape[0]

    @pl.when(pl.program_id(1) == 0)
    def _():
        carry_ref[...] = jnp.zeros_like(carry_ref)

    hn = _rms_mod(h_ref[...], g_ref[...], sh_ref[0], sc_ref[0])
    hb = hn.astype(BF16)
    k = jnp.dot(hb, wk_ref[...], preferred_element_type=F32)
    k_ref[...] = _head_rms(k, kng_ref[...]).astype(BF16)
    nt_dims = (((1,), (1,)), ((), ()))
    vt_ref[...] = lax.dot_general(wvt_ref[...], hb, nt_dims, preferred_element_type=F32).astype(BF16)
    fz = jnp.dot(hn, wf_ref[...], precision=lax.Precision.HIGHEST,
                 preferred_element_type=F32) + fb_ref[...]
    c = _log_sigmoid(fz)
    row = lax.broadcasted_iota(I32, c.shape, 0)
    shift = 1
    while shift < tm:
        c = c + jnp.where(row >= shift, pltpu.roll(c, shift, 0), 0.0)
        shift *= 2
    f = c + carry_ref[...]
    carry_ref[...] = f[tm - 1:tm, :]

    f2 = f * LOG2E
    hi = f2.astype(BF16)
    r1 = f2 - hi.astype(F32)
    mid = r1.astype(BF16)
    lo = (r1 - mid.astype(F32)).astype(BF16)
    pieces = jnp.concatenate([hi, mid, lo], axis=-1)
    ka_ref[...] = (jnp.dot(pieces, pk_ref[...], preferred_element_type=F32) + ck_ref[...]).astype(BF16)
    qa_ref[...] = (jnp.dot(pieces, pq_ref[...], preferred_element_type=F32) + cq_ref[...]).astype(BF16)


def _shared_kv(h2, g, shift, scale, wk, wvt, wf, fb, kng, bsz, seq, tm):
    t, d = h2.shape
    nt = seq // tm
    n_heads = d // HEAD_DIM
    aw = (n_heads // 2) * LANES
    pk, pq, ck, cq = _aug_tables(n_heads)
    row = lambda b, i: (b * nt + i, 0)
    const = lambda b, i: (0, 0)
    return pl.pallas_call(
        _kv_body,
        grid=(bsz, nt),
        in_specs=[pl.BlockSpec((tm, d), row),
                  pl.BlockSpec((1, d), const),
                  pl.BlockSpec((1, 1, d), lambda b, i: (b, 0, 0)),
                  pl.BlockSpec((1, 1, d), lambda b, i: (b, 0, 0)),
                  pl.BlockSpec((d, d), const),
                  pl.BlockSpec((d, d), const),
                  pl.BlockSpec((d, LANES), const),
                  pl.BlockSpec((1, LANES), const),
                  pl.BlockSpec((1, d), const),
                  pl.BlockSpec((F_SPLIT * LANES, aw), const),
                  pl.BlockSpec((F_SPLIT * LANES, aw), const),
                  pl.BlockSpec((1, aw), const),
                  pl.BlockSpec((1, aw), const)],
        out_specs=[pl.BlockSpec((tm, d), row),
                   pl.BlockSpec((d, tm), lambda b, i: (b, i)),
                   pl.BlockSpec((tm, aw), row),
                   pl.BlockSpec((tm, aw), row)],
        out_shape=[jax.ShapeDtypeStruct((t, d), BF16), jax.ShapeDtypeStruct((bsz * d, seq), BF16),
                   jax.ShapeDtypeStruct((t, aw), BF16), jax.ShapeDtypeStruct((t, aw), BF16)],
        scratch_shapes=[pltpu.VMEM((1, LANES), F32)],
        compiler_params=_params("parallel", "arbitrary"),
        name="shared_kv",
    )(h2, g, shift, scale, wk, wvt, wf, fb, kng, pk, pq, ck, cq)


def _qg_body(h_ref, g_ref, sh_ref, sc_ref, wq_ref, wg_ref, qng_ref, q_ref, og_ref):
    hn = _rms_mod(h_ref[...], g_ref[...], sh_ref[0], sc_ref[0])
    hb = hn.astype(BF16)
    q = jnp.dot(hb, wq_ref[...], preferred_element_type=F32)
    q_ref[...] = (_head_rms(q, qng_ref[...]) * (HEAD_DIM ** -0.5 * LOG2E)).astype(BF16)
    og_ref[...] = jnp.dot(hb, wg_ref[...], preferred_element_type=F32)


def _fox_qg(h2, g, shift, scale, wq, wg, qng, bsz, seq, tm):
    t, d = h2.shape
    nt = seq // tm
    row = lambda b, i: (b * nt + i, 0)
    const = lambda b, i: (0, 0)
    return pl.pallas_call(
        _qg_body,
        grid=(bsz, nt),
        in_specs=[pl.BlockSpec((tm, d), row),
                  pl.BlockSpec((1, d), const),
                  pl.BlockSpec((1, 1, d), lambda b, i: (b, 0, 0)),
                  pl.BlockSpec((1, 1, d), lambda b, i: (b, 0, 0)),
                  pl.BlockSpec((d, d), const),
                  pl.BlockSpec((d, d), const),
                  pl.BlockSpec((1, d), const)],
        out_specs=[pl.BlockSpec((tm, d), row), pl.BlockSpec((tm, d), row)],
        out_shape=[jax.ShapeDtypeStruct((t, d), BF16), jax.ShapeDtypeStruct((t, d), F32)],
        compiler_params=_params("parallel", "parallel"),
        name="fox_qg",
    )(h2, g, shift, scale, wq, wg, qng)


def _attn_body(q_ref, qa_ref, k_ref, ka_ref, vt_ref, o_ref, m_ref, l_ref, acc_ref, *, tq, hp):
    qi = pl.program_id(2)
    n_slabs = hp // 2
    lane_q = lax.broadcasted_iota(I32, (tq, LANES), 1)
    lo_q = lane_q < HEAD_DIM
    lo_v = lax.broadcasted_iota(I32, (LANES, tq), 0) < HEAD_DIM
    nt_dims = (((1,), (1,)), ((), ()))
    causal = lax.broadcasted_iota(I32, (tq, tq), 0) <= lax.broadcasted_iota(I32, (tq, tq), 1)

    qcats = []
    for h in range(hp):
        sl, hh = h // 2, h % 2
        q2 = q_ref[:, sl * LANES:(sl + 1) * LANES]
        qa = qa_ref[:, sl * LANES:(sl + 1) * LANES]
        own = lo_q if hh == 0 else jnp.logical_not(lo_q)
        own_a = (lane_q >= hh * AUG_LANES_PER_HEAD) & (lane_q < (hh + 1) * AUG_LANES_PER_HEAD)
        qcats.append(jnp.concatenate([jnp.where(own, q2, jnp.zeros_like(q2)),
                                      jnp.where(own_a, qa, jnp.zeros_like(qa))], axis=-1))

    m_ref[...] = jnp.full(m_ref.shape, NEG, F32)
    l_ref[...] = jnp.zeros(l_ref.shape, F32)
    acc_ref[...] = jnp.zeros(acc_ref.shape, F32)

    def block(kb, masked):
        s0 = pl.multiple_of(kb * tq, tq)
        for sl in range(n_slabs):
            kcat = jnp.concatenate([k_ref[pl.ds(s0, tq), sl * LANES:(sl + 1) * LANES],
                                    ka_ref[pl.ds(s0, tq), sl * LANES:(sl + 1) * LANES]], axis=-1)
            vt = vt_ref[sl * LANES:(sl + 1) * LANES, pl.ds(s0, tq)]
            vts = (jnp.where(lo_v, vt, jnp.zeros_like(vt)), jnp.where(lo_v, jnp.zeros_like(vt), vt))
            pv, alphas = None, []
            for hh in range(2):
                h = sl * 2 + hh
                st = lax.dot_general(kcat, qcats[h], nt_dims, preferred_element_type=F32)
                if masked:
                    st = jnp.where(causal, st, NEG)
                m_old = m_ref[h]
                m_new = jnp.maximum(m_old, jnp.max(st, axis=0, keepdims=True))
                alpha = jnp.exp2(m_old - m_new)
                p = jnp.exp2(st - m_new)
                l_ref[h] = alpha * l_ref[h] + jnp.sum(p, axis=0, keepdims=True)
                m_ref[h] = m_new
                part = jnp.dot(vts[hh], p.astype(BF16), preferred_element_type=F32)
                pv = part if pv is None else pv + part
                alphas.append(alpha)
            acc_ref[sl] = jnp.where(lo_v, alphas[0], alphas[1]) * acc_ref[sl] + pv

    def unmasked(kb, c):
        block(kb, False)
        return c

    lax.fori_loop(0, qi, unmasked, 0)
    block(qi, True)
    for sl in range(n_slabs):
        l2 = jnp.where(lo_v, l_ref[2 * sl], l_ref[2 * sl + 1])
        o_ref[:, sl * LANES:(sl + 1) * LANES] = (acc_ref[sl] / l2).T


def _fox_attention(q, qaug, k, kaug, vt, bsz, seq, tq, hp):
    t, d = q.shape
    w = hp * HEAD_DIM
    n_grp = d // w
    nq = seq // tq
    body = functools.partial(_attn_body, tq=tq, hp=hp)
    qrow = lambda b, j, i: (b * nq + i, j)
    krow = lambda b, j, i: (b, j)
    return pl.pallas_call(
        body,
        grid=(bsz, n_grp, nq),
        in_specs=[pl.BlockSpec((tq, w), qrow),
                  pl.BlockSpec((tq, w), qrow),
                  pl.BlockSpec((seq, w), krow),
                  pl.BlockSpec((seq, w), krow),
                  pl.BlockSpec((w, seq), lambda b, j, i: (b * n_grp + j, 0))],
        out_specs=pl.BlockSpec((tq, w), qrow),
        out_shape=jax.ShapeDtypeStruct((t, d), F32),
        scratch_shapes=[pltpu.VMEM((hp, 1, tq), F32), pltpu.VMEM((hp, 1, tq), F32),
                        pltpu.VMEM((hp // 2, LANES, tq), F32)],
        compiler_params=_params("parallel", "parallel", "arbitrary"),
        name="fox_attention",
    )(q, qaug, k, kaug, vt)


def _fox_out_body(o_ref, og_ref, h_ref, w_ref, gate_ref, out_ref):
    z = o_ref[...] * jax.nn.sigmoid(og_ref[...])
    out_ref[...] = h_ref[...] + gate_ref[0] * jnp.dot(z.astype(BF16), w_ref[...], preferred_element_type=F32)


def _fox_out(o, og, h2, w_o, gate, bsz, seq, tm):
    t, d = h2.shape
    nt = seq // tm
    row = lambda b, i: (b * nt + i, 0)
    return pl.pallas_call(
        _fox_out_body,
        grid=(bsz, nt),
        in_specs=[pl.BlockSpec((tm, d), row), pl.BlockSpec((tm, d), row), pl.BlockSpec((tm, d), row),
                  pl.BlockSpec((d, d), lambda b, i: (0, 0)),
                  pl.BlockSpec((1, 1, d), lambda b, i: (b, 0, 0))],
        out_specs=pl.BlockSpec((tm, d), row),
        out_shape=jax.ShapeDtypeStruct((t, d), F32),
        compiler_params=_params("parallel", "parallel"),
        name="fox_out",
    )(o, og, h2, w_o, gate)


def _tiles(seq):
    tm = min(512, seq)
    te = min(256, seq)
    tq = min(512, seq)
    tc = min(32, seq)
    return tm, te, tq, tc


def kernel(x, c, ln_g, ada_w, ada_b, s5_w_in, s5_lambda_re, s5_lambda_im, s5_log_dt, s5_b_re, s5_b_im,
           s5_c_re, s5_c_im, s5_d, s5_w_out, kv_g, kv_ada_w, kv_ada_b, kv_w, kv_fb, k_norm_g,
           fox_w_qg, fox_q_norm_g, fox_w_o, moe_wg, moe_bg, moe_we, moe_be, moe_w1, moe_w3, moe_w2):
    bsz, seq, d = x.shape
    depth = ln_g.shape[0]
    n_a = s5_w_in.shape[0]
    n_heads = d // HEAD_DIM
    tm, te, tq, tc = _tiles(seq)

    mods = _adaln(c, ada_w.reshape(depth * 2, d, 3 * d), ada_b.reshape(depth * 2, 1, 3 * d))
    mods = mods.reshape(depth, 2, bsz, 3, 1, d)
    kv_mods = _adaln(c, kv_ada_w[None], kv_ada_b[None, None]).reshape(bsz, 2, 1, d)

    h = x.reshape(bsz * seq, d)
    k = kaug = qaug = vt = None
    for l in range(depth):
        shift, scale, gate = mods[l, 0, :, 0], mods[l, 0, :, 1], mods[l, 0, :, 2]
        g = ln_g[l, 0][None]
        if l < n_a:
            u2 = _s5_in(h, g, shift, scale, s5_w_in[l].astype(BF16), bsz, seq, tm)
            bblk, cblk, a_re, a_im = _s5_tables(s5_lambda_re[l], s5_lambda_im[l], s5_log_dt[l],
                                                s5_b_re[l], s5_b_im[l], s5_c_re[l], s5_c_im[l])
            y2 = _s5_scan(u2, bblk, cblk, a_re, a_im, bsz, seq, tc)
            h = _s5_out(y2, u2, h, s5_d[l][None], s5_w_out[l].astype(BF16), gate, bsz, seq, tm)
        else:
            j = l - n_a
            qng = jnp.tile(fox_q_norm_g[j], n_heads)[None]
            q, og = _fox_qg(h, g, shift, scale, fox_w_qg[j][:, :d].astype(BF16),
                            fox_w_qg[j][:, d:].astype(BF16), qng, bsz, seq, tm)
            o = _fox_attention(q, qaug, k, kaug, vt, bsz, seq, tq, ATTN_HEADS_PER_STEP)
            h = _fox_out(o, og, h, fox_w_o[j].astype(BF16), gate, bsz, seq, tm)

        shift, scale, gate = mods[l, 1, :, 0], mods[l, 1, :, 1], mods[l, 1, :, 2]
        h = _moe(h, ln_g[l, 1][None], shift, scale, gate, moe_wg[l], moe_bg[l], moe_we[l], moe_be[l],
                 moe_w1[l], moe_w3[l], moe_w2[l], bsz, seq, tm, te)

        if l == n_a - 1:
            wf = jnp.zeros((d, LANES), F32).at[:, :n_heads].set(kv_w[:, 2 * d:])
            fb = jnp.zeros((1, LANES), F32).at[0, :n_heads].set(kv_fb)
            kng = jnp.tile(k_norm_g, n_heads)[None]
            k, vt, kaug, qaug = _shared_kv(h, kv_g[None], kv_mods[:, 0], kv_mods[:, 1],
                                           kv_w[:, :d].astype(BF16), kv_w[:, d:2 * d].T.astype(BF16),
                                           wf, fb, kng, bsz, seq, tm)
    return h.reshape(bsz, seq, d)
```

```python
import functools
import math

import numpy as np
import jax
import jax.numpy as jnp
from jax import lax
from jax.experimental import pallas as pl
from jax.experimental.pallas import tpu as pltpu

F32 = jnp.float32
BF16 = jnp.bfloat16
I32 = jnp.int32

EPS = 1e-6
NEG = -1e30
LOG2E = math.log2(math.e)
LANES = 128
SUBLANES = 8
VMEM_LIMIT_BYTES = 56 * 1024 * 1024

S5_GROUPS_PER_BLOCK = 16
N_EXPERT_GROUPS = 4
EXPERTS_PER_GROUP = 8
N_EXPERTS = N_EXPERT_GROUPS * EXPERTS_PER_GROUP
N_BUCKETS = N_EXPERT_GROUPS * EXPERTS_PER_GROUP * EXPERTS_PER_GROUP
N_PAIR_BUCKETS = N_EXPERT_GROUPS * (EXPERTS_PER_GROUP * (EXPERTS_PER_GROUP - 1) // 2)
META_LANES = LANES
META_ROWS = SUBLANES
ROW_DMA_UNROLL = 8
HEAD_DIM = 64
ATTN_HEADS_PER_STEP = 4
F_SPLIT = 3
AUG_LANES_PER_HEAD = 2 * F_SPLIT


def _params(*sem):
    return pltpu.CompilerParams(dimension_semantics=sem, vmem_limit_bytes=VMEM_LIMIT_BYTES)


def _rms_mod(x, g, shift, scale):
    ms = jnp.mean(x * x, axis=-1, keepdims=True)
    y = x * lax.rsqrt(ms + EPS) * g
    return y * (1.0 + scale) + shift


def _split_bf16(w):
    hi = w.astype(BF16)
    return hi, (w - hi.astype(F32)).astype(BF16)


def _dot_3pass(x, w_hi_ref, w_lo_ref):
    x_hi, x_lo = _split_bf16(x)
    w_hi = w_hi_ref[...]
    return (jnp.dot(x_hi, w_hi, preferred_element_type=F32)
            + jnp.dot(x_lo, w_hi, preferred_element_type=F32)
            + jnp.dot(x_hi, w_lo_ref[...], preferred_element_type=F32))


def _head_rms(x, g):
    tm, d = x.shape
    lane = lax.broadcasted_iota(I32, (tm, LANES), 1)
    lo = lane < HEAD_DIM
    outs = []
    for j in range(d // LANES):
        s = x[:, j * LANES:(j + 1) * LANES]
        sq = s * s
        s_lo = jnp.sum(jnp.where(lo, sq, 0.0), axis=-1, keepdims=True)
        s_hi = jnp.sum(jnp.where(lo, 0.0, sq), axis=-1, keepdims=True)
        r = jnp.where(lo, lax.rsqrt(s_lo / HEAD_DIM + EPS), lax.rsqrt(s_hi / HEAD_DIM + EPS))
        outs.append(s * r)
    return jnp.concatenate(outs, axis=-1) * g


def _adaln_body(c_ref, w_ref, b_ref, o_ref):
    c = c_ref[...]
    s = c * jax.nn.sigmoid(c)
    o_ref[0] = jnp.dot(s, w_ref[0], preferred_element_type=F32) + b_ref[0]


def _adaln(c, w, b):
    n_sets, d, n = w.shape
    bsz = c.shape[0]
    tn = 512 if n % 512 == 0 else n
    return pl.pallas_call(
        _adaln_body,
        grid=(n_sets, n // tn),
        in_specs=[pl.BlockSpec((bsz, d), lambda s, j: (0, 0)),
                  pl.BlockSpec((1, d, tn), lambda s, j: (s, 0, j)),
                  pl.BlockSpec((1, 1, tn), lambda s, j: (s, 0, j))],
        out_specs=pl.BlockSpec((1, bsz, tn), lambda s, j: (s, 0, j)),
        out_shape=jax.ShapeDtypeStruct((n_sets, bsz, n), F32),
        compiler_params=_params("parallel", "parallel"),
        name="adaln",
    )(c, w, b)


def _s5_in_body(x_ref, g_ref, sh_ref, sc_ref, w_ref, u_ref):
    hn = _rms_mod(x_ref[...], g_ref[...], sh_ref[0], sc_ref[0])
    u_ref[...] = jnp.dot(hn.astype(BF16), w_ref[...], preferred_element_type=F32)


def _s5_in(x2, g, shift, scale, w_in, bsz, seq, tm):
    d = x2.shape[1]
    nt = seq // tm
    row = lambda b, i: (b * nt + i, 0)
    return pl.pallas_call(
        _s5_in_body,
        grid=(bsz, nt),
        in_specs=[pl.BlockSpec((tm, d), row),
                  pl.BlockSpec((1, d), lambda b, i: (0, 0)),
                  pl.BlockSpec((1, 1, d), lambda b, i: (b, 0, 0)),
                  pl.BlockSpec((1, 1, d), lambda b, i: (b, 0, 0)),
                  pl.BlockSpec((d, d), lambda b, i: (0, 0))],
        out_specs=pl.BlockSpec((tm, d), row),
        out_shape=jax.ShapeDtypeStruct((bsz * seq, d), F32),
        compiler_params=_params("parallel", "parallel"),
        name="s5_in",
    )(x2, g, shift, scale, w_in)


def _s5_scan_body(u_hbm, bb_ref, cb_ref, are_ref, aim_ref, y_hbm,
                  ubuf, ybuf, bu_ref, st_ref, sem_in, sem_out, *, tc, nblk, sw, seq, n_chunks):
    bsz = SUBLANES
    cw = S5_GROUPS_PER_BLOCK * 16
    i = pl.program_id(0)
    slot = lax.rem(i, 2)

    def in_copy(chunk, sl, b):
        return pltpu.make_async_copy(u_hbm.at[pl.ds(b * seq + chunk * tc, tc)],
                                     ubuf.at[sl, :, b, :], sem_in.at[sl])

    def out_copy(chunk, sl, b):
        return pltpu.make_async_copy(ybuf.at[sl, :, b, :],
                                     y_hbm.at[pl.ds(b * seq + chunk * tc, tc)], sem_out.at[sl])

    @pl.when(i == 0)
    def _():
        st_ref[...] = jnp.zeros_like(st_ref)
        for b in range(bsz):
            in_copy(0, 0, b).start()

    @pl.when(i + 1 < n_chunks)
    def _():
        for b in range(bsz):
            in_copy(i + 1, 1 - slot, b).start()

    for b in range(bsz):
        in_copy(i, slot, b).wait()

    @pl.when(i >= 2)
    def _():
        for b in range(bsz):
            out_copy(i - 2, slot, b).wait()

    d = ubuf.shape[-1]
    u2 = ubuf[slot].reshape(tc * bsz, d).astype(BF16)
    for k in range(nblk):
        bu_ref[:, k * 2 * sw:(k + 1) * 2 * sw] = jnp.dot(
            u2[:, k * cw:(k + 1) * cw], bb_ref[k], preferred_element_type=F32)

    for k in range(nblk):
        re0, im0 = k * 2 * sw, k * 2 * sw + sw
        a_re = jnp.broadcast_to(are_ref[k], (SUBLANES, sw))
        a_im = jnp.broadcast_to(aim_ref[k], (SUBLANES, sw))

        def step(t, carry, re0=re0, im0=im0, a_re=a_re, a_im=a_im):
            s_re, s_im = carry
            r0 = pl.multiple_of(t * SUBLANES, SUBLANES)
            n_re = a_re * s_re - a_im * s_im + bu_ref[pl.ds(r0, SUBLANES), re0:re0 + sw]
            n_im = a_re * s_im + a_im * s_re + bu_ref[pl.ds(r0, SUBLANES), im0:im0 + sw]
            bu_ref[pl.ds(r0, SUBLANES), re0:re0 + sw] = n_re
            bu_ref[pl.ds(r0, SUBLANES), im0:im0 + sw] = n_im
            return n_re, n_im

        s_re, s_im = lax.fori_loop(
            0, tc, step, (st_ref[:, re0:re0 + sw], st_ref[:, im0:im0 + sw]), unroll=2)
        st_ref[:, re0:re0 + sw] = s_re
        st_ref[:, im0:im0 + sw] = s_im

    for k in range(nblk):
        s2 = bu_ref[:, k * 2 * sw:(k + 1) * 2 * sw].astype(BF16)
        yk = jnp.dot(s2, cb_ref[k], preferred_element_type=F32)
        ybuf[slot, :, :, k * cw:(k + 1) * cw] = yk.reshape(tc, bsz, cw)

    for b in range(bsz):
        out_copy(i, slot, b).start()

    @pl.when(i == n_chunks - 1)
    def _():
        if n_chunks >= 2:
            for b in range(bsz):
                out_copy(i - 1, 1 - slot, b).wait()
        for b in range(bsz):
            out_copy(i, slot, b).wait()


def _s5_scan(u2, bblk, cblk, a_re, a_im, bsz, seq, tc):
    assert bsz == SUBLANES, "the scan keeps the batch on the 8 sublanes of a vreg"
    rows, d = u2.shape
    nblk, cw, sw2 = bblk.shape
    sw = sw2 // 2
    n_chunks = seq // tc
    body = functools.partial(_s5_scan_body, tc=tc, nblk=nblk, sw=sw, seq=seq, n_chunks=n_chunks)
    return pl.pallas_call(
        body,
        grid=(n_chunks,),
        in_specs=[pl.BlockSpec(memory_space=pl.ANY),
                  pl.BlockSpec((nblk, cw, sw2), lambda i: (0, 0, 0)),
                  pl.BlockSpec((nblk, sw2, cw), lambda i: (0, 0, 0)),
                  pl.BlockSpec((nblk, 1, sw), lambda i: (0, 0, 0)),
                  pl.BlockSpec((nblk, 1, sw), lambda i: (0, 0, 0))],
        out_specs=pl.BlockSpec(memory_space=pl.ANY),
        out_shape=jax.ShapeDtypeStruct((rows, d), F32),
        scratch_shapes=[pltpu.VMEM((2, tc, bsz, d), F32), pltpu.VMEM((2, tc, bsz, d), F32),
                        pltpu.VMEM((tc * bsz, nblk * sw2), F32), pltpu.VMEM((bsz, nblk * sw2), F32),
                        pltpu.SemaphoreType.DMA((2,)), pltpu.SemaphoreType.DMA((2,))],
        compiler_params=_params("arbitrary"),
        name="s5_scan",
    )(u2, bblk, cblk, a_re, a_im)


def _s5_out_body(y_ref, u_ref, h_ref, d_ref, w_ref, gate_ref, o_ref):
    z = y_ref[...] + d_ref[...] * u_ref[...]
    act = jax.nn.gelu(z)
    vg = jnp.dot(act.astype(BF16), w_ref[...], preferred_element_type=F32)
    d = z.shape[-1]
    mix = vg[:, :d] * jax.nn.sigmoid(vg[:, d:])
    o_ref[...] = h_ref[...] + gate_ref[0] * mix


def _s5_out(y2, u2, h2, d_skip, w_out, gate, bsz, seq, tm):
    d = h2.shape[1]
    nt = seq // tm
    row = lambda b, i: (b * nt + i, 0)
    return pl.pallas_call(
        _s5_out_body,
        grid=(bsz, nt),
        in_specs=[pl.BlockSpec((tm, d), row),
                  pl.BlockSpec((tm, d), row),
                  pl.BlockSpec((tm, d), row),
                  pl.BlockSpec((1, d), lambda b, i: (0, 0)),
                  pl.BlockSpec((d, 2 * d), lambda b, i: (0, 0)),
                  pl.BlockSpec((1, 1, d), lambda b, i: (b, 0, 0))],
        out_specs=pl.BlockSpec((tm, d), row),
        out_shape=jax.ShapeDtypeStruct((bsz * seq, d), F32),
        compiler_params=_params("parallel", "parallel"),
        name="s5_out",
    )(y2, u2, h2, d_skip, w_out, gate)


def _s5_tables(lam_re, lam_im, log_dt, b_re, b_im, c_re, c_im):
    dt = jnp.exp(log_dt.astype(F32))[:, None]
    lr, li = lam_re.astype(F32), lam_im.astype(F32)
    mag = jnp.exp(lr * dt)
    a_re = mag * jnp.cos(li * dt)
    a_im = mag * jnp.sin(li * dt)
    den = lr * lr + li * li
    coef_re = ((a_re - 1.0) * lr + a_im * li) / den
    coef_im = (a_im * lr - (a_re - 1.0) * li) / den
    br_, bi_ = b_re.astype(F32), b_im.astype(F32)
    bbar_re = coef_re[..., None] * br_ - coef_im[..., None] * bi_
    bbar_im = coef_re[..., None] * bi_ + coef_im[..., None] * br_
    g, p, c = bbar_re.shape
    gb = S5_GROUPS_PER_BLOCK
    nblk = g // gb
    eye = jnp.eye(gb, dtype=F32)

    def in_blocks(m):
        return jnp.einsum('kgpc,gh->kgchp', m.reshape(nblk, gb, p, c), eye).reshape(nblk, gb * c, gb * p)

    def out_blocks(m):
        return jnp.einsum('kgcp,gh->kgphc', m.reshape(nblk, gb, c, p), eye).reshape(nblk, gb * p, gb * c)

    bblk = jnp.concatenate([in_blocks(bbar_re), in_blocks(bbar_im)], axis=-1).astype(BF16)
    cblk = jnp.concatenate([out_blocks(c_re.astype(F32)), -out_blocks(c_im.astype(F32))], axis=1).astype(BF16)
    return bblk, cblk, a_re.reshape(nblk, 1, gb * p), a_im.reshape(nblk, 1, gb * p)


def _router_body(h_ref, g_ref, sh_ref, sc_ref, wrh_ref, wrl_ref, br_ref, tri_ref,
                 x_ref, mt_ref, cnt_ref, carry_ref):
    tm, d = h_ref.shape
    ne, ng, epg = N_EXPERTS, N_EXPERT_GROUPS, EXPERTS_PER_GROUP

    @pl.when(pl.program_id(0) == 0)
    def _():
        carry_ref[...] = jnp.zeros_like(carry_ref)

    hn = _rms_mod(h_ref[...], g_ref[...], sh_ref[0], sc_ref[0])
    logits = _dot_3pass(hn, wrh_ref, wrl_ref) + br_ref[...]
    lane = lax.broadcasted_iota(I32, logits.shape, 1).astype(F32)
    big = jnp.float32(1e9)
    ninf = jnp.float32(-jnp.inf)

    gmask = (lane >= ne) & (lane < ne + ng)
    gmax = jnp.max(jnp.where(gmask, logits, ninf), axis=-1, keepdims=True)
    gsum = jnp.sum(jnp.where(gmask, jnp.exp(logits - gmax), 0.0), axis=-1, keepdims=True)
    p_g = 1.0 / gsum
    gidx = jnp.min(jnp.where(gmask & (logits == gmax), lane - ne, big), axis=-1, keepdims=True)

    emask = (lane < ne) & (jnp.floor(lane / epg) == gidx)
    v1 = jnp.max(jnp.where(emask, logits, ninf), axis=-1, keepdims=True)
    i1 = jnp.min(jnp.where(emask & (logits == v1), lane, big), axis=-1, keepdims=True)
    emask2 = emask & (lane != i1)
    v2 = jnp.max(jnp.where(emask2, logits, ninf), axis=-1, keepdims=True)
    i2 = jnp.min(jnp.where(emask2 & (logits == v2), lane, big), axis=-1, keepdims=True)
    e21 = jnp.exp(v2 - v1)
    w1 = p_g / (1.0 + e21)
    w2 = p_g * e21 / (1.0 + e21)

    first_lo = i1 < i2
    e_lo = jnp.where(first_lo, i1, i2)
    e_hi = jnp.where(first_lo, i2, i1)
    w_lo = jnp.where(first_lo, w1, w2)
    w_hi = jnp.where(first_lo, w2, w1)
    bucket = gidx * (epg * epg) + (e_lo - gidx * epg) * epg + (e_hi - gidx * epg)

    lane_b = lax.broadcasted_iota(I32, (tm, N_BUCKETS), 1).astype(F32)
    onehot = (lane_b == bucket).astype(F32)
    prefix = jnp.dot(tri_ref[...], onehot.astype(BF16), preferred_element_type=F32)
    carry = carry_ref[...]
    rank = jnp.sum(onehot * (prefix + carry), axis=-1, keepdims=True) - 1.0
    new_carry = carry + prefix[tm - 1:tm, :]
    carry_ref[...] = new_carry
    cnt_ref[...] = new_carry

    mlane = lax.broadcasted_iota(I32, (tm, META_LANES), 1)
    meta = jnp.where(mlane == 0, bucket,
           jnp.where(mlane == 1, rank,
           jnp.where(mlane == 2, w_lo,
           jnp.where(mlane == 3, w_hi, 0.0))))
    x_ref[:, :d] = hn
    x_ref[:, d:] = meta
    mt_ref[0] = meta.T[:META_ROWS, :]


def _router(h2, g, shift, scale, w_r, b_r, tri, nt_per_batch, tm):
    t, d = h2.shape
    wr_hi, wr_lo = _split_bf16(w_r)
    return pl.pallas_call(
        _router_body,
        grid=(t // tm,),
        in_specs=[pl.BlockSpec((tm, d), lambda i: (i, 0)),
                  pl.BlockSpec((1, d), lambda i: (0, 0)),
                  pl.BlockSpec((1, 1, d), lambda i: (i // nt_per_batch, 0, 0)),
                  pl.BlockSpec((1, 1, d), lambda i: (i // nt_per_batch, 0, 0)),
                  pl.BlockSpec((d, LANES), lambda i: (0, 0)),
                  pl.BlockSpec((d, LANES), lambda i: (0, 0)),
                  pl.BlockSpec((1, LANES), lambda i: (0, 0)),
                  pl.BlockSpec((tm, tm), lambda i: (0, 0))],
        out_specs=[pl.BlockSpec((tm, d + META_LANES), lambda i: (i, 0)),
                   pl.BlockSpec((1, META_ROWS, tm), lambda i: (i, 0, 0)),
                   pl.BlockSpec((1, N_BUCKETS), lambda i: (0, 0))],
        out_shape=[jax.ShapeDtypeStruct((t, d + META_LANES), F32),
                   jax.ShapeDtypeStruct((t // tm, META_ROWS, tm), F32),
                   jax.ShapeDtypeStruct((1, N_BUCKETS), F32)],
        scratch_shapes=[pltpu.VMEM((1, N_BUCKETS), F32)],
        compiler_params=_params("arbitrary"),
        name="moe_router",
    )(h2, g, shift, scale, wr_hi, wr_lo, b_r, tri)


def _plan_body(cnt_ref, mt_ref, pos_ref, maps_ref, start_ref, *, te, ntp):
    nb = N_BUCKETS
    epg = EXPERTS_PER_GROUP

    @pl.when(pl.program_id(0) == 0)
    def _():
        tiles = jnp.floor((cnt_ref[...] + (te - 1)) / te)
        tiles8 = jnp.broadcast_to(tiles, (SUBLANES, nb)).astype(BF16)
        r = lax.broadcasted_iota(I32, (nb, nb), 0)
        c = lax.broadcasted_iota(I32, (nb, nb), 1)
        nt_dims = (((1,), (1,)), ((), ()))
        start = lax.dot_general((c < r).astype(BF16), tiles8, nt_dims, preferred_element_type=F32)[:, :1]
        end = lax.dot_general((c <= r).astype(BF16), tiles8, nt_dims, preferred_element_type=F32)[:, :1]
        start_ref[...] = start * te
        n_tiles = end[nb - 1:nb, :]
        j = lax.broadcasted_iota(I32, (1, ntp), 1).astype(F32)
        tb = jnp.minimum(j, jnp.maximum(n_tiles - 1.0, 0.0))
        bucket = jnp.sum((end <= tb).astype(F32), axis=0, keepdims=True)
        grp = jnp.floor(bucket / (epg * epg))
        within = bucket - grp * (epg * epg)
        lo = jnp.floor(within / epg)
        e_lo = grp * epg + lo
        e_hi = grp * epg + (within - lo * epg)
        valid = (j < n_tiles).astype(F32)
        row = lax.broadcasted_iota(I32, (SUBLANES, ntp), 0)
        maps = jnp.where(row == 0, tb, jnp.where(row == 1, e_lo, jnp.where(row == 2, e_hi,
               jnp.where(row == 3, valid, 0.0))))
        maps_ref[...] = maps.astype(I32)

    tm = mt_ref.shape[-1]
    bucket_row = mt_ref[0, 0:1, :]
    rank_row = mt_ref[0, 1:2, :]
    rb = lax.broadcasted_iota(I32, (nb, tm), 0).astype(F32)
    pos = jnp.sum(jnp.where(rb == bucket_row, start_ref[...], 0.0), axis=0, keepdims=True) + rank_row
    pos_ref[0] = pos.astype(I32)


def _plan(counts, meta_t, te, nt_max):
    n_tt, _, tm = meta_t.shape
    ntp = ((nt_max + LANES - 1) // LANES) * LANES
    body = functools.partial(_plan_body, te=te, ntp=ntp)
    return pl.pallas_call(
        body,
        grid=(n_tt,),
        in_specs=[pl.BlockSpec((1, N_BUCKETS), lambda i: (0, 0)),
                  pl.BlockSpec((1, META_ROWS, tm), lambda i: (i, 0, 0))],
        out_specs=[pl.BlockSpec((1, 1, tm), lambda i: (i, 0, 0)),
                   pl.BlockSpec((SUBLANES, ntp), lambda i: (0, 0))],
        out_shape=[jax.ShapeDtypeStruct((n_tt, 1, tm), I32),
                   jax.ShapeDtypeStruct((SUBLANES, ntp), I32)],
        scratch_shapes=[pltpu.VMEM((N_BUCKETS, 1), F32)],
        compiler_params=_params("arbitrary"),
        name="moe_plan",
    )(counts, meta_t)


def _row_copy(src_ref, src_row, dst_ref, dst_row, sem):
    return pltpu.make_async_copy(src_ref.at[pl.ds(src_row, 1)], dst_ref.at[pl.ds(dst_row, 1)], sem)


def _dispatch_body(pos_ref, x_ref, xs_in_ref, xs_ref, sem):
    del xs_in_ref
    tm = x_ref.shape[0]

    def issue(grp, c):
        r0 = grp * ROW_DMA_UNROLL
        for u in range(ROW_DMA_UNROLL):
            _row_copy(x_ref, r0 + u, xs_ref, pos_ref[0, 0, r0 + u], sem).start(priority=u % 2)
        return c

    lax.fori_loop(0, tm // ROW_DMA_UNROLL, issue, 0)
    pltpu.make_async_copy(x_ref, xs_ref.at[pl.ds(0, tm)], sem).wait()


def _dispatch(xrow, pos3, t_pad, tm):
    t, w = xrow.shape
    zeros = jnp.zeros((t_pad, w), F32)
    return pl.pallas_call(
        _dispatch_body,
        grid=(t // tm,),
        in_specs=[pl.BlockSpec((1, 1, tm), lambda i: (i, 0, 0), memory_space=pltpu.SMEM),
                  pl.BlockSpec((tm, w), lambda i: (i, 0)),
                  pl.BlockSpec(memory_space=pl.ANY)],
        out_specs=pl.BlockSpec(memory_space=pl.ANY),
        out_shape=jax.ShapeDtypeStruct((t_pad, w), F32),
        scratch_shapes=[pltpu.SemaphoreType.DMA(())],
        input_output_aliases={2: 0},
        compiler_params=_params("arbitrary"),
        name="moe_dispatch",
    )(pos3, xrow, zeros)


def _expert_body(tb_ref, elo_ref, ehi_ref, valid_ref, x_ref,
                 w1a_ref, w3a_ref, w2a_ref, w1b_ref, w3b_ref, w2b_ref, o_ref):
    del tb_ref, elo_ref, ehi_ref
    d = o_ref.shape[1]

    @pl.when(valid_ref[pl.program_id(0)] == 0)
    def _():
        o_ref[...] = jnp.zeros_like(o_ref)

    @pl.when(valid_ref[pl.program_id(0)] == 1)
    def _():
        x = x_ref[:, :d].astype(BF16)

        def ffn(w1_ref, w3_ref, w2_ref, wt):
            a = jnp.dot(x, w1_ref[...].astype(BF16), preferred_element_type=F32)
            b = jnp.dot(x, w3_ref[...].astype(BF16), preferred_element_type=F32)
            mid = (a * jax.nn.sigmoid(a)) * b
            y = jnp.dot(mid.astype(BF16), w2_ref[...].astype(BF16), preferred_element_type=F32)
            return wt * y

        o_ref[...] = (ffn(w1a_ref, w3a_ref, w2a_ref, x_ref[:, d + 2:d + 3])
                      + ffn(w1b_ref, w3b_ref, w2b_ref, x_ref[:, d + 3:d + 4]))


def _experts(xs, w1, w3, w2, layer, tile_block, e_lo, e_hi, valid, te):
    t_pad, w = xs.shape
    _, n_e, d, f = w1.shape
    nt = t_pad // te
    x_spec = pl.BlockSpec((te, w), lambda j, tb, lo, hi, v: (tb[j], 0))
    up_lo = pl.BlockSpec((None, None, d, f), lambda j, tb, lo, hi, v: (layer, lo[j], 0, 0))
    dn_lo = pl.BlockSpec((None, None, f, d), lambda j, tb, lo, hi, v: (layer, lo[j], 0, 0))
    up_hi = pl.BlockSpec((None, None, d, f), lambda j, tb, lo, hi, v: (layer, hi[j], 0, 0))
    dn_hi = pl.BlockSpec((None, None, f, d), lambda j, tb, lo, hi, v: (layer, hi[j], 0, 0))
    return pl.pallas_call(
        _expert_body,
        grid_spec=pltpu.PrefetchScalarGridSpec(
            num_scalar_prefetch=4,
            grid=(nt,),
            in_specs=[x_spec, up_lo, up_lo, dn_lo, up_hi, up_hi, dn_hi],
            out_specs=pl.BlockSpec((te, d), lambda j, tb, lo, hi, v: (j, 0))),
        out_shape=jax.ShapeDtypeStruct((t_pad, d), F32),
        compiler_params=_params("arbitrary"),
        name="moe_experts",
    )(tile_block, e_lo, e_hi, valid, xs, w1, w3, w2, w1, w3, w2)


def _combine_body(pos_ref, ys_ref, h_ref, gate_ref, o_ref, ybuf, sem):
    tm = h_ref.shape[0]

    def issue(grp, c):
        r0 = grp * ROW_DMA_UNROLL
        for u in range(ROW_DMA_UNROLL):
            _row_copy(ys_ref, pos_ref[0, 0, r0 + u], ybuf, r0 + u, sem).start(priority=u % 2)
        return c

    lax.fori_loop(0, tm // ROW_DMA_UNROLL, issue, 0)
    pltpu.make_async_copy(ys_ref.at[pl.ds(0, tm)], ybuf, sem).wait()
    o_ref[...] = h_ref[...] + gate_ref[0] * ybuf[...]


def _combine(ys, pos3, h2, gate, nt_per_batch, tm):
    t, d = h2.shape
    return pl.pallas_call(
        _combine_body,
        grid=(t // tm,),
        in_specs=[pl.BlockSpec((1, 1, tm), lambda i: (i, 0, 0), memory_space=pltpu.SMEM),
                  pl.BlockSpec(memory_space=pl.ANY),
                  pl.BlockSpec((tm, d), lambda i: (i, 0)),
                  pl.BlockSpec((1, 1, d), lambda i: (i // nt_per_batch, 0, 0))],
        out_specs=pl.BlockSpec((tm, d), lambda i: (i, 0)),
        out_shape=jax.ShapeDtypeStruct((t, d), F32),
        scratch_shapes=[pltpu.VMEM((tm, d), F32), pltpu.SemaphoreType.DMA(())],
        compiler_params=_params("arbitrary"),
        name="moe_combine",
    )(pos3, ys, h2, gate)


def _moe(h2, g, shift, scale, gate, wg, bg, we, be, w1, w3, w2, layer, bsz, seq, tm, te):
    t, d = h2.shape
    nt_per_batch = seq // tm
    ne, ng = N_EXPERTS, N_EXPERT_GROUPS
    w_r = jnp.zeros((d, LANES), F32).at[:, :ne].set(we).at[:, ne:ne + ng].set(wg)
    b_r = jnp.zeros((1, LANES), F32).at[0, :ne].set(be).at[0, ne:ne + ng].set(bg)
    tri = jnp.asarray(np.tril(np.ones((tm, tm), np.float32)), BF16)
    xrow, meta_t, counts = _router(h2, g, shift, scale, w_r, b_r, tri, nt_per_batch, tm)
    nt_max = t // te + N_PAIR_BUCKETS
    pos3, maps = _plan(counts, meta_t, te, nt_max)
    xs = _dispatch(xrow, pos3, nt_max * te, tm)
    ys = _experts(xs, w1, w3, w2, layer, maps[0, :nt_max], maps[1, :nt_max], maps[2, :nt_max],
                  maps[3, :nt_max], te)
    return _combine(ys, pos3, h2, gate, nt_per_batch, tm)


def _log_sigmoid(x):
    return jnp.minimum(x, 0.0) - jnp.log1p(jnp.exp(-jnp.abs(x)))


def _aug_tables(n_heads):
    assert n_heads * AUG_LANES_PER_HEAD <= LANES
    width = LANES
    pk = np.zeros((F_SPLIT * LANES, width), np.float32)
    pq = np.zeros((F_SPLIT * LANES, width), np.float32)
    ck = np.zeros((1, width), np.float32)
    cq = np.zeros((1, width), np.float32)
    for h in range(n_heads):
        base = h * AUG_LANES_PER_HEAD
        for j in range(F_SPLIT):
            pk[j * LANES + h, base + j] = -1.0
            pq[j * LANES + h, base + F_SPLIT + j] = 1.0
            ck[0, base + F_SPLIT + j] = 1.0
            cq[0, base + j] = 1.0
    return jnp.asarray(pk, BF16), jnp.asarray(pq, BF16), jnp.asarray(ck), jnp.asarray(cq)


def _kv_body(h_ref, g_ref, sh_ref, sc_ref, wk_ref, wvt_ref, wfh_ref, wfl_ref, fb_ref, kng_ref,
             pk_ref, pq_ref, ck_ref, cq_ref, k_ref, vt_ref, ka_ref, qa_ref, carry_ref):
    tm = h_ref.shape[0]

    @pl.when(pl.program_id(1) == 0)
    def _():
        carry_ref[...] = jnp.zeros_like(carry_ref)

    hn = _rms_mod(h_ref[...], g_ref[...], sh_ref[0], sc_ref[0])
    hb = hn.astype(BF16)
    k = jnp.dot(hb, wk_ref[...], preferred_element_type=F32)
    k_ref[...] = _head_rms(k, kng_ref[...]).astype(BF16)
    nt_dims = (((1,), (1,)), ((), ()))
    vt_ref[...] = lax.dot_general(wvt_ref[...], hb, nt_dims, preferred_element_type=F32).astype(BF16)
    fz = _dot_3pass(hn, wfh_ref, wfl_ref) + fb_ref[...]
    c = _log_sigmoid(fz)
    row = lax.broadcasted_iota(I32, c.shape, 0)
    shift = 1
    while shift < tm:
        c = c + jnp.where(row >= shift, pltpu.roll(c, shift, 0), 0.0)
        shift *= 2
    f = c + carry_ref[...]
    carry_ref[...] = f[tm - 1:tm, :]

    f2 = f * LOG2E
    hi = f2.astype(BF16)
    r1 = f2 - hi.astype(F32)
    mid = r1.astype(BF16)
    lo = (r1 - mid.astype(F32)).astype(BF16)
    pieces = jnp.concatenate([hi, mid, lo], axis=-1)
    ka_ref[...] = (jnp.dot(pieces, pk_ref[...], preferred_element_type=F32) + ck_ref[...]).astype(BF16)
    qa_ref[...] = (jnp.dot(pieces, pq_ref[...], preferred_element_type=F32) + cq_ref[...]).astype(BF16)


def _shared_kv(h2, g, shift, scale, wk, wvt, wf, fb, kng, bsz, seq, tm):
    t, d = h2.shape
    nt = seq // tm
    aw = LANES
    pk, pq, ck, cq = _aug_tables(d // HEAD_DIM)
    wf_hi, wf_lo = _split_bf16(wf)
    row = lambda b, i: (b * nt + i, 0)
    const = lambda b, i: (0, 0)
    return pl.pallas_call(
        _kv_body,
        grid=(bsz, nt),
        in_specs=[pl.BlockSpec((tm, d), row),
                  pl.BlockSpec((1, d), const),
                  pl.BlockSpec((1, 1, d), lambda b, i: (b, 0, 0)),
                  pl.BlockSpec((1, 1, d), lambda b, i: (b, 0, 0)),
                  pl.BlockSpec((d, d), const),
                  pl.BlockSpec((d, d), const),
                  pl.BlockSpec((d, LANES), const),
                  pl.BlockSpec((d, LANES), const),
                  pl.BlockSpec((1, LANES), const),
                  pl.BlockSpec((1, d), const),
                  pl.BlockSpec((F_SPLIT * LANES, aw), const),
                  pl.BlockSpec((F_SPLIT * LANES, aw), const),
                  pl.BlockSpec((1, aw), const),
                  pl.BlockSpec((1, aw), const)],
        out_specs=[pl.BlockSpec((tm, d), row),
                   pl.BlockSpec((d, tm), lambda b, i: (b, i)),
                   pl.BlockSpec((tm, aw), row),
                   pl.BlockSpec((tm, aw), row)],
        out_shape=[jax.ShapeDtypeStruct((t, d), BF16), jax.ShapeDtypeStruct((bsz * d, seq), BF16),
                   jax.ShapeDtypeStruct((t, aw), BF16), jax.ShapeDtypeStruct((t, aw), BF16)],
        scratch_shapes=[pltpu.VMEM((1, LANES), F32)],
        compiler_params=_params("parallel", "arbitrary"),
        name="shared_kv",
    )(h2, g, shift, scale, wk, wvt, wf_hi, wf_lo, fb, kng, pk, pq, ck, cq)


def _qg_body(h_ref, g_ref, sh_ref, sc_ref, wq_ref, wg_ref, qng_ref, q_ref, og_ref):
    hn = _rms_mod(h_ref[...], g_ref[...], sh_ref[0], sc_ref[0])
    hb = hn.astype(BF16)
    q = jnp.dot(hb, wq_ref[...], preferred_element_type=F32)
    q_ref[...] = (_head_rms(q, qng_ref[...]) * (HEAD_DIM ** -0.5 * LOG2E)).astype(BF16)
    og_ref[...] = jnp.dot(hb, wg_ref[...], preferred_element_type=F32).astype(BF16)


def _fox_qg(h2, g, shift, scale, wq, wg, qng, bsz, seq, tm):
    t, d = h2.shape
    nt = seq // tm
    row = lambda b, i: (b * nt + i, 0)
    const = lambda b, i: (0, 0)
    return pl.pallas_call(
        _qg_body,
        grid=(bsz, nt),
        in_specs=[pl.BlockSpec((tm, d), row),
                  pl.BlockSpec((1, d), const),
                  pl.BlockSpec((1, 1, d), lambda b, i: (b, 0, 0)),
                  pl.BlockSpec((1, 1, d), lambda b, i: (b, 0, 0)),
                  pl.BlockSpec((d, d), const),
                  pl.BlockSpec((d, d), const),
                  pl.BlockSpec((1, d), const)],
        out_specs=[pl.BlockSpec((tm, d), row), pl.BlockSpec((tm, d), row)],
        out_shape=[jax.ShapeDtypeStruct((t, d), BF16), jax.ShapeDtypeStruct((t, d), BF16)],
        compiler_params=_params("parallel", "parallel"),
        name="fox_qg",
    )(h2, g, shift, scale, wq, wg, qng)


def _attn_body(q_ref, qa_ref, k_ref, ka_ref, vt_ref, o_ref, m_ref, l_ref, acc_ref, *, tq, hp):
    grp = pl.program_id(1)
    qi = pl.program_id(2)
    n_slabs = hp // 2
    lane_q = lax.broadcasted_iota(I32, (tq, LANES), 1)
    lo_q = lane_q < HEAD_DIM
    lo_v = lax.broadcasted_iota(I32, (LANES, tq), 0) < HEAD_DIM
    nt_dims = (((1,), (1,)), ((), ()))
    causal = lax.broadcasted_iota(I32, (tq, tq), 0) <= lax.broadcasted_iota(I32, (tq, tq), 1)

    qa = qa_ref[...]
    qcats = []
    for h in range(hp):
        sl, hh = h // 2, h % 2
        q2 = q_ref[:, sl * LANES:(sl + 1) * LANES]
        own = lo_q if hh == 0 else jnp.logical_not(lo_q)
        a0 = (grp * hp + h) * AUG_LANES_PER_HEAD
        own_a = (lane_q >= a0) & (lane_q < a0 + AUG_LANES_PER_HEAD)
        qcats.append(jnp.concatenate([jnp.where(own, q2, jnp.zeros_like(q2)),
                                      jnp.where(own_a, qa, jnp.zeros_like(qa))], axis=-1))

    m_ref[...] = jnp.full(m_ref.shape, NEG, F32)
    l_ref[...] = jnp.zeros(l_ref.shape, F32)
    acc_ref[...] = jnp.zeros(acc_ref.shape, F32)

    def block(kb, masked):
        s0 = pl.multiple_of(kb * tq, tq)
        for sl in range(n_slabs):
            kcat = jnp.concatenate([k_ref[pl.ds(s0, tq), sl * LANES:(sl + 1) * LANES],
                                    ka_ref[pl.ds(s0, tq), :]], axis=-1)
            vt = vt_ref[sl * LANES:(sl + 1) * LANES, pl.ds(s0, tq)]
            vts = (jnp.where(lo_v, vt, jnp.zeros_like(vt)), jnp.where(lo_v, jnp.zeros_like(vt), vt))
            pv, alphas = None, []
            for hh in range(2):
                h = sl * 2 + hh
                st = lax.dot_general(kcat, qcats[h], nt_dims, preferred_element_type=F32)
                if masked:
                    st = jnp.where(causal, st, NEG)
                m_old = m_ref[h]
                m_new = jnp.maximum(m_old, jnp.max(st, axis=0, keepdims=True))
                alpha = jnp.exp2(m_old - m_new)
                p = jnp.exp2(st - m_new)
                l_ref[h] = alpha * l_ref[h] + jnp.sum(p, axis=0, keepdims=True)
                m_ref[h] = m_new
                part = jnp.dot(vts[hh], p.astype(BF16), preferred_element_type=F32)
                pv = part if pv is None else pv + part
                alphas.append(alpha)
            acc_ref[sl] = jnp.where(lo_v, alphas[0], alphas[1]) * acc_ref[sl] + pv

    def unmasked(kb, c):
        block(kb, False)
        return c

    lax.fori_loop(0, qi, unmasked, 0)
    block(qi, True)
    for sl in range(n_slabs):
        l2 = jnp.where(lo_v, l_ref[2 * sl], l_ref[2 * sl + 1])
        o_ref[:, sl * LANES:(sl + 1) * LANES] = (acc_ref[sl] / l2).T.astype(BF16)


def _fox_attention(q, qaug, k, kaug, vt, bsz, seq, tq, hp):
    t, d = q.shape
    w = hp * HEAD_DIM
    n_grp = d // w
    nq = seq // tq
    body = functools.partial(_attn_body, tq=tq, hp=hp)
    qrow = lambda b, j, i: (b * nq + i, j)
    krow = lambda b, j, i: (b, j)
    return pl.pallas_call(
        body,
        grid=(bsz, n_grp, nq),
        in_specs=[pl.BlockSpec((tq, w), qrow),
                  pl.BlockSpec((tq, LANES), lambda b, j, i: (b * nq + i, 0)),
                  pl.BlockSpec((seq, w), krow),
                  pl.BlockSpec((seq, LANES), lambda b, j, i: (b, 0)),
                  pl.BlockSpec((w, seq), lambda b, j, i: (b * n_grp + j, 0))],
        out_specs=pl.BlockSpec((tq, w), qrow),
        out_shape=jax.ShapeDtypeStruct((t, d), BF16),
        scratch_shapes=[pltpu.VMEM((hp, 1, tq), F32), pltpu.VMEM((hp, 1, tq), F32),
                        pltpu.VMEM((hp // 2, LANES, tq), F32)],
        compiler_params=_params("parallel", "parallel", "arbitrary"),
        name="fox_attention",
    )(q, qaug, k, kaug, vt)


def _fox_out_body(o_ref, og_ref, h_ref, w_ref, gate_ref, out_ref):
    z = o_ref[...].astype(F32) * jax.nn.sigmoid(og_ref[...].astype(F32))
    out_ref[...] = h_ref[...] + gate_ref[0] * jnp.dot(z.astype(BF16), w_ref[...], preferred_element_type=F32)


def _fox_out(o, og, h2, w_o, gate, bsz, seq, tm):
    t, d = h2.shape
    nt = seq // tm
    row = lambda b, i: (b * nt + i, 0)
    return pl.pallas_call(
        _fox_out_body,
        grid=(bsz, nt),
        in_specs=[pl.BlockSpec((tm, d), row), pl.BlockSpec((tm, d), row), pl.BlockSpec((tm, d), row),
                  pl.BlockSpec((d, d), lambda b, i: (0, 0)),
                  pl.BlockSpec((1, 1, d), lambda b, i: (b, 0, 0))],
        out_specs=pl.BlockSpec((tm, d), row),
        out_shape=jax.ShapeDtypeStruct((t, d), F32),
        compiler_params=_params("parallel", "parallel"),
        name="fox_out",
    )(o, og, h2, w_o, gate)


def _tiles(seq):
    tm = min(512, seq)
    te = min(256, seq)
    tq = min(512, seq)
    tc = min(32, seq)
    return tm, te, tq, tc


def kernel(x, c, ln_g, ada_w, ada_b, s5_w_in, s5_lambda_re, s5_lambda_im, s5_log_dt, s5_b_re, s5_b_im,
           s5_c_re, s5_c_im, s5_d, s5_w_out, kv_g, kv_ada_w, kv_ada_b, kv_w, kv_fb, k_norm_g,
           fox_w_qg, fox_q_norm_g, fox_w_o, moe_wg, moe_bg, moe_we, moe_be, moe_w1, moe_w3, moe_w2):
    bsz, seq, d = x.shape
    depth = ln_g.shape[0]
    n_a = s5_w_in.shape[0]
    n_heads = d // HEAD_DIM
    tm, te, tq, tc = _tiles(seq)

    mods = _adaln(c, ada_w.reshape(depth * 2, d, 3 * d), ada_b.reshape(depth * 2, 1, 3 * d))
    mods = mods.reshape(depth, 2, bsz, 3, 1, d)
    kv_mods = _adaln(c, kv_ada_w[None], kv_ada_b[None, None]).reshape(bsz, 2, 1, d)

    h = x.reshape(bsz * seq, d)
    k = kaug = qaug = vt = None
    for l in range(depth):
        shift, scale, gate = mods[l, 0, :, 0], mods[l, 0, :, 1], mods[l, 0, :, 2]
        g = ln_g[l, 0][None]
        if l < n_a:
            u2 = _s5_in(h, g, shift, scale, s5_w_in[l].astype(BF16), bsz, seq, tm)
            bblk, cblk, a_re, a_im = _s5_tables(s5_lambda_re[l], s5_lambda_im[l], s5_log_dt[l],
                                                s5_b_re[l], s5_b_im[l], s5_c_re[l], s5_c_im[l])
            y2 = _s5_scan(u2, bblk, cblk, a_re, a_im, bsz, seq, tc)
            h = _s5_out(y2, u2, h, s5_d[l][None], s5_w_out[l].astype(BF16), gate, bsz, seq, tm)
        else:
            j = l - n_a
            qng = jnp.tile(fox_q_norm_g[j], n_heads)[None]
            q, og = _fox_qg(h, g, shift, scale, fox_w_qg[j][:, :d].astype(BF16),
                            fox_w_qg[j][:, d:].astype(BF16), qng, bsz, seq, tm)
            o = _fox_attention(q, qaug, k, kaug, vt, bsz, seq, tq, ATTN_HEADS_PER_STEP)
            h = _fox_out(o, og, h, fox_w_o[j].astype(BF16), gate, bsz, seq, tm)

        shift, scale, gate = mods[l, 1, :, 0], mods[l, 1, :, 1], mods[l, 1, :, 2]
        h = _moe(h, ln_g[l, 1][None], shift, scale, gate, moe_wg[l], moe_bg[l], moe_we[l], moe_be[l],
                 moe_w1, moe_w3, moe_w2, l, bsz, seq, tm, te)

        if l == n_a - 1:
            wf = jnp.zeros((d, LANES), F32).at[:, :n_heads].set(kv_w[:, 2 * d:])
            fb = jnp.zeros((1, LANES), F32).at[0, :n_heads].set(kv_fb)
            kng = jnp.tile(k_norm_g, n_heads)[None]
            k, vt, kaug, qaug = _shared_kv(h, kv_g[None], kv_mods[:, 0], kv_mods[:, 1],
                                           kv_w[:, :d].astype(BF16), kv_w[:, d:2 * d].T.astype(BF16),
                                           wf, fb, kng, bsz, seq, tm)
    return h.reshape(bsz, seq, d)
```

```python
import functools
import math

import numpy as np
import jax
import jax.numpy as jnp
from jax import lax
from jax.experimental import pallas as pl
from jax.experimental.pallas import tpu as pltpu

F32 = jnp.float32
BF16 = jnp.bfloat16
I32 = jnp.int32

EPS = 1e-6
NEG = -1e30
LOG2E = math.log2(math.e)
LANES = 128
SUBLANES = 8
VMEM_LIMIT_BYTES = 56 * 1024 * 1024

S5_GROUPS_PER_BLOCK = 16
N_EXPERT_GROUPS = 4
EXPERTS_PER_GROUP = 8
N_EXPERTS = N_EXPERT_GROUPS * EXPERTS_PER_GROUP
N_BUCKETS = N_EXPERT_GROUPS * EXPERTS_PER_GROUP * EXPERTS_PER_GROUP
N_PAIR_BUCKETS = N_EXPERT_GROUPS * (EXPERTS_PER_GROUP * (EXPERTS_PER_GROUP - 1) // 2)
META_LANES = LANES
META_ROWS = SUBLANES
ROW_DMA_UNROLL = 8
HEAD_DIM = 64
ATTN_HEADS_PER_STEP = 4
F_SPLIT = 3
AUG_LANES_PER_HEAD = 2 * F_SPLIT


def _params(*sem):
    return pltpu.CompilerParams(dimension_semantics=sem, vmem_limit_bytes=VMEM_LIMIT_BYTES)


def _rms_mod(x, g, shift, scale):
    ms = jnp.mean(x * x, axis=-1, keepdims=True)
    y = x * lax.rsqrt(ms + EPS) * g
    return y * (1.0 + scale) + shift


def _split_bf16(w):
    hi = w.astype(BF16)
    return hi, (w - hi.astype(F32)).astype(BF16)


def _dot_3pass(x, w_hi_ref, w_lo_ref):
    x_hi, x_lo = _split_bf16(x)
    w_hi = w_hi_ref[...]
    return (jnp.dot(x_hi, w_hi, preferred_element_type=F32)
            + jnp.dot(x_lo, w_hi, preferred_element_type=F32)
            + jnp.dot(x_hi, w_lo_ref[...], preferred_element_type=F32))


def _head_rms(x, g):
    tm, d = x.shape
    lane = lax.broadcasted_iota(I32, (tm, LANES), 1)
    lo = lane < HEAD_DIM
    outs = []
    for j in range(d // LANES):
        s = x[:, j * LANES:(j + 1) * LANES]
        sq = s * s
        s_lo = jnp.sum(jnp.where(lo, sq, 0.0), axis=-1, keepdims=True)
        s_hi = jnp.sum(jnp.where(lo, 0.0, sq), axis=-1, keepdims=True)
        r = jnp.where(lo, lax.rsqrt(s_lo / HEAD_DIM + EPS), lax.rsqrt(s_hi / HEAD_DIM + EPS))
        outs.append(s * r)
    return jnp.concatenate(outs, axis=-1) * g


def _adaln_body(c_ref, w_ref, b_ref, o_ref):
    c = c_ref[...]
    s = c * jax.nn.sigmoid(c)
    o_ref[0] = jnp.dot(s, w_ref[0], preferred_element_type=F32) + b_ref[0]


def _adaln(c, w, b):
    n_sets, d, n = w.shape
    bsz = c.shape[0]
    tn = 512 if n % 512 == 0 else n
    return pl.pallas_call(
        _adaln_body,
        grid=(n_sets, n // tn),
        in_specs=[pl.BlockSpec((bsz, d), lambda s, j: (0, 0)),
                  pl.BlockSpec((1, d, tn), lambda s, j: (s, 0, j)),
                  pl.BlockSpec((1, 1, tn), lambda s, j: (s, 0, j))],
        out_specs=pl.BlockSpec((1, bsz, tn), lambda s, j: (s, 0, j)),
        out_shape=jax.ShapeDtypeStruct((n_sets, bsz, n), F32),
        compiler_params=_params("parallel", "parallel"),
        name="adaln",
    )(c, w, b)


def _s5_in_body(x_ref, g_ref, sh_ref, sc_ref, w_ref, u_ref):
    hn = _rms_mod(x_ref[...], g_ref[...], sh_ref[0], sc_ref[0])
    u_ref[...] = jnp.dot(hn.astype(BF16), w_ref[...], preferred_element_type=F32)


def _s5_in(x2, g, shift, scale, w_in, bsz, seq, tm):
    d = x2.shape[1]
    nt = seq // tm
    row = lambda b, i: (b * nt + i, 0)
    return pl.pallas_call(
        _s5_in_body,
        grid=(bsz, nt),
        in_specs=[pl.BlockSpec((tm, d), row),
                  pl.BlockSpec((1, d), lambda b, i: (0, 0)),
                  pl.BlockSpec((1, 1, d), lambda b, i: (b, 0, 0)),
                  pl.BlockSpec((1, 1, d), lambda b, i: (b, 0, 0)),
                  pl.BlockSpec((d, d), lambda b, i: (0, 0))],
        out_specs=pl.BlockSpec((tm, d), row),
        out_shape=jax.ShapeDtypeStruct((bsz * seq, d), F32),
        compiler_params=_params("parallel", "parallel"),
        name="s5_in",
    )(x2, g, shift, scale, w_in)


def _s5_scan_body(u_hbm, bb_ref, cb_ref, are_ref, aim_ref, y_hbm,
                  ubuf, ybuf, bu_ref, st_ref, sem_in, sem_out, *, tc, nblk, sw, seq, n_chunks):
    bsz = SUBLANES
    cw = S5_GROUPS_PER_BLOCK * 16
    i = pl.program_id(0)
    slot = lax.rem(i, 2)

    def in_copy(chunk, sl, b):
        return pltpu.make_async_copy(u_hbm.at[pl.ds(b * seq + chunk * tc, tc)],
                                     ubuf.at[sl, :, b, :], sem_in.at[sl])

    def out_copy(chunk, sl, b):
        return pltpu.make_async_copy(ybuf.at[sl, :, b, :],
                                     y_hbm.at[pl.ds(b * seq + chunk * tc, tc)], sem_out.at[sl])

    @pl.when(i == 0)
    def _():
        st_ref[...] = jnp.zeros_like(st_ref)
        for b in range(bsz):
            in_copy(0, 0, b).start()

    @pl.when(i + 1 < n_chunks)
    def _():
        for b in range(bsz):
            in_copy(i + 1, 1 - slot, b).start()

    for b in range(bsz):
        in_copy(i, slot, b).wait()

    @pl.when(i >= 2)
    def _():
        for b in range(bsz):
            out_copy(i - 2, slot, b).wait()

    d = ubuf.shape[-1]
    u2 = ubuf[slot].reshape(tc * bsz, d).astype(BF16)
    for k in range(nblk):
        bu_ref[:, k * 2 * sw:(k + 1) * 2 * sw] = jnp.dot(
            u2[:, k * cw:(k + 1) * cw], bb_ref[k], preferred_element_type=F32)

    for k in range(nblk):
        re0, im0 = k * 2 * sw, k * 2 * sw + sw
        a_re = jnp.broadcast_to(are_ref[k], (SUBLANES, sw))
        a_im = jnp.broadcast_to(aim_ref[k], (SUBLANES, sw))

        def step(t, carry, re0=re0, im0=im0, a_re=a_re, a_im=a_im):
            s_re, s_im = carry
            r0 = pl.multiple_of(t * SUBLANES, SUBLANES)
            n_re = a_re * s_re - a_im * s_im + bu_ref[pl.ds(r0, SUBLANES), re0:re0 + sw]
            n_im = a_re * s_im + a_im * s_re + bu_ref[pl.ds(r0, SUBLANES), im0:im0 + sw]
            bu_ref[pl.ds(r0, SUBLANES), re0:re0 + sw] = n_re
            bu_ref[pl.ds(r0, SUBLANES), im0:im0 + sw] = n_im
            return n_re, n_im

        s_re, s_im = lax.fori_loop(
            0, tc, step, (st_ref[:, re0:re0 + sw], st_ref[:, im0:im0 + sw]), unroll=2)
        st_ref[:, re0:re0 + sw] = s_re
        st_ref[:, im0:im0 + sw] = s_im

    for k in range(nblk):
        s2 = bu_ref[:, k * 2 * sw:(k + 1) * 2 * sw].astype(BF16)
        yk = jnp.dot(s2, cb_ref[k], preferred_element_type=F32)
        ybuf[slot, :, :, k * cw:(k + 1) * cw] = yk.reshape(tc, bsz, cw)

    for b in range(bsz):
        out_copy(i, slot, b).start()

    @pl.when(i == n_chunks - 1)
    def _():
        if n_chunks >= 2:
            for b in range(bsz):
                out_copy(i - 1, 1 - slot, b).wait()
        for b in range(bsz):
            out_copy(i, slot, b).wait()


def _s5_scan(u2, bblk, cblk, a_re, a_im, bsz, seq, tc):
    assert bsz == SUBLANES, "the scan keeps the batch on the 8 sublanes of a vreg"
    rows, d = u2.shape
    nblk, cw, sw2 = bblk.shape
    sw = sw2 // 2
    n_chunks = seq // tc
    body = functools.partial(_s5_scan_body, tc=tc, nblk=nblk, sw=sw, seq=seq, n_chunks=n_chunks)
    return pl.pallas_call(
        body,
        grid=(n_chunks,),
        in_specs=[pl.BlockSpec(memory_space=pl.ANY),
                  pl.BlockSpec((nblk, cw, sw2), lambda i: (0, 0, 0)),
                  pl.BlockSpec((nblk, sw2, cw), lambda i: (0, 0, 0)),
                  pl.BlockSpec((nblk, 1, sw), lambda i: (0, 0, 0)),
                  pl.BlockSpec((nblk, 1, sw), lambda i: (0, 0, 0))],
        out_specs=pl.BlockSpec(memory_space=pl.ANY),
        out_shape=jax.ShapeDtypeStruct((rows, d), F32),
        scratch_shapes=[pltpu.VMEM((2, tc, bsz, d), F32), pltpu.VMEM((2, tc, bsz, d), F32),
                        pltpu.VMEM((tc * bsz, nblk * sw2), F32), pltpu.VMEM((bsz, nblk * sw2), F32),
                        pltpu.SemaphoreType.DMA((2,)), pltpu.SemaphoreType.DMA((2,))],
        compiler_params=_params("arbitrary"),
        name="s5_scan",
    )(u2, bblk, cblk, a_re, a_im)


def _s5_out_body(y_ref, u_ref, h_ref, d_ref, w_ref, gate_ref, o_ref):
    z = y_ref[...] + d_ref[...] * u_ref[...]
    act = jax.nn.gelu(z)
    vg = jnp.dot(act.astype(BF16), w_ref[...], preferred_element_type=F32)
    d = z.shape[-1]
    mix = vg[:, :d] * jax.nn.sigmoid(vg[:, d:])
    o_ref[...] = h_ref[...] + gate_ref[0] * mix


def _s5_out(y2, u2, h2, d_skip, w_out, gate, bsz, seq, tm):
    d = h2.shape[1]
    nt = seq // tm
    row = lambda b, i: (b * nt + i, 0)
    return pl.pallas_call(
        _s5_out_body,
        grid=(bsz, nt),
        in_specs=[pl.BlockSpec((tm, d), row),
                  pl.BlockSpec((tm, d), row),
                  pl.BlockSpec((tm, d), row),
                  pl.BlockSpec((1, d), lambda b, i: (0, 0)),
                  pl.BlockSpec((d, 2 * d), lambda b, i: (0, 0)),
                  pl.BlockSpec((1, 1, d), lambda b, i: (b, 0, 0))],
        out_specs=pl.BlockSpec((tm, d), row),
        out_shape=jax.ShapeDtypeStruct((bsz * seq, d), F32),
        compiler_params=_params("parallel", "parallel"),
        name="s5_out",
    )(y2, u2, h2, d_skip, w_out, gate)


def _s5_tables(lam_re, lam_im, log_dt, b_re, b_im, c_re, c_im):
    dt = jnp.exp(log_dt.astype(F32))[:, None]
    lr, li = lam_re.astype(F32), lam_im.astype(F32)
    mag = jnp.exp(lr * dt)
    a_re = mag * jnp.cos(li * dt)
    a_im = mag * jnp.sin(li * dt)
    den = lr * lr + li * li
    coef_re = ((a_re - 1.0) * lr + a_im * li) / den
    coef_im = (a_im * lr - (a_re - 1.0) * li) / den
    br_, bi_ = b_re.astype(F32), b_im.astype(F32)
    bbar_re = coef_re[..., None] * br_ - coef_im[..., None] * bi_
    bbar_im = coef_re[..., None] * bi_ + coef_im[..., None] * br_
    g, p, c = bbar_re.shape
    gb = S5_GROUPS_PER_BLOCK
    nblk = g // gb
    eye = jnp.eye(gb, dtype=F32)

    def in_blocks(m):
        return jnp.einsum('kgpc,gh->kgchp', m.reshape(nblk, gb, p, c), eye).reshape(nblk, gb * c, gb * p)

    def out_blocks(m):
        return jnp.einsum('kgcp,gh->kgphc', m.reshape(nblk, gb, c, p), eye).reshape(nblk, gb * p, gb * c)

    bblk = jnp.concatenate([in_blocks(bbar_re), in_blocks(bbar_im)], axis=-1).astype(BF16)
    cblk = jnp.concatenate([out_blocks(c_re.astype(F32)), -out_blocks(c_im.astype(F32))], axis=1).astype(BF16)
    return bblk, cblk, a_re.reshape(nblk, 1, gb * p), a_im.reshape(nblk, 1, gb * p)


def _router_body(h_ref, g_ref, sh_ref, sc_ref, wrh_ref, wrl_ref, br_ref, tri_ref,
                 x_ref, mt_ref, cnt_ref, carry_ref):
    tm, d = h_ref.shape
    ne, ng, epg = N_EXPERTS, N_EXPERT_GROUPS, EXPERTS_PER_GROUP

    @pl.when(pl.program_id(0) == 0)
    def _():
        carry_ref[...] = jnp.zeros_like(carry_ref)

    hn = _rms_mod(h_ref[...], g_ref[...], sh_ref[0], sc_ref[0])
    logits = _dot_3pass(hn, wrh_ref, wrl_ref) + br_ref[...]
    lane = lax.broadcasted_iota(I32, logits.shape, 1).astype(F32)
    big = jnp.float32(1e9)
    ninf = jnp.float32(-jnp.inf)

    gmask = (lane >= ne) & (lane < ne + ng)
    gmax = jnp.max(jnp.where(gmask, logits, ninf), axis=-1, keepdims=True)
    gsum = jnp.sum(jnp.where(gmask, jnp.exp(logits - gmax), 0.0), axis=-1, keepdims=True)
    p_g = 1.0 / gsum
    gidx = jnp.min(jnp.where(gmask & (logits == gmax), lane - ne, big), axis=-1, keepdims=True)

    emask = (lane < ne) & (jnp.floor(lane / epg) == gidx)
    v1 = jnp.max(jnp.where(emask, logits, ninf), axis=-1, keepdims=True)
    i1 = jnp.min(jnp.where(emask & (logits == v1), lane, big), axis=-1, keepdims=True)
    emask2 = emask & (lane != i1)
    v2 = jnp.max(jnp.where(emask2, logits, ninf), axis=-1, keepdims=True)
    i2 = jnp.min(jnp.where(emask2 & (logits == v2), lane, big), axis=-1, keepdims=True)
    e21 = jnp.exp(v2 - v1)
    w1 = p_g / (1.0 + e21)
    w2 = p_g * e21 / (1.0 + e21)

    first_lo = i1 < i2
    e_lo = jnp.where(first_lo, i1, i2)
    e_hi = jnp.where(first_lo, i2, i1)
    w_lo = jnp.where(first_lo, w1, w2)
    w_hi = jnp.where(first_lo, w2, w1)
    bucket = gidx * (epg * epg) + (e_lo - gidx * epg) * epg + (e_hi - gidx * epg)

    lane_b = lax.broadcasted_iota(I32, (tm, N_BUCKETS), 1).astype(F32)
    onehot = (lane_b == bucket).astype(F32)
    prefix = jnp.dot(tri_ref[...], onehot.astype(BF16), preferred_element_type=F32)
    carry = carry_ref[...]
    rank = jnp.sum(onehot * (prefix + carry), axis=-1, keepdims=True) - 1.0
    new_carry = carry + prefix[tm - 1:tm, :]
    carry_ref[...] = new_carry
    cnt_ref[...] = new_carry

    mlane = lax.broadcasted_iota(I32, (tm, META_LANES), 1)
    meta = jnp.where(mlane == 0, bucket,
           jnp.where(mlane == 1, rank,
           jnp.where(mlane == 2, w_lo,
           jnp.where(mlane == 3, w_hi, 0.0))))
    x_ref[:, :d] = hn
    x_ref[:, d:] = meta
    mt_ref[0] = meta.T[:META_ROWS, :]


def _router(h2, g, shift, scale, w_r, b_r, tri, nt_per_batch, tm):
    t, d = h2.shape
    wr_hi, wr_lo = _split_bf16(w_r)
    return pl.pallas_call(
        _router_body,
        grid=(t // tm,),
        in_specs=[pl.BlockSpec((tm, d), lambda i: (i, 0)),
                  pl.BlockSpec((1, d), lambda i: (0, 0)),
                  pl.BlockSpec((1, 1, d), lambda i: (i // nt_per_batch, 0, 0)),
                  pl.BlockSpec((1, 1, d), lambda i: (i // nt_per_batch, 0, 0)),
                  pl.BlockSpec((d, LANES), lambda i: (0, 0)),
                  pl.BlockSpec((d, LANES), lambda i: (0, 0)),
                  pl.BlockSpec((1, LANES), lambda i: (0, 0)),
                  pl.BlockSpec((tm, tm), lambda i: (0, 0))],
        out_specs=[pl.BlockSpec((tm, d + META_LANES), lambda i: (i, 0)),
                   pl.BlockSpec((1, META_ROWS, tm), lambda i: (i, 0, 0)),
                   pl.BlockSpec((1, N_BUCKETS), lambda i: (0, 0))],
        out_shape=[jax.ShapeDtypeStruct((t, d + META_LANES), F32),
                   jax.ShapeDtypeStruct((t // tm, META_ROWS, tm), F32),
                   jax.ShapeDtypeStruct((1, N_BUCKETS), F32)],
        scratch_shapes=[pltpu.VMEM((1, N_BUCKETS), F32)],
        compiler_params=_params("arbitrary"),
        name="moe_router",
    )(h2, g, shift, scale, wr_hi, wr_lo, b_r, tri)


def _plan_body(cnt_ref, mt_ref, pos_ref, maps_ref, start_ref, *, te, nwp):
    nb = N_BUCKETS
    epg = EXPERTS_PER_GROUP

    @pl.when(pl.program_id(0) == 0)
    def _():
        r = lax.broadcasted_iota(I32, (nb, nb), 0)
        c = lax.broadcasted_iota(I32, (nb, nb), 1)
        nt_dims = (((1,), (1,)), ((), ()))

        def column(mask, row_vals):
            row8 = jnp.broadcast_to(row_vals, (SUBLANES, nb)).astype(BF16)
            return lax.dot_general(mask.astype(BF16), row8, nt_dims, preferred_element_type=F32)[:, :1]

        cnt = cnt_ref[...]
        cnt_hi = jnp.floor(cnt / 256.0)
        cnt_lo = cnt - 256.0 * cnt_hi
        start = 256.0 * column(c < r, cnt_hi) + column(c < r, cnt_lo)
        count = 256.0 * column(c == r, cnt_hi) + column(c == r, cnt_lo)
        end = start + count
        start_ref[...] = start
        first_tile = jnp.floor(start / te)
        n_items = jnp.where(count > 0.0, jnp.floor((end - 1.0) / te) - first_tile + 1.0, 0.0)
        items8 = jnp.broadcast_to(n_items, (nb, LANES)).astype(BF16)
        item_end = jnp.dot((c <= r).astype(BF16), items8, preferred_element_type=F32)[:, :1]
        item_start = item_end - n_items
        n_total = item_end[nb - 1:nb, :]

        w = lax.broadcasted_iota(I32, (1, nwp), 1).astype(F32)
        wc = jnp.minimum(w, jnp.maximum(n_total - 1.0, 0.0))
        bucket = jnp.sum((item_end <= wc).astype(F32), axis=0, keepdims=True)
        sel = lax.broadcasted_iota(I32, (nb, nwp), 0).astype(F32) == bucket

        def pick(col):
            return jnp.sum(jnp.where(sel, col, 0.0), axis=0, keepdims=True)

        valid = (w < n_total).astype(F32)
        tile = pick(first_tile) + (wc - pick(item_start))
        row_lo = (jnp.maximum(pick(start), tile * te) - tile * te) * valid
        row_hi = (jnp.minimum(pick(end), (tile + 1.0) * te) - tile * te) * valid
        grp = jnp.floor(bucket / (epg * epg))
        within = bucket - grp * (epg * epg)
        lo = jnp.floor(within / epg)
        e_lo = grp * epg + lo
        e_hi = grp * epg + (within - lo * epg)
        row = lax.broadcasted_iota(I32, (SUBLANES, nwp), 0)
        maps = jnp.where(row == 0, tile, jnp.where(row == 1, e_lo, jnp.where(row == 2, e_hi,
               jnp.where(row == 3, valid, jnp.where(row == 4, row_lo, jnp.where(row == 5, row_hi, 0.0))))))
        maps_ref[...] = maps.astype(I32)

    tm = mt_ref.shape[-1]
    bucket_row = mt_ref[0, 0:1, :]
    rank_row = mt_ref[0, 1:2, :]
    rb = lax.broadcasted_iota(I32, (nb, tm), 0).astype(F32)
    pos = jnp.sum(jnp.where(rb == bucket_row, start_ref[...], 0.0), axis=0, keepdims=True) + rank_row
    pos_ref[0] = pos.astype(I32)


def _plan(counts, meta_t, te, n_items_max):
    n_tt, _, tm = meta_t.shape
    nwp = ((n_items_max + LANES - 1) // LANES) * LANES
    body = functools.partial(_plan_body, te=te, nwp=nwp)
    return pl.pallas_call(
        body,
        grid=(n_tt,),
        in_specs=[pl.BlockSpec((1, N_BUCKETS), lambda i: (0, 0)),
                  pl.BlockSpec((1, META_ROWS, tm), lambda i: (i, 0, 0))],
        out_specs=[pl.BlockSpec((1, 1, tm), lambda i: (i, 0, 0)),
                   pl.BlockSpec((SUBLANES, nwp), lambda i: (0, 0))],
        out_shape=[jax.ShapeDtypeStruct((n_tt, 1, tm), I32),
                   jax.ShapeDtypeStruct((SUBLANES, nwp), I32)],
        scratch_shapes=[pltpu.VMEM((N_BUCKETS, 1), F32)],
        compiler_params=_params("arbitrary"),
        name="moe_plan",
    )(counts, meta_t)


def _row_copy(src_ref, src_row, dst_ref, dst_row, sem):
    return pltpu.make_async_copy(src_ref.at[pl.ds(src_row, 1)], dst_ref.at[pl.ds(dst_row, 1)], sem)


def _dispatch_body(pos_ref, x_ref, xs_ref, sem):
    tm = x_ref.shape[0]

    def issue(grp, c):
        r0 = grp * ROW_DMA_UNROLL
        for u in range(ROW_DMA_UNROLL):
            _row_copy(x_ref, r0 + u, xs_ref, pos_ref[0, 0, r0 + u], sem).start(priority=u % 2)
        return c

    lax.fori_loop(0, tm // ROW_DMA_UNROLL, issue, 0)
    pltpu.make_async_copy(x_ref, xs_ref.at[pl.ds(0, tm)], sem).wait()


def _dispatch(xrow, pos3, tm):
    t, w = xrow.shape
    t_pad = t
    return pl.pallas_call(
        _dispatch_body,
        grid=(t // tm,),
        in_specs=[pl.BlockSpec((1, 1, tm), lambda i: (i, 0, 0), memory_space=pltpu.SMEM),
                  pl.BlockSpec((tm, w), lambda i: (i, 0))],
        out_specs=pl.BlockSpec(memory_space=pl.ANY),
        out_shape=jax.ShapeDtypeStruct((t_pad, w), F32),
        scratch_shapes=[pltpu.SemaphoreType.DMA(())],
        compiler_params=_params("arbitrary"),
        name="moe_dispatch",
    )(pos3, xrow)


def _expert_body(tile_ref, elo_ref, ehi_ref, valid_ref, rlo_ref, rhi_ref, x_ref,
                 w1a_ref, w3a_ref, w2a_ref, w1b_ref, w3b_ref, w2b_ref, o_ref):
    del tile_ref, elo_ref, ehi_ref
    te, d = o_ref.shape
    j = pl.program_id(0)

    @pl.when(valid_ref[j] == 1)
    def _():
        rows = lax.broadcasted_iota(I32, (te, 1), 0)
        live = (rows >= rlo_ref[j]) & (rows < rhi_ref[j])
        x = x_ref[:, :d].astype(BF16)

        def ffn(w1_ref, w3_ref, w2_ref, wt):
            a = jnp.dot(x, w1_ref[...].astype(BF16), preferred_element_type=F32)
            b = jnp.dot(x, w3_ref[...].astype(BF16), preferred_element_type=F32)
            mid = (a * jax.nn.sigmoid(a)) * b
            y = jnp.dot(mid.astype(BF16), w2_ref[...].astype(BF16), preferred_element_type=F32)
            return wt * y

        res = (ffn(w1a_ref, w3a_ref, w2a_ref, jnp.where(live, x_ref[:, d + 2:d + 3], 0.0))
               + ffn(w1b_ref, w3b_ref, w2b_ref, jnp.where(live, x_ref[:, d + 3:d + 4], 0.0)))

        @pl.when(rlo_ref[j] == 0)
        def _():
            o_ref[...] = res

        @pl.when(rlo_ref[j] != 0)
        def _():
            o_ref[...] += res


def _experts(xs, w1, w3, w2, layer, maps, n_items, te):
    t, w = xs.shape
    _, n_e, d, f = w1.shape
    x_spec = pl.BlockSpec((te, w), lambda j, tl, lo, hi, v, a, b: (tl[j], 0))
    up_lo = pl.BlockSpec((None, None, d, f), lambda j, tl, lo, hi, v, a, b: (layer, lo[j], 0, 0))
    dn_lo = pl.BlockSpec((None, None, f, d), lambda j, tl, lo, hi, v, a, b: (layer, lo[j], 0, 0))
    up_hi = pl.BlockSpec((None, None, d, f), lambda j, tl, lo, hi, v, a, b: (layer, hi[j], 0, 0))
    dn_hi = pl.BlockSpec((None, None, f, d), lambda j, tl, lo, hi, v, a, b: (layer, hi[j], 0, 0))
    return pl.pallas_call(
        _expert_body,
        grid_spec=pltpu.PrefetchScalarGridSpec(
            num_scalar_prefetch=6,
            grid=(n_items,),
            in_specs=[x_spec, up_lo, up_lo, dn_lo, up_hi, up_hi, dn_hi],
            out_specs=pl.BlockSpec((te, d), lambda j, tl, lo, hi, v, a, b: (tl[j], 0))),
        out_shape=jax.ShapeDtypeStruct((t, d), F32),
        compiler_params=_params("arbitrary"),
        name="moe_experts",
    )(*[maps[i, :n_items] for i in range(6)], xs, w1, w3, w2, w1, w3, w2)


def _combine_body(pos_ref, ys_ref, h_ref, gate_ref, o_ref, ybuf, sem):
    tm = h_ref.shape[0]

    def issue(grp, c):
        r0 = grp * ROW_DMA_UNROLL
        for u in range(ROW_DMA_UNROLL):
            _row_copy(ys_ref, pos_ref[0, 0, r0 + u], ybuf, r0 + u, sem).start(priority=u % 2)
        return c

    lax.fori_loop(0, tm // ROW_DMA_UNROLL, issue, 0)
    pltpu.make_async_copy(ys_ref.at[pl.ds(0, tm)], ybuf, sem).wait()
    o_ref[...] = h_ref[...] + gate_ref[0] * ybuf[...]


def _combine(ys, pos3, h2, gate, nt_per_batch, tm):
    t, d = h2.shape
    return pl.pallas_call(
        _combine_body,
        grid=(t // tm,),
        in_specs=[pl.BlockSpec((1, 1, tm), lambda i: (i, 0, 0), memory_space=pltpu.SMEM),
                  pl.BlockSpec(memory_space=pl.ANY),
                  pl.BlockSpec((tm, d), lambda i: (i, 0)),
                  pl.BlockSpec((1, 1, d), lambda i: (i // nt_per_batch, 0, 0))],
        out_specs=pl.BlockSpec((tm, d), lambda i: (i, 0)),
        out_shape=jax.ShapeDtypeStruct((t, d), F32),
        scratch_shapes=[pltpu.VMEM((tm, d), F32), pltpu.SemaphoreType.DMA(())],
        compiler_params=_params("arbitrary"),
        name="moe_combine",
    )(pos3, ys, h2, gate)


def _moe(h2, g, shift, scale, gate, wg, bg, we, be, w1, w3, w2, layer, bsz, seq, tm, te):
    t, d = h2.shape
    nt_per_batch = seq // tm
    ne, ng = N_EXPERTS, N_EXPERT_GROUPS
    w_r = jnp.zeros((d, LANES), F32).at[:, :ne].set(we).at[:, ne:ne + ng].set(wg)
    b_r = jnp.zeros((1, LANES), F32).at[0, :ne].set(be).at[0, ne:ne + ng].set(bg)
    tri = jnp.asarray(np.tril(np.ones((tm, tm), np.float32)), BF16)
    xrow, meta_t, counts = _router(h2, g, shift, scale, w_r, b_r, tri, nt_per_batch, tm)
    n_items_max = t // te + N_PAIR_BUCKETS
    pos3, maps = _plan(counts, meta_t, te, n_items_max)
    xs = _dispatch(xrow, pos3, tm)
    ys = _experts(xs, w1, w3, w2, layer, maps, n_items_max, te)
    return _combine(ys, pos3, h2, gate, nt_per_batch, tm)


def _log_sigmoid(x):
    return jnp.minimum(x, 0.0) - jnp.log1p(jnp.exp(-jnp.abs(x)))


def _aug_tables(n_heads):
    assert n_heads * AUG_LANES_PER_HEAD <= LANES
    width = LANES
    pk = np.zeros((F_SPLIT * LANES, width), np.float32)
    pq = np.zeros((F_SPLIT * LANES, width), np.float32)
    ck = np.zeros((1, width), np.float32)
    cq = np.zeros((1, width), np.float32)
    for h in range(n_heads):
        base = h * AUG_LANES_PER_HEAD
        for j in range(F_SPLIT):
            pk[j * LANES + h, base + j] = -1.0
            pq[j * LANES + h, base + F_SPLIT + j] = 1.0
            ck[0, base + F_SPLIT + j] = 1.0
            cq[0, base + j] = 1.0
    return jnp.asarray(pk, BF16), jnp.asarray(pq, BF16), jnp.asarray(ck), jnp.asarray(cq)


def _kv_body(h_ref, g_ref, sh_ref, sc_ref, wk_ref, wvt_ref, wfh_ref, wfl_ref, fb_ref, kng_ref,
             pk_ref, pq_ref, ck_ref, cq_ref, k_ref, vt_ref, ka_ref, qa_ref, carry_ref):
    tm = h_ref.shape[0]

    @pl.when(pl.program_id(1) == 0)
    def _():
        carry_ref[...] = jnp.zeros_like(carry_ref)

    hn = _rms_mod(h_ref[...], g_ref[...], sh_ref[0], sc_ref[0])
    hb = hn.astype(BF16)
    k = jnp.dot(hb, wk_ref[...], preferred_element_type=F32)
    k_ref[...] = _head_rms(k, kng_ref[...]).astype(BF16)
    nt_dims = (((1,), (1,)), ((), ()))
    vt_ref[...] = lax.dot_general(wvt_ref[...], hb, nt_dims, preferred_element_type=F32).astype(BF16)
    fz = _dot_3pass(hn, wfh_ref, wfl_ref) + fb_ref[...]
    c = _log_sigmoid(fz)
    row = lax.broadcasted_iota(I32, c.shape, 0)
    shift = 1
    while shift < tm:
        c = c + jnp.where(row >= shift, pltpu.roll(c, shift, 0), 0.0)
        shift *= 2
    f = c + carry_ref[...]
    carry_ref[...] = f[tm - 1:tm, :]

    f2 = f * LOG2E
    hi = f2.astype(BF16)
    r1 = f2 - hi.astype(F32)
    mid = r1.astype(BF16)
    lo = (r1 - mid.astype(F32)).astype(BF16)
    pieces = jnp.concatenate([hi, mid, lo], axis=-1)
    ka_ref[...] = (jnp.dot(pieces, pk_ref[...], preferred_element_type=F32) + ck_ref[...]).astype(BF16)
    qa_ref[...] = (jnp.dot(pieces, pq_ref[...], preferred_element_type=F32) + cq_ref[...]).astype(BF16)


def _shared_kv(h2, g, shift, scale, wk, wvt, wf, fb, kng, bsz, seq, tm):
    t, d = h2.shape
    nt = seq // tm
    aw = LANES
    pk, pq, ck, cq = _aug_tables(d // HEAD_DIM)
    wf_hi, wf_lo = _split_bf16(wf)
    row = lambda b, i: (b * nt + i, 0)
    const = lambda b, i: (0, 0)
    return pl.pallas_call(
        _kv_body,
        grid=(bsz, nt),
        in_specs=[pl.BlockSpec((tm, d), row),
                  pl.BlockSpec((1, d), const),
                  pl.BlockSpec((1, 1, d), lambda b, i: (b, 0, 0)),
                  pl.BlockSpec((1, 1, d), lambda b, i: (b, 0, 0)),
                  pl.BlockSpec((d, d), const),
                  pl.BlockSpec((d, d), const),
                  pl.BlockSpec((d, LANES), const),
                  pl.BlockSpec((d, LANES), const),
                  pl.BlockSpec((1, LANES), const),
                  pl.BlockSpec((1, d), const),
                  pl.BlockSpec((F_SPLIT * LANES, aw), const),
                  pl.BlockSpec((F_SPLIT * LANES, aw), const),
                  pl.BlockSpec((1, aw), const),
                  pl.BlockSpec((1, aw), const)],
        out_specs=[pl.BlockSpec((tm, d), row),
                   pl.BlockSpec((d, tm), lambda b, i: (b, i)),
                   pl.BlockSpec((tm, aw), row),
                   pl.BlockSpec((tm, aw), row)],
        out_shape=[jax.ShapeDtypeStruct((t, d), BF16), jax.ShapeDtypeStruct((bsz * d, seq), BF16),
                   jax.ShapeDtypeStruct((t, aw), BF16), jax.ShapeDtypeStruct((t, aw), BF16)],
        scratch_shapes=[pltpu.VMEM((1, LANES), F32)],
        compiler_params=_params("parallel", "arbitrary"),
        name="shared_kv",
    )(h2, g, shift, scale, wk, wvt, wf_hi, wf_lo, fb, kng, pk, pq, ck, cq)


def _qg_body(h_ref, g_ref, sh_ref, sc_ref, wq_ref, wg_ref, qng_ref, q_ref, og_ref):
    hn = _rms_mod(h_ref[...], g_ref[...], sh_ref[0], sc_ref[0])
    hb = hn.astype(BF16)
    q = jnp.dot(hb, wq_ref[...], preferred_element_type=F32)
    q_ref[...] = (_head_rms(q, qng_ref[...]) * (HEAD_DIM ** -0.5 * LOG2E)).astype(BF16)
    og_ref[...] = jnp.dot(hb, wg_ref[...], preferred_element_type=F32).astype(BF16)


def _fox_qg(h2, g, shift, scale, wq, wg, qng, bsz, seq, tm):
    t, d = h2.shape
    nt = seq // tm
    row = lambda b, i: (b * nt + i, 0)
    const = lambda b, i: (0, 0)
    return pl.pallas_call(
        _qg_body,
        grid=(bsz, nt),
        in_specs=[pl.BlockSpec((tm, d), row),
                  pl.BlockSpec((1, d), const),
                  pl.BlockSpec((1, 1, d), lambda b, i: (b, 0, 0)),
                  pl.BlockSpec((1, 1, d), lambda b, i: (b, 0, 0)),
                  pl.BlockSpec((d, d), const),
                  pl.BlockSpec((d, d), const),
                  pl.BlockSpec((1, d), const)],
        out_specs=[pl.BlockSpec((tm, d), row), pl.BlockSpec((tm, d), row)],
        out_shape=[jax.ShapeDtypeStruct((t, d), BF16), jax.ShapeDtypeStruct((t, d), BF16)],
        compiler_params=_params("parallel", "parallel"),
        name="fox_qg",
    )(h2, g, shift, scale, wq, wg, qng)


def _attn_body(q_ref, qa_ref, k_ref, ka_ref, vt_ref, o_ref, m_ref, l_ref, acc_ref, s0_ref, s1_ref,
               *, tq, hp):
    grp = pl.program_id(1)
    qi = pl.program_id(2)
    n_slabs = hp // 2
    lane_q = lax.broadcasted_iota(I32, (tq, LANES), 1)
    lo_q = lane_q < HEAD_DIM
    lo_v = lax.broadcasted_iota(I32, (LANES, tq), 0) < HEAD_DIM
    nt_dims = (((1,), (1,)), ((), ()))
    causal = lax.broadcasted_iota(I32, (tq, tq), 0) <= lax.broadcasted_iota(I32, (tq, tq), 1)

    qa = qa_ref[...]
    qcats = []
    for h in range(hp):
        sl, hh = h // 2, h % 2
        q2 = q_ref[:, sl * LANES:(sl + 1) * LANES]
        own = lo_q if hh == 0 else jnp.logical_not(lo_q)
        a0 = (grp * hp + h) * AUG_LANES_PER_HEAD
        own_a = (lane_q >= a0) & (lane_q < a0 + AUG_LANES_PER_HEAD)
        qcats.append(jnp.concatenate([jnp.where(own, q2, jnp.zeros_like(q2)),
                                      jnp.where(own_a, qa, jnp.zeros_like(qa))], axis=-1))

    m_ref[...] = jnp.full(m_ref.shape, NEG, F32)
    l_ref[...] = jnp.zeros(l_ref.shape, F32)
    acc_ref[...] = jnp.zeros(acc_ref.shape, F32)

    def scores(kb, dst_ref):
        s0 = pl.multiple_of(kb * tq, tq)
        for h in range(hp):
            sl = h // 2
            kcat = jnp.concatenate([k_ref[pl.ds(s0, tq), sl * LANES:(sl + 1) * LANES],
                                    ka_ref[pl.ds(s0, tq), :]], axis=-1)
            dst_ref[h] = lax.dot_general(kcat, qcats[h], nt_dims, preferred_element_type=F32)

    def consume(kb, src_ref, masked):
        s0 = pl.multiple_of(kb * tq, tq)
        for sl in range(n_slabs):
            vt = vt_ref[sl * LANES:(sl + 1) * LANES, pl.ds(s0, tq)]
            vts = (jnp.where(lo_v, vt, jnp.zeros_like(vt)), jnp.where(lo_v, jnp.zeros_like(vt), vt))
            pv, alphas = None, []
            for hh in range(2):
                h = sl * 2 + hh
                st = src_ref[h]
                if masked:
                    st = jnp.where(causal, st, NEG)
                m_old = m_ref[h]
                m_new = jnp.maximum(m_old, jnp.max(st, axis=0, keepdims=True))
                alpha = jnp.exp2(m_old - m_new)
                p = jnp.exp2(st - m_new)
                l_ref[h] = alpha * l_ref[h] + jnp.sum(p, axis=0, keepdims=True)
                m_ref[h] = m_new
                part = jnp.dot(vts[hh], p.astype(BF16), preferred_element_type=F32)
                pv = part if pv is None else pv + part
                alphas.append(alpha)
            acc_ref[sl] = jnp.where(lo_v, alphas[0], alphas[1]) * acc_ref[sl] + pv

    scores(0, s0_ref)

    def pair(j, c):
        scores(2 * j + 1, s1_ref)
        consume(2 * j, s0_ref, False)
        scores(2 * j + 2, s0_ref)
        consume(2 * j + 1, s1_ref, False)
        return c

    lax.fori_loop(0, qi // 2, pair, 0)

    @pl.when(qi % 2 == 0)
    def _():
        consume(qi, s0_ref, True)

    @pl.when(qi % 2 == 1)
    def _():
        scores(qi, s1_ref)
        consume(qi - 1, s0_ref, False)
        consume(qi, s1_ref, True)

    for sl in range(n_slabs):
        l2 = jnp.where(lo_v, l_ref[2 * sl], l_ref[2 * sl + 1])
        o_ref[:, sl * LANES:(sl + 1) * LANES] = (acc_ref[sl] / l2).T.astype(BF16)


def _fox_attention(q, qaug, k, kaug, vt, bsz, seq, tq, hp):
    t, d = q.shape
    w = hp * HEAD_DIM
    n_grp = d // w
    nq = seq // tq
    body = functools.partial(_attn_body, tq=tq, hp=hp)
    qrow = lambda b, j, i: (b * nq + i, j)
    krow = lambda b, j, i: (b, j)
    return pl.pallas_call(
        body,
        grid=(bsz, n_grp, nq),
        in_specs=[pl.BlockSpec((tq, w), qrow),
                  pl.BlockSpec((tq, LANES), lambda b, j, i: (b * nq + i, 0)),
                  pl.BlockSpec((seq, w), krow),
                  pl.BlockSpec((seq, LANES), lambda b, j, i: (b, 0)),
                  pl.BlockSpec((w, seq), lambda b, j, i: (b * n_grp + j, 0))],
        out_specs=pl.BlockSpec((tq, w), qrow),
        out_shape=jax.ShapeDtypeStruct((t, d), BF16),
        scratch_shapes=[pltpu.VMEM((hp, 1, tq), F32), pltpu.VMEM((hp, 1, tq), F32),
                        pltpu.VMEM((hp // 2, LANES, tq), F32),
                        pltpu.VMEM((hp, tq, tq), F32), pltpu.VMEM((hp, tq, tq), F32)],
        compiler_params=_params("parallel", "parallel", "arbitrary"),
        name="fox_attention",
    )(q, qaug, k, kaug, vt)


def _fox_out_body(o_ref, og_ref, h_ref, w_ref, gate_ref, out_ref):
    z = o_ref[...].astype(F32) * jax.nn.sigmoid(og_ref[...].astype(F32))
    out_ref[...] = h_ref[...] + gate_ref[0] * jnp.dot(z.astype(BF16), w_ref[...], preferred_element_type=F32)


def _fox_out(o, og, h2, w_o, gate, bsz, seq, tm):
    t, d = h2.shape
    nt = seq // tm
    row = lambda b, i: (b * nt + i, 0)
    return pl.pallas_call(
        _fox_out_body,
        grid=(bsz, nt),
        in_specs=[pl.BlockSpec((tm, d), row), pl.BlockSpec((tm, d), row), pl.BlockSpec((tm, d), row),
                  pl.BlockSpec((d, d), lambda b, i: (0, 0)),
                  pl.BlockSpec((1, 1, d), lambda b, i: (b, 0, 0))],
        out_specs=pl.BlockSpec((tm, d), row),
        out_shape=jax.ShapeDtypeStruct((t, d), F32),
        compiler_params=_params("parallel", "parallel"),
        name="fox_out",
    )(o, og, h2, w_o, gate)


def _tiles(seq):
    tm = min(512, seq)
    te = min(256, seq)
    tq = min(512, seq)
    tc = min(32, seq)
    return tm, te, tq, tc


def kernel(x, c, ln_g, ada_w, ada_b, s5_w_in, s5_lambda_re, s5_lambda_im, s5_log_dt, s5_b_re, s5_b_im,
           s5_c_re, s5_c_im, s5_d, s5_w_out, kv_g, kv_ada_w, kv_ada_b, kv_w, kv_fb, k_norm_g,
           fox_w_qg, fox_q_norm_g, fox_w_o, moe_wg, moe_bg, moe_we, moe_be, moe_w1, moe_w3, moe_w2):
    bsz, seq, d = x.shape
    depth = ln_g.shape[0]
    n_a = s5_w_in.shape[0]
    n_heads = d // HEAD_DIM
    tm, te, tq, tc = _tiles(seq)

    mods = _adaln(c, ada_w.reshape(depth * 2, d, 3 * d), ada_b.reshape(depth * 2, 1, 3 * d))
    mods = mods.reshape(depth, 2, bsz, 3, 1, d)
    kv_mods = _adaln(c, kv_ada_w[None], kv_ada_b[None, None]).reshape(bsz, 2, 1, d)

    h = x.reshape(bsz * seq, d)
    k = kaug = qaug = vt = None
    for l in range(depth):
        shift, scale, gate = mods[l, 0, :, 0], mods[l, 0, :, 1], mods[l, 0, :, 2]
        g = ln_g[l, 0][None]
        if l < n_a:
            u2 = _s5_in(h, g, shift, scale, s5_w_in[l].astype(BF16), bsz, seq, tm)
            bblk, cblk, a_re, a_im = _s5_tables(s5_lambda_re[l], s5_lambda_im[l], s5_log_dt[l],
                                                s5_b_re[l], s5_b_im[l], s5_c_re[l], s5_c_im[l])
            y2 = _s5_scan(u2, bblk, cblk, a_re, a_im, bsz, seq, tc)
            h = _s5_out(y2, u2, h, s5_d[l][None], s5_w_out[l].astype(BF16), gate, bsz, seq, tm)
        else:
            j = l - n_a
            qng = jnp.tile(fox_q_norm_g[j], n_heads)[None]
            q, og = _fox_qg(h, g, shift, scale, fox_w_qg[j][:, :d].astype(BF16),
                            fox_w_qg[j][:, d:].astype(BF16), qng, bsz, seq, tm)
            o = _fox_attention(q, qaug, k, kaug, vt, bsz, seq, tq, ATTN_HEADS_PER_STEP)
            h = _fox_out(o, og, h, fox_w_o[j].astype(BF16), gate, bsz, seq, tm)

        shift, scale, gate = mods[l, 1, :, 0], mods[l, 1, :, 1], mods[l, 1, :, 2]
        h = _moe(h, ln_g[l, 1][None], shift, scale, gate, moe_wg[l], moe_bg[l], moe_we[l], moe_be[l],
                 moe_w1, moe_w3, moe_w2, l, bsz, seq, tm, te)

        if l == n_a - 1:
            wf = jnp.zeros((d, LANES), F32).at[:, :n_heads].set(kv_w[:, 2 * d:])
            fb = jnp.zeros((1, LANES), F32).at[0, :n_heads].set(kv_fb)
            kng = jnp.tile(k_norm_g, n_heads)[None]
            k, vt, kaug, qaug = _shared_kv(h, kv_g[None], kv_mods[:, 0], kv_mods[:, 1],
                                           kv_w[:, :d].astype(BF16), kv_w[:, d:2 * d].T.astype(BF16),
                                           wf, fb, kng, bsz, seq, tm)
    return h.reshape(bsz, seq, d)
```

```python
import functools
import math

import numpy as np
import jax
import jax.numpy as jnp
from jax import lax
from jax.experimental import pallas as pl
from jax.experimental.pallas import tpu as pltpu

F32 = jnp.float32
BF16 = jnp.bfloat16
I32 = jnp.int32

EPS = 1e-6
NEG = -1e30
LOG2E = math.log2(math.e)
LANES = 128
SUBLANES = 8
VMEM_LIMIT_BYTES = 56 * 1024 * 1024

S5_GROUPS_PER_BLOCK = 16
N_EXPERT_GROUPS = 4
EXPERTS_PER_GROUP = 8
N_EXPERTS = N_EXPERT_GROUPS * EXPERTS_PER_GROUP
N_BUCKETS = N_EXPERT_GROUPS * EXPERTS_PER_GROUP * EXPERTS_PER_GROUP
N_PAIR_BUCKETS = N_EXPERT_GROUPS * (EXPERTS_PER_GROUP * (EXPERTS_PER_GROUP - 1) // 2)
META_LANES = LANES
META_ROWS = SUBLANES
ROW_DMA_UNROLL = 8
HEAD_DIM = 64
ATTN_HEADS_PER_STEP = 4
DEN_ROWS = 16
F_SPLIT = 3
AUG_LANES_PER_HEAD = 2 * F_SPLIT


def _params(*sem):
    return pltpu.CompilerParams(dimension_semantics=sem, vmem_limit_bytes=VMEM_LIMIT_BYTES)


def _rms_mod(x, g, shift, scale):
    ms = jnp.mean(x * x, axis=-1, keepdims=True)
    y = x * lax.rsqrt(ms + EPS) * g
    return y * (1.0 + scale) + shift


def _split_bf16(w):
    hi = w.astype(BF16)
    return hi, (w - hi.astype(F32)).astype(BF16)


def _dot_3pass(x, w_hi_ref, w_lo_ref):
    x_hi, x_lo = _split_bf16(x)
    w_hi = w_hi_ref[...]
    return (jnp.dot(x_hi, w_hi, preferred_element_type=F32)
            + jnp.dot(x_lo, w_hi, preferred_element_type=F32)
            + jnp.dot(x_hi, w_lo_ref[...], preferred_element_type=F32))


def _head_rms(x, g):
    tm, d = x.shape
    lane = lax.broadcasted_iota(I32, (tm, LANES), 1)
    lo = lane < HEAD_DIM
    outs = []
    for j in range(d // LANES):
        s = x[:, j * LANES:(j + 1) * LANES]
        sq = s * s
        s_lo = jnp.sum(jnp.where(lo, sq, 0.0), axis=-1, keepdims=True)
        s_hi = jnp.sum(jnp.where(lo, 0.0, sq), axis=-1, keepdims=True)
        r = jnp.where(lo, lax.rsqrt(s_lo / HEAD_DIM + EPS), lax.rsqrt(s_hi / HEAD_DIM + EPS))
        outs.append(s * r)
    return jnp.concatenate(outs, axis=-1) * g


def _adaln_body(c_ref, w_ref, b_ref, o_ref):
    c = c_ref[...]
    s = c * jax.nn.sigmoid(c)
    o_ref[0] = jnp.dot(s, w_ref[0], preferred_element_type=F32) + b_ref[0]


def _adaln(c, w, b):
    n_sets, d, n = w.shape
    bsz = c.shape[0]
    tn = 512 if n % 512 == 0 else n
    return pl.pallas_call(
        _adaln_body,
        grid=(n_sets, n // tn),
        in_specs=[pl.BlockSpec((bsz, d), lambda s, j: (0, 0)),
                  pl.BlockSpec((1, d, tn), lambda s, j: (s, 0, j)),
                  pl.BlockSpec((1, 1, tn), lambda s, j: (s, 0, j))],
        out_specs=pl.BlockSpec((1, bsz, tn), lambda s, j: (s, 0, j)),
        out_shape=jax.ShapeDtypeStruct((n_sets, bsz, n), F32),
        compiler_params=_params("parallel", "parallel"),
        name="adaln",
    )(c, w, b)


def _s5_in_body(x_ref, g_ref, sh_ref, sc_ref, w_ref, u_ref):
    hn = _rms_mod(x_ref[...], g_ref[...], sh_ref[0], sc_ref[0])
    u_ref[...] = jnp.dot(hn.astype(BF16), w_ref[...], preferred_element_type=F32)


def _s5_in(x2, g, shift, scale, w_in, bsz, seq, tm):
    d = x2.shape[1]
    nt = seq // tm
    row = lambda b, i: (b * nt + i, 0)
    return pl.pallas_call(
        _s5_in_body,
        grid=(bsz, nt),
        in_specs=[pl.BlockSpec((tm, d), row),
                  pl.BlockSpec((1, d), lambda b, i: (0, 0)),
                  pl.BlockSpec((1, 1, d), lambda b, i: (b, 0, 0)),
                  pl.BlockSpec((1, 1, d), lambda b, i: (b, 0, 0)),
                  pl.BlockSpec((d, d), lambda b, i: (0, 0))],
        out_specs=pl.BlockSpec((tm, d), row),
        out_shape=jax.ShapeDtypeStruct((bsz * seq, d), F32),
        compiler_params=_params("parallel", "parallel"),
        name="s5_in",
    )(x2, g, shift, scale, w_in)


def _s5_scan_body(u_hbm, bb_ref, cb_ref, are_ref, aim_ref, y_hbm,
                  ubuf, ybuf, bu_ref, st_ref, sem_in, sem_out, *, tc, nblk, sw, seq, n_chunks):
    bsz = SUBLANES
    cw = S5_GROUPS_PER_BLOCK * 16
    i = pl.program_id(0)
    slot = lax.rem(i, 2)

    def in_copy(chunk, sl, b):
        return pltpu.make_async_copy(u_hbm.at[pl.ds(b * seq + chunk * tc, tc)],
                                     ubuf.at[sl, :, b, :], sem_in.at[sl])

    def out_copy(chunk, sl, b):
        return pltpu.make_async_copy(ybuf.at[sl, :, b, :],
                                     y_hbm.at[pl.ds(b * seq + chunk * tc, tc)], sem_out.at[sl])

    @pl.when(i == 0)
    def _():
        st_ref[...] = jnp.zeros_like(st_ref)
        for b in range(bsz):
            in_copy(0, 0, b).start()

    @pl.when(i + 1 < n_chunks)
    def _():
        for b in range(bsz):
            in_copy(i + 1, 1 - slot, b).start()

    for b in range(bsz):
        in_copy(i, slot, b).wait()

    @pl.when(i >= 2)
    def _():
        for b in range(bsz):
            out_copy(i - 2, slot, b).wait()

    d = ubuf.shape[-1]
    u2 = ubuf[slot].reshape(tc * bsz, d).astype(BF16)
    for k in range(nblk):
        bu_ref[:, k * 2 * sw:(k + 1) * 2 * sw] = jnp.dot(
            u2[:, k * cw:(k + 1) * cw], bb_ref[k], preferred_element_type=F32)

    for k in range(nblk):
        re0, im0 = k * 2 * sw, k * 2 * sw + sw
        a_re = jnp.broadcast_to(are_ref[k], (SUBLANES, sw))
        a_im = jnp.broadcast_to(aim_ref[k], (SUBLANES, sw))

        def step(t, carry, re0=re0, im0=im0, a_re=a_re, a_im=a_im):
            s_re, s_im = carry
            r0 = pl.multiple_of(t * SUBLANES, SUBLANES)
            n_re = a_re * s_re - a_im * s_im + bu_ref[pl.ds(r0, SUBLANES), re0:re0 + sw]
            n_im = a_re * s_im + a_im * s_re + bu_ref[pl.ds(r0, SUBLANES), im0:im0 + sw]
            bu_ref[pl.ds(r0, SUBLANES), re0:re0 + sw] = n_re
            bu_ref[pl.ds(r0, SUBLANES), im0:im0 + sw] = n_im
            return n_re, n_im

        s_re, s_im = lax.fori_loop(
            0, tc, step, (st_ref[:, re0:re0 + sw], st_ref[:, im0:im0 + sw]), unroll=True)
        st_ref[:, re0:re0 + sw] = s_re
        st_ref[:, im0:im0 + sw] = s_im

    for k in range(nblk):
        s2 = bu_ref[:, k * 2 * sw:(k + 1) * 2 * sw].astype(BF16)
        yk = jnp.dot(s2, cb_ref[k], preferred_element_type=F32)
        ybuf[slot, :, :, k * cw:(k + 1) * cw] = yk.reshape(tc, bsz, cw)

    for b in range(bsz):
        out_copy(i, slot, b).start()

    @pl.when(i == n_chunks - 1)
    def _():
        if n_chunks >= 2:
            for b in range(bsz):
                out_copy(i - 1, 1 - slot, b).wait()
        for b in range(bsz):
            out_copy(i, slot, b).wait()


def _s5_scan(u2, bblk, cblk, a_re, a_im, bsz, seq, tc):
    assert bsz == SUBLANES, "the scan keeps the batch on the 8 sublanes of a vreg"
    rows, d = u2.shape
    nblk, cw, sw2 = bblk.shape
    sw = sw2 // 2
    n_chunks = seq // tc
    body = functools.partial(_s5_scan_body, tc=tc, nblk=nblk, sw=sw, seq=seq, n_chunks=n_chunks)
    return pl.pallas_call(
        body,
        grid=(n_chunks,),
        in_specs=[pl.BlockSpec(memory_space=pl.ANY),
                  pl.BlockSpec((nblk, cw, sw2), lambda i: (0, 0, 0)),
                  pl.BlockSpec((nblk, sw2, cw), lambda i: (0, 0, 0)),
                  pl.BlockSpec((nblk, 1, sw), lambda i: (0, 0, 0)),
                  pl.BlockSpec((nblk, 1, sw), lambda i: (0, 0, 0))],
        out_specs=pl.BlockSpec(memory_space=pl.ANY),
        out_shape=jax.ShapeDtypeStruct((rows, d), F32),
        scratch_shapes=[pltpu.VMEM((2, tc, bsz, d), F32), pltpu.VMEM((2, tc, bsz, d), F32),
                        pltpu.VMEM((tc * bsz, nblk * sw2), F32), pltpu.VMEM((bsz, nblk * sw2), F32),
                        pltpu.SemaphoreType.DMA((2,)), pltpu.SemaphoreType.DMA((2,))],
        compiler_params=_params("arbitrary"),
        name="s5_scan",
    )(u2, bblk, cblk, a_re, a_im)


def _s5_out_body(y_ref, u_ref, h_ref, d_ref, w_ref, gate_ref, o_ref):
    z = y_ref[...] + d_ref[...] * u_ref[...]
    act = jax.nn.gelu(z)
    vg = jnp.dot(act.astype(BF16), w_ref[...], preferred_element_type=F32)
    d = z.shape[-1]
    mix = vg[:, :d] * jax.nn.sigmoid(vg[:, d:])
    o_ref[...] = h_ref[...] + gate_ref[0] * mix


def _s5_out(y2, u2, h2, d_skip, w_out, gate, bsz, seq, tm):
    d = h2.shape[1]
    nt = seq // tm
    row = lambda b, i: (b * nt + i, 0)
    return pl.pallas_call(
        _s5_out_body,
        grid=(bsz, nt),
        in_specs=[pl.BlockSpec((tm, d), row),
                  pl.BlockSpec((tm, d), row),
                  pl.BlockSpec((tm, d), row),
                  pl.BlockSpec((1, d), lambda b, i: (0, 0)),
                  pl.BlockSpec((d, 2 * d), lambda b, i: (0, 0)),
                  pl.BlockSpec((1, 1, d), lambda b, i: (b, 0, 0))],
        out_specs=pl.BlockSpec((tm, d), row),
        out_shape=jax.ShapeDtypeStruct((bsz * seq, d), F32),
        compiler_params=_params("parallel", "parallel"),
        name="s5_out",
    )(y2, u2, h2, d_skip, w_out, gate)


def _s5_tables(lam_re, lam_im, log_dt, b_re, b_im, c_re, c_im):
    dt = jnp.exp(log_dt.astype(F32))[:, None]
    lr, li = lam_re.astype(F32), lam_im.astype(F32)
    mag = jnp.exp(lr * dt)
    a_re = mag * jnp.cos(li * dt)
    a_im = mag * jnp.sin(li * dt)
    den = lr * lr + li * li
    coef_re = ((a_re - 1.0) * lr + a_im * li) / den
    coef_im = (a_im * lr - (a_re - 1.0) * li) / den
    br_, bi_ = b_re.astype(F32), b_im.astype(F32)
    bbar_re = coef_re[..., None] * br_ - coef_im[..., None] * bi_
    bbar_im = coef_re[..., None] * bi_ + coef_im[..., None] * br_
    g, p, c = bbar_re.shape
    gb = S5_GROUPS_PER_BLOCK
    nblk = g // gb
    eye = jnp.eye(gb, dtype=F32)

    def in_blocks(m):
        return jnp.einsum('kgpc,gh->kgchp', m.reshape(nblk, gb, p, c), eye).reshape(nblk, gb * c, gb * p)

    def out_blocks(m):
        return jnp.einsum('kgcp,gh->kgphc', m.reshape(nblk, gb, c, p), eye).reshape(nblk, gb * p, gb * c)

    bblk = jnp.concatenate([in_blocks(bbar_re), in_blocks(bbar_im)], axis=-1).astype(BF16)
    cblk = jnp.concatenate([out_blocks(c_re.astype(F32)), -out_blocks(c_im.astype(F32))], axis=1).astype(BF16)
    return bblk, cblk, a_re.reshape(nblk, 1, gb * p), a_im.reshape(nblk, 1, gb * p)


def _router_body(h_ref, g_ref, sh_ref, sc_ref, wrh_ref, wrl_ref, br_ref, tri_ref,
                 x_ref, mt_ref, cnt_ref, carry_ref):
    tm, d = h_ref.shape
    ne, ng, epg = N_EXPERTS, N_EXPERT_GROUPS, EXPERTS_PER_GROUP

    @pl.when(pl.program_id(0) == 0)
    def _():
        carry_ref[...] = jnp.zeros_like(carry_ref)

    hn = _rms_mod(h_ref[...], g_ref[...], sh_ref[0], sc_ref[0])
    logits = _dot_3pass(hn, wrh_ref, wrl_ref) + br_ref[...]
    lane = lax.broadcasted_iota(I32, logits.shape, 1).astype(F32)
    big = jnp.float32(1e9)
    ninf = jnp.float32(-jnp.inf)

    gmask = (lane >= ne) & (lane < ne + ng)
    gmax = jnp.max(jnp.where(gmask, logits, ninf), axis=-1, keepdims=True)
    gsum = jnp.sum(jnp.where(gmask, jnp.exp(logits - gmax), 0.0), axis=-1, keepdims=True)
    p_g = 1.0 / gsum
    gidx = jnp.min(jnp.where(gmask & (logits == gmax), lane - ne, big), axis=-1, keepdims=True)

    emask = (lane < ne) & (jnp.floor(lane / epg) == gidx)
    v1 = jnp.max(jnp.where(emask, logits, ninf), axis=-1, keepdims=True)
    i1 = jnp.min(jnp.where(emask & (logits == v1), lane, big), axis=-1, keepdims=True)
    emask2 = emask & (lane != i1)
    v2 = jnp.max(jnp.where(emask2, logits, ninf), axis=-1, keepdims=True)
    i2 = jnp.min(jnp.where(emask2 & (logits == v2), lane, big), axis=-1, keepdims=True)
    e21 = jnp.exp(v2 - v1)
    w1 = p_g / (1.0 + e21)
    w2 = p_g * e21 / (1.0 + e21)

    first_lo = i1 < i2
    e_lo = jnp.where(first_lo, i1, i2)
    e_hi = jnp.where(first_lo, i2, i1)
    w_lo = jnp.where(first_lo, w1, w2)
    w_hi = jnp.where(first_lo, w2, w1)
    bucket = gidx * (epg * epg) + (e_lo - gidx * epg) * epg + (e_hi - gidx * epg)

    lane_b = lax.broadcasted_iota(I32, (tm, N_BUCKETS), 1).astype(F32)
    onehot = (lane_b == bucket).astype(F32)
    prefix = jnp.dot(tri_ref[...], onehot.astype(BF16), preferred_element_type=F32)
    carry = carry_ref[...]
    rank = jnp.sum(onehot * (prefix + carry), axis=-1, keepdims=True) - 1.0
    new_carry = carry + prefix[tm - 1:tm, :]
    carry_ref[...] = new_carry
    cnt_ref[...] = new_carry

    mlane = lax.broadcasted_iota(I32, (tm, META_LANES), 1)
    meta = jnp.where(mlane == 0, bucket,
           jnp.where(mlane == 1, rank,
           jnp.where(mlane == 2, w_lo,
           jnp.where(mlane == 3, w_hi, 0.0))))
    x_ref[:, :d] = hn
    x_ref[:, d:] = meta
    mt_ref[0] = meta.T[:META_ROWS, :]


def _router(h2, g, shift, scale, w_r, b_r, tri, nt_per_batch, tm):
    t, d = h2.shape
    wr_hi, wr_lo = _split_bf16(w_r)
    return pl.pallas_call(
        _router_body,
        grid=(t // tm,),
        in_specs=[pl.BlockSpec((tm, d), lambda i: (i, 0)),
                  pl.BlockSpec((1, d), lambda i: (0, 0)),
                  pl.BlockSpec((1, 1, d), lambda i: (i // nt_per_batch, 0, 0)),
                  pl.BlockSpec((1, 1, d), lambda i: (i // nt_per_batch, 0, 0)),
                  pl.BlockSpec((d, LANES), lambda i: (0, 0)),
                  pl.BlockSpec((d, LANES), lambda i: (0, 0)),
                  pl.BlockSpec((1, LANES), lambda i: (0, 0)),
                  pl.BlockSpec((tm, tm), lambda i: (0, 0))],
        out_specs=[pl.BlockSpec((tm, d + META_LANES), lambda i: (i, 0)),
                   pl.BlockSpec((1, META_ROWS, tm), lambda i: (i, 0, 0)),
                   pl.BlockSpec((1, N_BUCKETS), lambda i: (0, 0))],
        out_shape=[jax.ShapeDtypeStruct((t, d + META_LANES), F32),
                   jax.ShapeDtypeStruct((t // tm, META_ROWS, tm), F32),
                   jax.ShapeDtypeStruct((1, N_BUCKETS), F32)],
        scratch_shapes=[pltpu.VMEM((1, N_BUCKETS), F32)],
        compiler_params=_params("arbitrary"),
        name="moe_router",
    )(h2, g, shift, scale, wr_hi, wr_lo, b_r, tri)


def _plan_body(cnt_ref, mt_ref, pos_ref, maps_ref, start_ref, *, te, nwp):
    nb = N_BUCKETS
    epg = EXPERTS_PER_GROUP

    @pl.when(pl.program_id(0) == 0)
    def _():
        r = lax.broadcasted_iota(I32, (nb, nb), 0)
        c = lax.broadcasted_iota(I32, (nb, nb), 1)
        nt_dims = (((1,), (1,)), ((), ()))

        def column(mask, row_vals):
            row8 = jnp.broadcast_to(row_vals, (SUBLANES, nb)).astype(BF16)
            return lax.dot_general(mask.astype(BF16), row8, nt_dims, preferred_element_type=F32)[:, :1]

        cnt = cnt_ref[...]
        cnt_hi = jnp.floor(cnt / 256.0)
        cnt_lo = cnt - 256.0 * cnt_hi
        start = 256.0 * column(c < r, cnt_hi) + column(c < r, cnt_lo)
        count = 256.0 * column(c == r, cnt_hi) + column(c == r, cnt_lo)
        end = start + count
        start_ref[...] = start
        first_tile = jnp.floor(start / te)
        n_items = jnp.where(count > 0.0, jnp.floor((end - 1.0) / te) - first_tile + 1.0, 0.0)
        items8 = jnp.broadcast_to(n_items, (nb, LANES)).astype(BF16)
        item_end = jnp.dot((c <= r).astype(BF16), items8, preferred_element_type=F32)[:, :1]
        item_start = item_end - n_items
        n_total = item_end[nb - 1:nb, :]

        w = lax.broadcasted_iota(I32, (1, nwp), 1).astype(F32)
        wc = jnp.minimum(w, jnp.maximum(n_total - 1.0, 0.0))
        bucket = jnp.sum((item_end <= wc).astype(F32), axis=0, keepdims=True)
        sel = lax.broadcasted_iota(I32, (nb, nwp), 0).astype(F32) == bucket

        def pick(col):
            return jnp.sum(jnp.where(sel, col, 0.0), axis=0, keepdims=True)

        valid = (w < n_total).astype(F32)
        tile = pick(first_tile) + (wc - pick(item_start))
        row_lo = (jnp.maximum(pick(start), tile * te) - tile * te) * valid
        row_hi = (jnp.minimum(pick(end), (tile + 1.0) * te) - tile * te) * valid
        grp = jnp.floor(bucket / (epg * epg))
        within = bucket - grp * (epg * epg)
        lo = jnp.floor(within / epg)
        e_lo = grp * epg + lo
        e_hi = grp * epg + (within - lo * epg)
        row = lax.broadcasted_iota(I32, (SUBLANES, nwp), 0)
        maps = jnp.where(row == 0, tile, jnp.where(row == 1, e_lo, jnp.where(row == 2, e_hi,
               jnp.where(row == 3, valid, jnp.where(row == 4, row_lo, jnp.where(row == 5, row_hi, 0.0))))))
        maps_ref[...] = maps.astype(I32)

    tm = mt_ref.shape[-1]
    bucket_row = mt_ref[0, 0:1, :]
    rank_row = mt_ref[0, 1:2, :]
    rb = lax.broadcasted_iota(I32, (nb, tm), 0).astype(F32)
    pos = jnp.sum(jnp.where(rb == bucket_row, start_ref[...], 0.0), axis=0, keepdims=True) + rank_row
    pos_ref[0] = pos.astype(I32)


def _plan(counts, meta_t, te, n_items_max):
    n_tt, _, tm = meta_t.shape
    nwp = ((n_items_max + LANES - 1) // LANES) * LANES
    body = functools.partial(_plan_body, te=te, nwp=nwp)
    return pl.pallas_call(
        body,
        grid=(n_tt,),
        in_specs=[pl.BlockSpec((1, N_BUCKETS), lambda i: (0, 0)),
                  pl.BlockSpec((1, META_ROWS, tm), lambda i: (i, 0, 0))],
        out_specs=[pl.BlockSpec((1, 1, tm), lambda i: (i, 0, 0)),
                   pl.BlockSpec((SUBLANES, nwp), lambda i: (0, 0))],
        out_shape=[jax.ShapeDtypeStruct((n_tt, 1, tm), I32),
                   jax.ShapeDtypeStruct((SUBLANES, nwp), I32)],
        scratch_shapes=[pltpu.VMEM((N_BUCKETS, 1), F32)],
        compiler_params=_params("arbitrary"),
        name="moe_plan",
    )(counts, meta_t)


def _row_copy(src_ref, src_row, dst_ref, dst_row, sem):
    return pltpu.make_async_copy(src_ref.at[pl.ds(src_row, 1)], dst_ref.at[pl.ds(dst_row, 1)], sem)


def _dispatch_body(pos_ref, x_ref, xs_ref, sem):
    tm = x_ref.shape[0]

    def issue(grp, c):
        r0 = pl.multiple_of(grp * ROW_DMA_UNROLL, ROW_DMA_UNROLL)
        for u in range(ROW_DMA_UNROLL):
            _row_copy(x_ref, r0 + u, xs_ref, pos_ref[0, 0, r0 + u], sem).start(priority=u % 2)
        return c

    lax.fori_loop(0, tm // ROW_DMA_UNROLL, issue, 0)
    pltpu.make_async_copy(x_ref, xs_ref.at[pl.ds(0, tm)], sem).wait()


def _dispatch(xrow, pos3, tm):
    t, w = xrow.shape
    t_pad = t
    return pl.pallas_call(
        _dispatch_body,
        grid=(t // tm,),
        in_specs=[pl.BlockSpec((1, 1, tm), lambda i: (i, 0, 0), memory_space=pltpu.SMEM),
                  pl.BlockSpec((tm, w), lambda i: (i, 0))],
        out_specs=pl.BlockSpec(memory_space=pl.ANY),
        out_shape=jax.ShapeDtypeStruct((t_pad, w), F32),
        scratch_shapes=[pltpu.SemaphoreType.DMA(())],
        compiler_params=_params("arbitrary"),
        name="moe_dispatch",
    )(pos3, xrow)


def _expert_body(tile_ref, elo_ref, ehi_ref, valid_ref, rlo_ref, rhi_ref, x_ref,
                 w1a_ref, w3a_ref, w2a_ref, w1b_ref, w3b_ref, w2b_ref, o_ref,
                 wa_up, wa_dn, wb_up, wb_dn):
    del tile_ref
    te, d = o_ref.shape
    j = pl.program_id(0)

    @pl.when(valid_ref[j] == 1)
    def _():
        rows = lax.broadcasted_iota(I32, (te, 1), 0)
        live = (rows >= rlo_ref[j]) & (rows < rhi_ref[j])
        x = x_ref[:, :d].astype(BF16)
        prev = jnp.maximum(j - 1, 0)

        @pl.when((j == 0) | (elo_ref[j] != elo_ref[prev]))
        def _():
            wa_up[0] = w1a_ref[...].astype(BF16)
            wa_up[1] = w3a_ref[...].astype(BF16)
            wa_dn[...] = w2a_ref[...].astype(BF16)

        @pl.when((j == 0) | (ehi_ref[j] != ehi_ref[prev]))
        def _():
            wb_up[0] = w1b_ref[...].astype(BF16)
            wb_up[1] = w3b_ref[...].astype(BF16)
            wb_dn[...] = w2b_ref[...].astype(BF16)

        def ffn(up, dn, wt):
            a = jnp.dot(x, up[0], preferred_element_type=F32)
            b = jnp.dot(x, up[1], preferred_element_type=F32)
            mid = (a * jax.nn.sigmoid(a)) * b
            y = jnp.dot(mid.astype(BF16), dn[...], preferred_element_type=F32)
            return wt * y

        res = (ffn(wa_up, wa_dn, jnp.where(live, x_ref[:, d + 2:d + 3], 0.0))
               + ffn(wb_up, wb_dn, jnp.where(live, x_ref[:, d + 3:d + 4], 0.0)))

        @pl.when(rlo_ref[j] == 0)
        def _():
            o_ref[...] = res

        @pl.when(rlo_ref[j] != 0)
        def _():
            o_ref[...] += res


def _experts(xs, w1, w3, w2, layer, maps, n_items, te):
    t, w = xs.shape
    _, n_e, d, f = w1.shape
    x_spec = pl.BlockSpec((te, w), lambda j, tl, lo, hi, v, a, b: (tl[j], 0))
    up_lo = pl.BlockSpec((None, None, d, f), lambda j, tl, lo, hi, v, a, b: (layer, lo[j], 0, 0))
    dn_lo = pl.BlockSpec((None, None, f, d), lambda j, tl, lo, hi, v, a, b: (layer, lo[j], 0, 0))
    up_hi = pl.BlockSpec((None, None, d, f), lambda j, tl, lo, hi, v, a, b: (layer, hi[j], 0, 0))
    dn_hi = pl.BlockSpec((None, None, f, d), lambda j, tl, lo, hi, v, a, b: (layer, hi[j], 0, 0))
    return pl.pallas_call(
        _expert_body,
        grid_spec=pltpu.PrefetchScalarGridSpec(
            num_scalar_prefetch=6,
            grid=(n_items,),
            in_specs=[x_spec, up_lo, up_lo, dn_lo, up_hi, up_hi, dn_hi],
            out_specs=pl.BlockSpec((te, d), lambda j, tl, lo, hi, v, a, b: (tl[j], 0)),
            scratch_shapes=[pltpu.VMEM((2, d, f), BF16), pltpu.VMEM((f, d), BF16),
                            pltpu.VMEM((2, d, f), BF16), pltpu.VMEM((f, d), BF16)]),
        out_shape=jax.ShapeDtypeStruct((t, d), F32),
        compiler_params=_params("arbitrary"),
        name="moe_experts",
    )(*[maps[i, :n_items] for i in range(6)], xs, w1, w3, w2, w1, w3, w2)


def _combine_body(pos_ref, ys_ref, h_ref, gate_ref, o_ref, ybuf, sem):
    tm = h_ref.shape[0]

    def issue(grp, c):
        r0 = pl.multiple_of(grp * ROW_DMA_UNROLL, ROW_DMA_UNROLL)
        for u in range(ROW_DMA_UNROLL):
            _row_copy(ys_ref, pos_ref[0, 0, r0 + u], ybuf, r0 + u, sem).start(priority=u % 2)
        return c

    lax.fori_loop(0, tm // ROW_DMA_UNROLL, issue, 0)
    pltpu.make_async_copy(ys_ref.at[pl.ds(0, tm)], ybuf, sem).wait()
    o_ref[...] = h_ref[...] + gate_ref[0] * ybuf[...]


def _combine(ys, pos3, h2, gate, nt_per_batch, tm):
    t, d = h2.shape
    return pl.pallas_call(
        _combine_body,
        grid=(t // tm,),
        in_specs=[pl.BlockSpec((1, 1, tm), lambda i: (i, 0, 0), memory_space=pltpu.SMEM),
                  pl.BlockSpec(memory_space=pl.ANY),
                  pl.BlockSpec((tm, d), lambda i: (i, 0)),
                  pl.BlockSpec((1, 1, d), lambda i: (i // nt_per_batch, 0, 0))],
        out_specs=pl.BlockSpec((tm, d), lambda i: (i, 0)),
        out_shape=jax.ShapeDtypeStruct((t, d), F32),
        scratch_shapes=[pltpu.VMEM((tm, d), F32), pltpu.SemaphoreType.DMA(())],
        compiler_params=_params("arbitrary"),
        name="moe_combine",
    )(pos3, ys, h2, gate)


def _moe(h2, g, shift, scale, gate, wg, bg, we, be, w1, w3, w2, layer, bsz, seq, tm, te):
    t, d = h2.shape
    nt_per_batch = seq // tm
    ne, ng = N_EXPERTS, N_EXPERT_GROUPS
    w_r = jnp.zeros((d, LANES), F32).at[:, :ne].set(we).at[:, ne:ne + ng].set(wg)
    b_r = jnp.zeros((1, LANES), F32).at[0, :ne].set(be).at[0, ne:ne + ng].set(bg)
    tri = jnp.asarray(np.tril(np.ones((tm, tm), np.float32)), BF16)
    xrow, meta_t, counts = _router(h2, g, shift, scale, w_r, b_r, tri, nt_per_batch, tm)
    n_items_max = t // te + N_PAIR_BUCKETS
    pos3, maps = _plan(counts, meta_t, te, n_items_max)
    xs = _dispatch(xrow, pos3, tm)
    ys = _experts(xs, w1, w3, w2, layer, maps, n_items_max, te)
    return _combine(ys, pos3, h2, gate, nt_per_batch, tm)


def _log_sigmoid(x):
    return jnp.minimum(x, 0.0) - jnp.log1p(jnp.exp(-jnp.abs(x)))


def _aug_tables(n_heads):
    assert n_heads * AUG_LANES_PER_HEAD <= LANES
    width = LANES
    pk = np.zeros((F_SPLIT * LANES, width), np.float32)
    pq = np.zeros((F_SPLIT * LANES, width), np.float32)
    ck = np.zeros((1, width), np.float32)
    cq = np.zeros((1, width), np.float32)
    for h in range(n_heads):
        base = h * AUG_LANES_PER_HEAD
        for j in range(F_SPLIT):
            pk[j * LANES + h, base + j] = -1.0
            pq[j * LANES + h, base + F_SPLIT + j] = 1.0
            ck[0, base + F_SPLIT + j] = 1.0
            cq[0, base + j] = 1.0
    return jnp.asarray(pk, BF16), jnp.asarray(pq, BF16), jnp.asarray(ck), jnp.asarray(cq)


def _kv_body(h_ref, g_ref, sh_ref, sc_ref, wk_ref, wvt_ref, wfh_ref, wfl_ref, fb_ref, kng_ref,
             pk_ref, pq_ref, ck_ref, cq_ref, k_ref, vt_ref, ka_ref, qa_ref, carry_ref):
    tm = h_ref.shape[0]

    @pl.when(pl.program_id(1) == 0)
    def _():
        carry_ref[...] = jnp.zeros_like(carry_ref)

    hn = _rms_mod(h_ref[...], g_ref[...], sh_ref[0], sc_ref[0])
    hb = hn.astype(BF16)
    k = jnp.dot(hb, wk_ref[...], preferred_element_type=F32)
    k_ref[...] = _head_rms(k, kng_ref[...]).astype(BF16)
    nt_dims = (((1,), (1,)), ((), ()))
    vt_ref[...] = lax.dot_general(wvt_ref[...], hb, nt_dims, preferred_element_type=F32).astype(BF16)
    fz = _dot_3pass(hn, wfh_ref, wfl_ref) + fb_ref[...]
    c = _log_sigmoid(fz)
    row = lax.broadcasted_iota(I32, c.shape, 0)
    shift = 1
    while shift < tm:
        c = c + jnp.where(row >= shift, pltpu.roll(c, shift, 0), 0.0)
        shift *= 2
    f = c + carry_ref[...]
    carry_ref[...] = f[tm - 1:tm, :]

    f2 = f * LOG2E
    hi = f2.astype(BF16)
    r1 = f2 - hi.astype(F32)
    mid = r1.astype(BF16)
    lo = (r1 - mid.astype(F32)).astype(BF16)
    pieces = jnp.concatenate([hi, mid, lo], axis=-1)
    ka_ref[...] = (jnp.dot(pieces, pk_ref[...], preferred_element_type=F32) + ck_ref[...]).astype(BF16)
    qa_ref[...] = (jnp.dot(pieces, pq_ref[...], preferred_element_type=F32) + cq_ref[...]).astype(BF16)


def _shared_kv(h2, g, shift, scale, wk, wvt, wf, fb, kng, bsz, seq, tm):
    t, d = h2.shape
    nt = seq // tm
    aw = LANES
    pk, pq, ck, cq = _aug_tables(d // HEAD_DIM)
    wf_hi, wf_lo = _split_bf16(wf)
    row = lambda b, i: (b * nt + i, 0)
    const = lambda b, i: (0, 0)
    return pl.pallas_call(
        _kv_body,
        grid=(bsz, nt),
        in_specs=[pl.BlockSpec((tm, d), row),
                  pl.BlockSpec((1, d), const),
                  pl.BlockSpec((1, 1, d), lambda b, i: (b, 0, 0)),
                  pl.BlockSpec((1, 1, d), lambda b, i: (b, 0, 0)),
                  pl.BlockSpec((d, d), const),
                  pl.BlockSpec((d, d), const),
                  pl.BlockSpec((d, LANES), const),
                  pl.BlockSpec((d, LANES), const),
                  pl.BlockSpec((1, LANES), const),
                  pl.BlockSpec((1, d), const),
                  pl.BlockSpec((F_SPLIT * LANES, aw), const),
                  pl.BlockSpec((F_SPLIT * LANES, aw), const),
                  pl.BlockSpec((1, aw), const),
                  pl.BlockSpec((1, aw), const)],
        out_specs=[pl.BlockSpec((tm, d), row),
                   pl.BlockSpec((d, tm), lambda b, i: (b, i)),
                   pl.BlockSpec((tm, aw), row),
                   pl.BlockSpec((tm, aw), row)],
        out_shape=[jax.ShapeDtypeStruct((t, d), BF16), jax.ShapeDtypeStruct((bsz * d, seq), BF16),
                   jax.ShapeDtypeStruct((t, aw), BF16), jax.ShapeDtypeStruct((t, aw), BF16)],
        scratch_shapes=[pltpu.VMEM((1, LANES), F32)],
        compiler_params=_params("parallel", "arbitrary"),
        name="shared_kv",
    )(h2, g, shift, scale, wk, wvt, wf_hi, wf_lo, fb, kng, pk, pq, ck, cq)


def _qg_body(h_ref, g_ref, sh_ref, sc_ref, wq_ref, wg_ref, qng_ref, q_ref, og_ref):
    hn = _rms_mod(h_ref[...], g_ref[...], sh_ref[0], sc_ref[0])
    hb = hn.astype(BF16)
    q = jnp.dot(hb, wq_ref[...], preferred_element_type=F32)
    q_ref[...] = (_head_rms(q, qng_ref[...]) * (HEAD_DIM ** -0.5 * LOG2E)).astype(BF16)
    og_ref[...] = jnp.dot(hb, wg_ref[...], preferred_element_type=F32).astype(BF16)


def _fox_qg(h2, g, shift, scale, wq, wg, qng, bsz, seq, tm):
    t, d = h2.shape
    nt = seq // tm
    row = lambda b, i: (b * nt + i, 0)
    const = lambda b, i: (0, 0)
    return pl.pallas_call(
        _qg_body,
        grid=(bsz, nt),
        in_specs=[pl.BlockSpec((tm, d), row),
                  pl.BlockSpec((1, d), const),
                  pl.BlockSpec((1, 1, d), lambda b, i: (b, 0, 0)),
                  pl.BlockSpec((1, 1, d), lambda b, i: (b, 0, 0)),
                  pl.BlockSpec((d, d), const),
                  pl.BlockSpec((d, d), const),
                  pl.BlockSpec((1, d), const)],
        out_specs=[pl.BlockSpec((tm, d), row), pl.BlockSpec((tm, d), row)],
        out_shape=[jax.ShapeDtypeStruct((t, d), BF16), jax.ShapeDtypeStruct((t, d), BF16)],
        compiler_params=_params("parallel", "parallel"),
        name="fox_qg",
    )(h2, g, shift, scale, wq, wg, qng)


def _attn_body(q_ref, qa_ref, k_ref, ka_ref, vt_ref, o_ref, m_ref, acc_ref, s0_ref, s1_ref, *, tq, hp):
    grp = pl.program_id(1)
    qi = pl.program_id(2)
    n_slabs = hp // 2
    lane_q = lax.broadcasted_iota(I32, (tq, LANES), 1)
    lo_q = lane_q < HEAD_DIM
    lo_v = lax.broadcasted_iota(I32, (LANES, tq), 0) < HEAD_DIM
    nt_dims = (((1,), (1,)), ((), ()))
    causal = lax.broadcasted_iota(I32, (tq, tq), 0) <= lax.broadcasted_iota(I32, (tq, tq), 1)

    qa = qa_ref[...]
    qcats = []
    for h in range(hp):
        sl, hh = h // 2, h % 2
        q2 = q_ref[:, sl * LANES:(sl + 1) * LANES]
        own = lo_q if hh == 0 else jnp.logical_not(lo_q)
        a0 = (grp * hp + h) * AUG_LANES_PER_HEAD
        own_a = (lane_q >= a0) & (lane_q < a0 + AUG_LANES_PER_HEAD)
        qcats.append(jnp.concatenate([jnp.where(own, q2, jnp.zeros_like(q2)),
                                      jnp.where(own_a, qa, jnp.zeros_like(qa))], axis=-1))

    m_ref[...] = jnp.full(m_ref.shape, NEG, F32)
    ones_r = jnp.ones((DEN_ROWS, tq), BF16)
    zeros_r = jnp.zeros((DEN_ROWS, tq), BF16)
    acc_row = lax.broadcasted_iota(I32, (LANES + 2 * DEN_ROWS, tq), 0)
    first_head_rows = (acc_row < HEAD_DIM) | ((acc_row >= LANES) & (acc_row < LANES + DEN_ROWS))
    acc_ref[...] = jnp.zeros(acc_ref.shape, F32)

    def scores(kb, dst_ref):
        s0 = pl.multiple_of(kb * tq, tq)
        for h in range(hp):
            sl = h // 2
            kcat = jnp.concatenate([k_ref[pl.ds(s0, tq), sl * LANES:(sl + 1) * LANES],
                                    ka_ref[pl.ds(s0, tq), :]], axis=-1)
            dst_ref[h] = lax.dot_general(kcat, qcats[h], nt_dims, preferred_element_type=F32)

    def consume(kb, src_ref, masked):
        s0 = pl.multiple_of(kb * tq, tq)
        for sl in range(n_slabs):
            vt = vt_ref[sl * LANES:(sl + 1) * LANES, pl.ds(s0, tq)]
            vts = (jnp.concatenate([jnp.where(lo_v, vt, jnp.zeros_like(vt)), ones_r, zeros_r], axis=0),
                   jnp.concatenate([jnp.where(lo_v, jnp.zeros_like(vt), vt), zeros_r, ones_r], axis=0))
            pv, alphas = None, []
            for hh in range(2):
                h = sl * 2 + hh
                st = src_ref[h]
                if masked:
                    st = jnp.where(causal, st, NEG)
                m_old = m_ref[h]
                m_new = jnp.maximum(m_old, jnp.max(st, axis=0, keepdims=True))
                alpha = jnp.exp2(m_old - m_new)
                p = jnp.exp2(st - m_new)
                m_ref[h] = m_new
                part = jnp.dot(vts[hh], p.astype(BF16), preferred_element_type=F32)
                pv = part if pv is None else pv + part
                alphas.append(alpha)
            acc_ref[sl] = jnp.where(first_head_rows, alphas[0], alphas[1]) * acc_ref[sl] + pv

    scores(0, s0_ref)

    def pair(j, c):
        scores(2 * j + 1, s1_ref)
        consume(2 * j, s0_ref, False)
        scores(2 * j + 2, s0_ref)
        consume(2 * j + 1, s1_ref, False)
        return c

    lax.fori_loop(0, qi // 2, pair, 0)

    @pl.when(qi % 2 == 0)
    def _():
        consume(qi, s0_ref, True)

    @pl.when(qi % 2 == 1)
    def _():
        scores(qi, s1_ref)
        consume(qi - 1, s0_ref, False)
        consume(qi, s1_ref, True)

    for sl in range(n_slabs):
        acc = acc_ref[sl]
        l2 = jnp.where(lo_v, acc[LANES:LANES + 1, :], acc[LANES + DEN_ROWS:LANES + DEN_ROWS + 1, :])
        o_ref[:, sl * LANES:(sl + 1) * LANES] = (acc[:LANES, :] / l2).T.astype(BF16)


def _fox_attention(q, qaug, k, kaug, vt, bsz, seq, tq, hp):
    t, d = q.shape
    w = hp * HEAD_DIM
    n_grp = d // w
    nq = seq // tq
    body = functools.partial(_attn_body, tq=tq, hp=hp)
    qrow = lambda b, j, i: (b * nq + i, j)
    krow = lambda b, j, i: (b, j)
    return pl.pallas_call(
        body,
        grid=(bsz, n_grp, nq),
        in_specs=[pl.BlockSpec((tq, w), qrow),
                  pl.BlockSpec((tq, LANES), lambda b, j, i: (b * nq + i, 0)),
                  pl.BlockSpec((seq, w), krow),
                  pl.BlockSpec((seq, LANES), lambda b, j, i: (b, 0)),
                  pl.BlockSpec((w, seq), lambda b, j, i: (b * n_grp + j, 0))],
        out_specs=pl.BlockSpec((tq, w), qrow),
        out_shape=jax.ShapeDtypeStruct((t, d), BF16),
        scratch_shapes=[pltpu.VMEM((hp, 1, tq), F32),
                        pltpu.VMEM((hp // 2, LANES + 2 * DEN_ROWS, tq), F32),
                        pltpu.VMEM((hp, tq, tq), F32), pltpu.VMEM((hp, tq, tq), F32)],
        compiler_params=_params("parallel", "parallel", "arbitrary"),
        name="fox_attention",
    )(q, qaug, k, kaug, vt)


def _fox_out_body(o_ref, og_ref, h_ref, w_ref, gate_ref, out_ref):
    z = o_ref[...].astype(F32) * jax.nn.sigmoid(og_ref[...].astype(F32))
    out_ref[...] = h_ref[...] + gate_ref[0] * jnp.dot(z.astype(BF16), w_ref[...], preferred_element_type=F32)


def _fox_out(o, og, h2, w_o, gate, bsz, seq, tm):
    t, d = h2.shape
    nt = seq // tm
    row = lambda b, i: (b * nt + i, 0)
    return pl.pallas_call(
        _fox_out_body,
        grid=(bsz, nt),
        in_specs=[pl.BlockSpec((tm, d), row), pl.BlockSpec((tm, d), row), pl.BlockSpec((tm, d), row),
                  pl.BlockSpec((d, d), lambda b, i: (0, 0)),
                  pl.BlockSpec((1, 1, d), lambda b, i: (b, 0, 0))],
        out_specs=pl.BlockSpec((tm, d), row),
        out_shape=jax.ShapeDtypeStruct((t, d), F32),
        compiler_params=_params("parallel", "parallel"),
        name="fox_out",
    )(o, og, h2, w_o, gate)


def _tiles(seq):
    tm = min(512, seq)
    te = min(256, seq)
    tq = min(512, seq)
    tc = min(32, seq)
    return tm, te, tq, tc


def kernel(x, c, ln_g, ada_w, ada_b, s5_w_in, s5_lambda_re, s5_lambda_im, s5_log_dt, s5_b_re, s5_b_im,
           s5_c_re, s5_c_im, s5_d, s5_w_out, kv_g, kv_ada_w, kv_ada_b, kv_w, kv_fb, k_norm_g,
           fox_w_qg, fox_q_norm_g, fox_w_o, moe_wg, moe_bg, moe_we, moe_be, moe_w1, moe_w3, moe_w2):
    bsz, seq, d = x.shape
    depth = ln_g.shape[0]
    n_a = s5_w_in.shape[0]
    n_heads = d // HEAD_DIM
    tm, te, tq, tc = _tiles(seq)

    mods = _adaln(c, ada_w.reshape(depth * 2, d, 3 * d), ada_b.reshape(depth * 2, 1, 3 * d))
    mods = mods.reshape(depth, 2, bsz, 3, 1, d)
    kv_mods = _adaln(c, kv_ada_w[None], kv_ada_b[None, None]).reshape(bsz, 2, 1, d)

    h = x.reshape(bsz * seq, d)
    k = kaug = qaug = vt = None
    for l in range(depth):
        shift, scale, gate = mods[l, 0, :, 0], mods[l, 0, :, 1], mods[l, 0, :, 2]
        g = ln_g[l, 0][None]
        if l < n_a:
            u2 = _s5_in(h, g, shift, scale, s5_w_in[l].astype(BF16), bsz, seq, tm)
            bblk, cblk, a_re, a_im = _s5_tables(s5_lambda_re[l], s5_lambda_im[l], s5_log_dt[l],
                                                s5_b_re[l], s5_b_im[l], s5_c_re[l], s5_c_im[l])
            y2 = _s5_scan(u2, bblk, cblk, a_re, a_im, bsz, seq, tc)
            h = _s5_out(y2, u2, h, s5_d[l][None], s5_w_out[l].astype(BF16), gate, bsz, seq, tm)
        else:
            j = l - n_a
            qng = jnp.tile(fox_q_norm_g[j], n_heads)[None]
            q, og = _fox_qg(h, g, shift, scale, fox_w_qg[j][:, :d].astype(BF16),
                            fox_w_qg[j][:, d:].astype(BF16), qng, bsz, seq, tm)
            o = _fox_attention(q, qaug, k, kaug, vt, bsz, seq, tq, ATTN_HEADS_PER_STEP)
            h = _fox_out(o, og, h, fox_w_o[j].astype(BF16), gate, bsz, seq, tm)

        shift, scale, gate = mods[l, 1, :, 0], mods[l, 1, :, 1], mods[l, 1, :, 2]
        h = _moe(h, ln_g[l, 1][None], shift, scale, gate, moe_wg[l], moe_bg[l], moe_we[l], moe_be[l],
                 moe_w1, moe_w3, moe_w2, l, bsz, seq, tm, te)

        if l == n_a - 1:
            wf = jnp.zeros((d, LANES), F32).at[:, :n_heads].set(kv_w[:, 2 * d:])
            fb = jnp.zeros((1, LANES), F32).at[0, :n_heads].set(kv_fb)
            kng = jnp.tile(k_norm_g, n_heads)[None]
            k, vt, kaug, qaug = _shared_kv(h, kv_g[None], kv_mods[:, 0], kv_mods[:, 1],
                                           kv_w[:, :d].astype(BF16), kv_w[:, d:2 * d].T.astype(BF16),
                                           wf, fb, kng, bsz, seq, tm)
    return h.reshape(bsz, seq, d)
```

```python
import functools
import math

import numpy as np
import jax
import jax.numpy as jnp
from jax import lax
from jax.experimental import pallas as pl
from jax.experimental.pallas import tpu as pltpu

F32 = jnp.float32
BF16 = jnp.bfloat16
I32 = jnp.int32

EPS = 1e-6
NEG = -1e30
LOG2E = math.log2(math.e)
LANES = 128
SUBLANES = 8
VMEM_LIMIT_BYTES = 56 * 1024 * 1024

S5_GROUPS_PER_BLOCK = 16
N_EXPERT_GROUPS = 4
EXPERTS_PER_GROUP = 8
N_EXPERTS = N_EXPERT_GROUPS * EXPERTS_PER_GROUP
N_BUCKETS = N_EXPERT_GROUPS * EXPERTS_PER_GROUP * EXPERTS_PER_GROUP
N_PAIR_BUCKETS = N_EXPERT_GROUPS * (EXPERTS_PER_GROUP * (EXPERTS_PER_GROUP - 1) // 2)
META_LANES = LANES
META_ROWS = SUBLANES
ROW_DMA_UNROLL = 8
HEAD_DIM = 64
ATTN_HEADS_PER_STEP = 4
DEN_ROWS = 16
F_SPLIT = 3
AUG_LANES_PER_HEAD = 2 * F_SPLIT


def _params(*sem):
    return pltpu.CompilerParams(dimension_semantics=sem, vmem_limit_bytes=VMEM_LIMIT_BYTES)


def _rms_mod(x, g, shift, scale):
    ms = jnp.mean(x * x, axis=-1, keepdims=True)
    y = x * lax.rsqrt(ms + EPS) * g
    return y * (1.0 + scale) + shift


def _split_bf16(w):
    hi = w.astype(BF16)
    return hi, (w - hi.astype(F32)).astype(BF16)


def _dot_3pass(x, w_hi_ref, w_lo_ref):
    x_hi, x_lo = _split_bf16(x)
    w_hi = w_hi_ref[...]
    return (jnp.dot(x_hi, w_hi, preferred_element_type=F32)
            + jnp.dot(x_lo, w_hi, preferred_element_type=F32)
            + jnp.dot(x_hi, w_lo_ref[...], preferred_element_type=F32))


def _head_rms(x, g):
    tm, d = x.shape
    lane = lax.broadcasted_iota(I32, (tm, LANES), 1)
    lo = lane < HEAD_DIM
    outs = []
    for j in range(d // LANES):
        s = x[:, j * LANES:(j + 1) * LANES]
        sq = s * s
        s_lo = jnp.sum(jnp.where(lo, sq, 0.0), axis=-1, keepdims=True)
        s_hi = jnp.sum(jnp.where(lo, 0.0, sq), axis=-1, keepdims=True)
        r = jnp.where(lo, lax.rsqrt(s_lo / HEAD_DIM + EPS), lax.rsqrt(s_hi / HEAD_DIM + EPS))
        outs.append(s * r)
    return jnp.concatenate(outs, axis=-1) * g


def _adaln_body(c_ref, w_ref, b_ref, o_ref):
    c = c_ref[...]
    s = c * jax.nn.sigmoid(c)
    o_ref[0] = jnp.dot(s, w_ref[0], preferred_element_type=F32) + b_ref[0]


def _adaln(c, w, b):
    n_sets, d, n = w.shape
    bsz = c.shape[0]
    tn = 512 if n % 512 == 0 else n
    return pl.pallas_call(
        _adaln_body,
        grid=(n_sets, n // tn),
        in_specs=[pl.BlockSpec((bsz, d), lambda s, j: (0, 0)),
                  pl.BlockSpec((1, d, tn), lambda s, j: (s, 0, j)),
                  pl.BlockSpec((1, 1, tn), lambda s, j: (s, 0, j))],
        out_specs=pl.BlockSpec((1, bsz, tn), lambda s, j: (s, 0, j)),
        out_shape=jax.ShapeDtypeStruct((n_sets, bsz, n), F32),
        compiler_params=_params("parallel", "parallel"),
        name="adaln",
    )(c, w, b)


def _s5_in_body(x_ref, g_ref, sh_ref, sc_ref, w_ref, u_ref):
    hn = _rms_mod(x_ref[...], g_ref[...], sh_ref[0], sc_ref[0])
    u_ref[...] = jnp.dot(hn.astype(BF16), w_ref[...], preferred_element_type=F32)


def _s5_in(x2, g, shift, scale, w_in, bsz, seq, tm):
    d = x2.shape[1]
    nt = seq // tm
    row = lambda b, i: (b * nt + i, 0)
    return pl.pallas_call(
        _s5_in_body,
        grid=(bsz, nt),
        in_specs=[pl.BlockSpec((tm, d), row),
                  pl.BlockSpec((1, d), lambda b, i: (0, 0)),
                  pl.BlockSpec((1, 1, d), lambda b, i: (b, 0, 0)),
                  pl.BlockSpec((1, 1, d), lambda b, i: (b, 0, 0)),
                  pl.BlockSpec((d, d), lambda b, i: (0, 0))],
        out_specs=pl.BlockSpec((tm, d), row),
        out_shape=jax.ShapeDtypeStruct((bsz * seq, d), F32),
        compiler_params=_params("parallel", "parallel"),
        name="s5_in",
    )(x2, g, shift, scale, w_in)


def _s5_scan_body(u_hbm, bb_ref, cb_ref, are_ref, aim_ref, y_hbm,
                  ubuf, ybuf, bu_ref, st_ref, sem_in, sem_out, *, tc, nblk, sw, seq, n_chunks):
    bsz = SUBLANES
    cw = S5_GROUPS_PER_BLOCK * 16
    i = pl.program_id(0)
    slot = lax.rem(i, 2)

    def in_copy(chunk, sl, b):
        return pltpu.make_async_copy(u_hbm.at[pl.ds(b * seq + chunk * tc, tc)],
                                     ubuf.at[sl, :, b, :], sem_in.at[sl])

    def out_copy(chunk, sl, b):
        return pltpu.make_async_copy(ybuf.at[sl, :, b, :],
                                     y_hbm.at[pl.ds(b * seq + chunk * tc, tc)], sem_out.at[sl])

    @pl.when(i == 0)
    def _():
        st_ref[...] = jnp.zeros_like(st_ref)
        for b in range(bsz):
            in_copy(0, 0, b).start()

    @pl.when(i + 1 < n_chunks)
    def _():
        for b in range(bsz):
            in_copy(i + 1, 1 - slot, b).start()

    for b in range(bsz):
        in_copy(i, slot, b).wait()

    @pl.when(i >= 2)
    def _():
        for b in range(bsz):
            out_copy(i - 2, slot, b).wait()

    d = ubuf.shape[-1]
    u2 = ubuf[slot].reshape(tc * bsz, d).astype(BF16)
    for k in range(nblk):
        bu_ref[:, k * 2 * sw:(k + 1) * 2 * sw] = jnp.dot(
            u2[:, k * cw:(k + 1) * cw], bb_ref[k], preferred_element_type=F32)

    for k in range(nblk):
        re0, im0 = k * 2 * sw, k * 2 * sw + sw
        a_re = jnp.broadcast_to(are_ref[k], (SUBLANES, sw))
        a_im = jnp.broadcast_to(aim_ref[k], (SUBLANES, sw))

        def step(t, carry, re0=re0, im0=im0, a_re=a_re, a_im=a_im):
            s_re, s_im = carry
            r0 = pl.multiple_of(t * SUBLANES, SUBLANES)
            n_re = a_re * s_re - a_im * s_im + bu_ref[pl.ds(r0, SUBLANES), re0:re0 + sw]
            n_im = a_re * s_im + a_im * s_re + bu_ref[pl.ds(r0, SUBLANES), im0:im0 + sw]
            bu_ref[pl.ds(r0, SUBLANES), re0:re0 + sw] = n_re
            bu_ref[pl.ds(r0, SUBLANES), im0:im0 + sw] = n_im
            return n_re, n_im

        s_re, s_im = lax.fori_loop(
            0, tc, step, (st_ref[:, re0:re0 + sw], st_ref[:, im0:im0 + sw]), unroll=True)
        st_ref[:, re0:re0 + sw] = s_re
        st_ref[:, im0:im0 + sw] = s_im

    for k in range(nblk):
        s2 = bu_ref[:, k * 2 * sw:(k + 1) * 2 * sw].astype(BF16)
        yk = jnp.dot(s2, cb_ref[k], preferred_element_type=F32)
        ybuf[slot, :, :, k * cw:(k + 1) * cw] = yk.reshape(tc, bsz, cw)

    for b in range(bsz):
        out_copy(i, slot, b).start()

    @pl.when(i == n_chunks - 1)
    def _():
        if n_chunks >= 2:
            for b in range(bsz):
                out_copy(i - 1, 1 - slot, b).wait()
        for b in range(bsz):
            out_copy(i, slot, b).wait()


def _s5_scan(u2, bblk, cblk, a_re, a_im, bsz, seq, tc):
    assert bsz == SUBLANES, "the scan keeps the batch on the 8 sublanes of a vreg"
    rows, d = u2.shape
    nblk, cw, sw2 = bblk.shape
    sw = sw2 // 2
    n_chunks = seq // tc
    body = functools.partial(_s5_scan_body, tc=tc, nblk=nblk, sw=sw, seq=seq, n_chunks=n_chunks)
    return pl.pallas_call(
        body,
        grid=(n_chunks,),
        in_specs=[pl.BlockSpec(memory_space=pl.ANY),
                  pl.BlockSpec((nblk, cw, sw2), lambda i: (0, 0, 0)),
                  pl.BlockSpec((nblk, sw2, cw), lambda i: (0, 0, 0)),
                  pl.BlockSpec((nblk, 1, sw), lambda i: (0, 0, 0)),
                  pl.BlockSpec((nblk, 1, sw), lambda i: (0, 0, 0))],
        out_specs=pl.BlockSpec(memory_space=pl.ANY),
        out_shape=jax.ShapeDtypeStruct((rows, d), F32),
        scratch_shapes=[pltpu.VMEM((2, tc, bsz, d), F32), pltpu.VMEM((2, tc, bsz, d), F32),
                        pltpu.VMEM((tc * bsz, nblk * sw2), F32), pltpu.VMEM((bsz, nblk * sw2), F32),
                        pltpu.SemaphoreType.DMA((2,)), pltpu.SemaphoreType.DMA((2,))],
        compiler_params=_params("arbitrary"),
        name="s5_scan",
    )(u2, bblk, cblk, a_re, a_im)


def _s5_out_body(y_ref, u_ref, h_ref, d_ref, w_ref, gate_ref, o_ref):
    z = y_ref[...] + d_ref[...] * u_ref[...]
    act = jax.nn.gelu(z)
    vg = jnp.dot(act.astype(BF16), w_ref[...], preferred_element_type=F32)
    d = z.shape[-1]
    mix = vg[:, :d] * jax.nn.sigmoid(vg[:, d:])
    o_ref[...] = h_ref[...] + gate_ref[0] * mix


def _s5_out(y2, u2, h2, d_skip, w_out, gate, bsz, seq, tm):
    d = h2.shape[1]
    nt = seq // tm
    row = lambda b, i: (b * nt + i, 0)
    return pl.pallas_call(
        _s5_out_body,
        grid=(bsz, nt),
        in_specs=[pl.BlockSpec((tm, d), row),
                  pl.BlockSpec((tm, d), row),
                  pl.BlockSpec((tm, d), row),
                  pl.BlockSpec((1, d), lambda b, i: (0, 0)),
                  pl.BlockSpec((d, 2 * d), lambda b, i: (0, 0)),
                  pl.BlockSpec((1, 1, d), lambda b, i: (b, 0, 0))],
        out_specs=pl.BlockSpec((tm, d), row),
        out_shape=jax.ShapeDtypeStruct((bsz * seq, d), F32),
        compiler_params=_params("parallel", "parallel"),
        name="s5_out",
    )(y2, u2, h2, d_skip, w_out, gate)


def _s5_tables(lam_re, lam_im, log_dt, b_re, b_im, c_re, c_im):
    dt = jnp.exp(log_dt.astype(F32))[:, None]
    lr, li = lam_re.astype(F32), lam_im.astype(F32)
    mag = jnp.exp(lr * dt)
    a_re = mag * jnp.cos(li * dt)
    a_im = mag * jnp.sin(li * dt)
    den = lr * lr + li * li
    coef_re = ((a_re - 1.0) * lr + a_im * li) / den
    coef_im = (a_im * lr - (a_re - 1.0) * li) / den
    br_, bi_ = b_re.astype(F32), b_im.astype(F32)
    bbar_re = coef_re[..., None] * br_ - coef_im[..., None] * bi_
    bbar_im = coef_re[..., None] * bi_ + coef_im[..., None] * br_
    g, p, c = bbar_re.shape
    gb = S5_GROUPS_PER_BLOCK
    nblk = g // gb
    eye = jnp.eye(gb, dtype=F32)

    def in_blocks(m):
        return jnp.einsum('kgpc,gh->kgchp', m.reshape(nblk, gb, p, c), eye).reshape(nblk, gb * c, gb * p)

    def out_blocks(m):
        return jnp.einsum('kgcp,gh->kgphc', m.reshape(nblk, gb, c, p), eye).reshape(nblk, gb * p, gb * c)

    bblk = jnp.concatenate([in_blocks(bbar_re), in_blocks(bbar_im)], axis=-1).astype(BF16)
    cblk = jnp.concatenate([out_blocks(c_re.astype(F32)), -out_blocks(c_im.astype(F32))], axis=1).astype(BF16)
    return bblk, cblk, a_re.reshape(nblk, 1, gb * p), a_im.reshape(nblk, 1, gb * p)


def _router_body(h_ref, g_ref, sh_ref, sc_ref, wrh_ref, wrl_ref, br_ref, tri_ref,
                 x_ref, mt_ref, cnt_ref, carry_ref):
    tm, d = h_ref.shape
    ne, ng, epg = N_EXPERTS, N_EXPERT_GROUPS, EXPERTS_PER_GROUP

    @pl.when(pl.program_id(0) == 0)
    def _():
        carry_ref[...] = jnp.zeros_like(carry_ref)

    hn = _rms_mod(h_ref[...], g_ref[...], sh_ref[0], sc_ref[0])
    logits = _dot_3pass(hn, wrh_ref, wrl_ref) + br_ref[...]
    lane = lax.broadcasted_iota(I32, logits.shape, 1).astype(F32)
    big = jnp.float32(1e9)
    ninf = jnp.float32(-jnp.inf)

    gmask = (lane >= ne) & (lane < ne + ng)
    gmax = jnp.max(jnp.where(gmask, logits, ninf), axis=-1, keepdims=True)
    gsum = jnp.sum(jnp.where(gmask, jnp.exp(logits - gmax), 0.0), axis=-1, keepdims=True)
    p_g = 1.0 / gsum
    gidx = jnp.min(jnp.where(gmask & (logits == gmax), lane - ne, big), axis=-1, keepdims=True)

    emask = (lane < ne) & (jnp.floor(lane / epg) == gidx)
    v1 = jnp.max(jnp.where(emask, logits, ninf), axis=-1, keepdims=True)
    i1 = jnp.min(jnp.where(emask & (logits == v1), lane, big), axis=-1, keepdims=True)
    emask2 = emask & (lane != i1)
    v2 = jnp.max(jnp.where(emask2, logits, ninf), axis=-1, keepdims=True)
    i2 = jnp.min(jnp.where(emask2 & (logits == v2), lane, big), axis=-1, keepdims=True)
    e21 = jnp.exp(v2 - v1)
    w1 = p_g / (1.0 + e21)
    w2 = p_g * e21 / (1.0 + e21)

    first_lo = i1 < i2
    e_lo = jnp.where(first_lo, i1, i2)
    e_hi = jnp.where(first_lo, i2, i1)
    w_lo = jnp.where(first_lo, w1, w2)
    w_hi = jnp.where(first_lo, w2, w1)
    bucket = gidx * (epg * epg) + (e_lo - gidx * epg) * epg + (e_hi - gidx * epg)

    lane_b = lax.broadcasted_iota(I32, (tm, N_BUCKETS), 1).astype(F32)
    onehot = (lane_b == bucket).astype(F32)
    prefix = jnp.dot(tri_ref[...], onehot.astype(BF16), preferred_element_type=F32)
    carry = carry_ref[...]
    rank = jnp.sum(onehot * (prefix + carry), axis=-1, keepdims=True) - 1.0
    new_carry = carry + prefix[tm - 1:tm, :]
    carry_ref[...] = new_carry
    cnt_ref[...] = new_carry

    mlane = lax.broadcasted_iota(I32, (tm, META_LANES), 1)
    meta = jnp.where(mlane == 0, bucket,
           jnp.where(mlane == 1, rank,
           jnp.where(mlane == 2, w_lo,
           jnp.where(mlane == 3, w_hi, 0.0))))
    x_ref[:, :d] = hn
    x_ref[:, d:] = meta
    mt_ref[0] = meta.T[:META_ROWS, :]


def _router(h2, g, shift, scale, w_r, b_r, tri, nt_per_batch, tm):
    t, d = h2.shape
    wr_hi, wr_lo = _split_bf16(w_r)
    return pl.pallas_call(
        _router_body,
        grid=(t // tm,),
        in_specs=[pl.BlockSpec((tm, d), lambda i: (i, 0)),
                  pl.BlockSpec((1, d), lambda i: (0, 0)),
                  pl.BlockSpec((1, 1, d), lambda i: (i // nt_per_batch, 0, 0)),
                  pl.BlockSpec((1, 1, d), lambda i: (i // nt_per_batch, 0, 0)),
                  pl.BlockSpec((d, LANES), lambda i: (0, 0)),
                  pl.BlockSpec((d, LANES), lambda i: (0, 0)),
                  pl.BlockSpec((1, LANES), lambda i: (0, 0)),
                  pl.BlockSpec((tm, tm), lambda i: (0, 0))],
        out_specs=[pl.BlockSpec((tm, d + META_LANES), lambda i: (i, 0)),
                   pl.BlockSpec((1, META_ROWS, tm), lambda i: (i, 0, 0)),
                   pl.BlockSpec((1, N_BUCKETS), lambda i: (0, 0))],
        out_shape=[jax.ShapeDtypeStruct((t, d + META_LANES), F32),
                   jax.ShapeDtypeStruct((t // tm, META_ROWS, tm), F32),
                   jax.ShapeDtypeStruct((1, N_BUCKETS), F32)],
        scratch_shapes=[pltpu.VMEM((1, N_BUCKETS), F32)],
        compiler_params=_params("arbitrary"),
        name="moe_router",
    )(h2, g, shift, scale, wr_hi, wr_lo, b_r, tri)


def _plan_body(cnt_ref, mt_ref, pos_ref, maps_ref, start_ref, *, te, nwp):
    nb = N_BUCKETS
    epg = EXPERTS_PER_GROUP

    @pl.when(pl.program_id(0) == 0)
    def _():
        r = lax.broadcasted_iota(I32, (nb, nb), 0)
        c = lax.broadcasted_iota(I32, (nb, nb), 1)
        nt_dims = (((1,), (1,)), ((), ()))

        def column(mask, row_vals):
            row8 = jnp.broadcast_to(row_vals, (SUBLANES, nb)).astype(BF16)
            return lax.dot_general(mask.astype(BF16), row8, nt_dims, preferred_element_type=F32)[:, :1]

        cnt = cnt_ref[...]
        cnt_hi = jnp.floor(cnt / 256.0)
        cnt_lo = cnt - 256.0 * cnt_hi
        start = 256.0 * column(c < r, cnt_hi) + column(c < r, cnt_lo)
        count = 256.0 * column(c == r, cnt_hi) + column(c == r, cnt_lo)
        end = start + count
        start_ref[...] = start
        first_tile = jnp.floor(start / te)
        n_items = jnp.where(count > 0.0, jnp.floor((end - 1.0) / te) - first_tile + 1.0, 0.0)
        items8 = jnp.broadcast_to(n_items, (nb, LANES)).astype(BF16)
        item_end = jnp.dot((c <= r).astype(BF16), items8, preferred_element_type=F32)[:, :1]
        item_start = item_end - n_items
        n_total = item_end[nb - 1:nb, :]

        w = lax.broadcasted_iota(I32, (1, nwp), 1).astype(F32)
        wc = jnp.minimum(w, jnp.maximum(n_total - 1.0, 0.0))
        bucket = jnp.sum((item_end <= wc).astype(F32), axis=0, keepdims=True)
        sel = lax.broadcasted_iota(I32, (nb, nwp), 0).astype(F32) == bucket

        def pick(col):
            return jnp.sum(jnp.where(sel, col, 0.0), axis=0, keepdims=True)

        valid = (w < n_total).astype(F32)
        tile = pick(first_tile) + (wc - pick(item_start))
        row_lo = (jnp.maximum(pick(start), tile * te) - tile * te) * valid
        row_hi = (jnp.minimum(pick(end), (tile + 1.0) * te) - tile * te) * valid
        grp = jnp.floor(bucket / (epg * epg))
        within = bucket - grp * (epg * epg)
        lo = jnp.floor(within / epg)
        e_lo = grp * epg + lo
        e_hi = grp * epg + (within - lo * epg)
        row = lax.broadcasted_iota(I32, (SUBLANES, nwp), 0)
        maps = jnp.where(row == 0, tile, jnp.where(row == 1, e_lo, jnp.where(row == 2, e_hi,
               jnp.where(row == 3, valid, jnp.where(row == 4, row_lo, jnp.where(row == 5, row_hi, 0.0))))))
        maps_ref[...] = maps.astype(I32)

    tm = mt_ref.shape[-1]
    bucket_row = mt_ref[0, 0:1, :]
    rank_row = mt_ref[0, 1:2, :]
    rb = lax.broadcasted_iota(I32, (nb, tm), 0).astype(F32)
    pos = jnp.sum(jnp.where(rb == bucket_row, start_ref[...], 0.0), axis=0, keepdims=True) + rank_row
    pos_ref[0] = pos.astype(I32)


def _plan(counts, meta_t, te, n_items_max):
    n_tt, _, tm = meta_t.shape
    nwp = ((n_items_max + LANES - 1) // LANES) * LANES
    body = functools.partial(_plan_body, te=te, nwp=nwp)
    return pl.pallas_call(
        body,
        grid=(n_tt,),
        in_specs=[pl.BlockSpec((1, N_BUCKETS), lambda i: (0, 0)),
                  pl.BlockSpec((1, META_ROWS, tm), lambda i: (i, 0, 0))],
        out_specs=[pl.BlockSpec((1, 1, tm), lambda i: (i, 0, 0)),
                   pl.BlockSpec((SUBLANES, nwp), lambda i: (0, 0))],
        out_shape=[jax.ShapeDtypeStruct((n_tt, 1, tm), I32),
                   jax.ShapeDtypeStruct((SUBLANES, nwp), I32)],
        scratch_shapes=[pltpu.VMEM((N_BUCKETS, 1), F32)],
        compiler_params=_params("arbitrary"),
        name="moe_plan",
    )(counts, meta_t)


def _row_copy(src_ref, src_row, dst_ref, dst_row, sem):
    return pltpu.make_async_copy(src_ref.at[pl.ds(src_row, 1)], dst_ref.at[pl.ds(dst_row, 1)], sem)


def _dispatch_body(pos_ref, x_ref, xs_ref, sem):
    tm = x_ref.shape[0]

    def issue(grp, c):
        r0 = pl.multiple_of(grp * ROW_DMA_UNROLL, ROW_DMA_UNROLL)
        for u in range(ROW_DMA_UNROLL):
            _row_copy(x_ref, r0 + u, xs_ref, pos_ref[0, 0, r0 + u], sem).start(priority=u % 2)
        return c

    lax.fori_loop(0, tm // ROW_DMA_UNROLL, issue, 0, unroll=True)
    pltpu.make_async_copy(x_ref, xs_ref.at[pl.ds(0, tm)], sem).wait()


def _dispatch(xrow, pos3, tm):
    t, w = xrow.shape
    t_pad = t
    return pl.pallas_call(
        _dispatch_body,
        grid=(t // tm,),
        in_specs=[pl.BlockSpec((1, 1, tm), lambda i: (i, 0, 0), memory_space=pltpu.SMEM),
                  pl.BlockSpec((tm, w), lambda i: (i, 0))],
        out_specs=pl.BlockSpec(memory_space=pl.ANY),
        out_shape=jax.ShapeDtypeStruct((t_pad, w), F32),
        scratch_shapes=[pltpu.SemaphoreType.DMA(())],
        compiler_params=_params("arbitrary"),
        name="moe_dispatch",
    )(pos3, xrow)


def _expert_body(tile_ref, elo_ref, ehi_ref, valid_ref, rlo_ref, rhi_ref, x_ref,
                 w1_ref, w3_ref, w2_ref, o_ref, up_s, dn_s):
    del tile_ref
    te, d = o_ref.shape
    epg = EXPERTS_PER_GROUP
    j = pl.program_id(0)
    prev = jnp.maximum(j - 1, 0)

    @pl.when((j == 0) | (elo_ref[j] // epg != elo_ref[prev] // epg))
    def _():
        for e in range(epg):
            up_s[e, 0] = w1_ref[e].astype(BF16)
            up_s[e, 1] = w3_ref[e].astype(BF16)
            dn_s[e] = w2_ref[e].astype(BF16)

    @pl.when(valid_ref[j] == 1)
    def _():
        rows = lax.broadcasted_iota(I32, (te, 1), 0)
        live = (rows >= rlo_ref[j]) & (rows < rhi_ref[j])
        x = x_ref[:, :d].astype(BF16)

        def ffn(e, wt):
            a = jnp.dot(x, up_s[e, 0], preferred_element_type=F32)
            b = jnp.dot(x, up_s[e, 1], preferred_element_type=F32)
            mid = (a * jax.nn.sigmoid(a)) * b
            y = jnp.dot(mid.astype(BF16), dn_s[e], preferred_element_type=F32)
            return wt * y

        res = (ffn(elo_ref[j] % epg, jnp.where(live, x_ref[:, d + 2:d + 3], 0.0))
               + ffn(ehi_ref[j] % epg, jnp.where(live, x_ref[:, d + 3:d + 4], 0.0)))

        @pl.when(rlo_ref[j] == 0)
        def _():
            o_ref[...] = res

        @pl.when(rlo_ref[j] != 0)
        def _():
            o_ref[...] += res


def _experts(xs, w1, w3, w2, layer, maps, n_items, te):
    t, w = xs.shape
    _, n_e, d, f = w1.shape
    epg = EXPERTS_PER_GROUP
    x_spec = pl.BlockSpec((te, w), lambda j, tl, lo, hi, v, a, b: (tl[j], 0))
    up = pl.BlockSpec((None, epg, d, f), lambda j, tl, lo, hi, v, a, b: (layer, lo[j] // epg, 0, 0),
                      pipeline_mode=pl.Buffered(1))
    dn = pl.BlockSpec((None, epg, f, d), lambda j, tl, lo, hi, v, a, b: (layer, lo[j] // epg, 0, 0),
                      pipeline_mode=pl.Buffered(1))
    return pl.pallas_call(
        _expert_body,
        grid_spec=pltpu.PrefetchScalarGridSpec(
            num_scalar_prefetch=6,
            grid=(n_items,),
            in_specs=[x_spec, up, up, dn],
            out_specs=pl.BlockSpec((te, d), lambda j, tl, lo, hi, v, a, b: (tl[j], 0)),
            scratch_shapes=[pltpu.VMEM((epg, 2, d, f), BF16), pltpu.VMEM((epg, f, d), BF16)]),
        out_shape=jax.ShapeDtypeStruct((t, d), F32),
        compiler_params=_params("arbitrary"),
        name="moe_experts",
    )(*[maps[i, :n_items] for i in range(6)], xs, w1, w3, w2)


def _combine_body(pos_ref, ys_ref, h_ref, gate_ref, o_ref, ybuf, sem):
    tm = h_ref.shape[0]

    def issue(grp, c):
        r0 = pl.multiple_of(grp * ROW_DMA_UNROLL, ROW_DMA_UNROLL)
        for u in range(ROW_DMA_UNROLL):
            _row_copy(ys_ref, pos_ref[0, 0, r0 + u], ybuf, r0 + u, sem).start(priority=u % 2)
        return c

    lax.fori_loop(0, tm // ROW_DMA_UNROLL, issue, 0, unroll=True)
    pltpu.make_async_copy(ys_ref.at[pl.ds(0, tm)], ybuf, sem).wait()
    o_ref[...] = h_ref[...] + gate_ref[0] * ybuf[...]


def _combine(ys, pos3, h2, gate, nt_per_batch, tm):
    t, d = h2.shape
    return pl.pallas_call(
        _combine_body,
        grid=(t // tm,),
        in_specs=[pl.BlockSpec((1, 1, tm), lambda i: (i, 0, 0), memory_space=pltpu.SMEM),
                  pl.BlockSpec(memory_space=pl.ANY),
                  pl.BlockSpec((tm, d), lambda i: (i, 0)),
                  pl.BlockSpec((1, 1, d), lambda i: (i // nt_per_batch, 0, 0))],
        out_specs=pl.BlockSpec((tm, d), lambda i: (i, 0)),
        out_shape=jax.ShapeDtypeStruct((t, d), F32),
        scratch_shapes=[pltpu.VMEM((tm, d), F32), pltpu.SemaphoreType.DMA(())],
        compiler_params=_params("arbitrary"),
        name="moe_combine",
    )(pos3, ys, h2, gate)


def _moe(h2, g, shift, scale, gate, wg, bg, we, be, w1, w3, w2, layer, bsz, seq, tm, te):
    t, d = h2.shape
    nt_per_batch = seq // tm
    ne, ng = N_EXPERTS, N_EXPERT_GROUPS
    w_r = jnp.zeros((d, LANES), F32).at[:, :ne].set(we).at[:, ne:ne + ng].set(wg)
    b_r = jnp.zeros((1, LANES), F32).at[0, :ne].set(be).at[0, ne:ne + ng].set(bg)
    tri = jnp.asarray(np.tril(np.ones((tm, tm), np.float32)), BF16)
    xrow, meta_t, counts = _router(h2, g, shift, scale, w_r, b_r, tri, nt_per_batch, tm)
    n_items_max = t // te + N_PAIR_BUCKETS
    pos3, maps = _plan(counts, meta_t, te, n_items_max)
    xs = _dispatch(xrow, pos3, tm)
    ys = _experts(xs, w1, w3, w2, layer, maps, n_items_max, te)
    return _combine(ys, pos3, h2, gate, nt_per_batch, tm)


def _log_sigmoid(x):
    return jnp.minimum(x, 0.0) - jnp.log1p(jnp.exp(-jnp.abs(x)))


def _aug_tables(n_heads):
    assert n_heads * AUG_LANES_PER_HEAD <= LANES
    width = LANES
    pk = np.zeros((F_SPLIT * LANES, width), np.float32)
    pq = np.zeros((F_SPLIT * LANES, width), np.float32)
    ck = np.zeros((1, width), np.float32)
    cq = np.zeros((1, width), np.float32)
    for h in range(n_heads):
        base = h * AUG_LANES_PER_HEAD
        for j in range(F_SPLIT):
            pk[j * LANES + h, base + j] = -1.0
            pq[j * LANES + h, base + F_SPLIT + j] = 1.0
            ck[0, base + F_SPLIT + j] = 1.0
            cq[0, base + j] = 1.0
    return jnp.asarray(pk, BF16), jnp.asarray(pq, BF16), jnp.asarray(ck), jnp.asarray(cq)


def _kv_body(h_ref, g_ref, sh_ref, sc_ref, wk_ref, wvt_ref, wfh_ref, wfl_ref, fb_ref, kng_ref,
             pk_ref, pq_ref, ck_ref, cq_ref, k_ref, vt_ref, ka_ref, qa_ref, carry_ref):
    tm = h_ref.shape[0]

    @pl.when(pl.program_id(1) == 0)
    def _():
        carry_ref[...] = jnp.zeros_like(carry_ref)

    hn = _rms_mod(h_ref[...], g_ref[...], sh_ref[0], sc_ref[0])
    hb = hn.astype(BF16)
    k = jnp.dot(hb, wk_ref[...], preferred_element_type=F32)
    k_ref[...] = _head_rms(k, kng_ref[...]).astype(BF16)
    nt_dims = (((1,), (1,)), ((), ()))
    vt_ref[...] = lax.dot_general(wvt_ref[...], hb, nt_dims, preferred_element_type=F32).astype(BF16)
    fz = _dot_3pass(hn, wfh_ref, wfl_ref) + fb_ref[...]
    c = _log_sigmoid(fz)
    row = lax.broadcasted_iota(I32, c.shape, 0)
    shift = 1
    while shift < tm:
        c = c + jnp.where(row >= shift, pltpu.roll(c, shift, 0), 0.0)
        shift *= 2
    f = c + carry_ref[...]
    carry_ref[...] = f[tm - 1:tm, :]

    f2 = f * LOG2E
    hi = f2.astype(BF16)
    r1 = f2 - hi.astype(F32)
    mid = r1.astype(BF16)
    lo = (r1 - mid.astype(F32)).astype(BF16)
    pieces = jnp.concatenate([hi, mid, lo], axis=-1)
    ka_ref[...] = (jnp.dot(pieces, pk_ref[...], preferred_element_type=F32) + ck_ref[...]).astype(BF16)
    qa_ref[...] = (jnp.dot(pieces, pq_ref[...], preferred_element_type=F32) + cq_ref[...]).astype(BF16)


def _shared_kv(h2, g, shift, scale, wk, wvt, wf, fb, kng, bsz, seq, tm):
    t, d = h2.shape
    nt = seq // tm
    aw = LANES
    pk, pq, ck, cq = _aug_tables(d // HEAD_DIM)
    wf_hi, wf_lo = _split_bf16(wf)
    row = lambda b, i: (b * nt + i, 0)
    const = lambda b, i: (0, 0)
    return pl.pallas_call(
        _kv_body,
        grid=(bsz, nt),
        in_specs=[pl.BlockSpec((tm, d), row),
                  pl.BlockSpec((1, d), const),
                  pl.BlockSpec((1, 1, d), lambda b, i: (b, 0, 0)),
                  pl.BlockSpec((1, 1, d), lambda b, i: (b, 0, 0)),
                  pl.BlockSpec((d, d), const),
                  pl.BlockSpec((d, d), const),
                  pl.BlockSpec((d, LANES), const),
                  pl.BlockSpec((d, LANES), const),
                  pl.BlockSpec((1, LANES), const),
                  pl.BlockSpec((1, d), const),
                  pl.BlockSpec((F_SPLIT * LANES, aw), const),
                  pl.BlockSpec((F_SPLIT * LANES, aw), const),
                  pl.BlockSpec((1, aw), const),
                  pl.BlockSpec((1, aw), const)],
        out_specs=[pl.BlockSpec((tm, d), row),
                   pl.BlockSpec((d, tm), lambda b, i: (b, i)),
                   pl.BlockSpec((tm, aw), row),
                   pl.BlockSpec((tm, aw), row)],
        out_shape=[jax.ShapeDtypeStruct((t, d), BF16), jax.ShapeDtypeStruct((bsz * d, seq), BF16),
                   jax.ShapeDtypeStruct((t, aw), BF16), jax.ShapeDtypeStruct((t, aw), BF16)],
        scratch_shapes=[pltpu.VMEM((1, LANES), F32)],
        compiler_params=_params("parallel", "arbitrary"),
        name="shared_kv",
    )(h2, g, shift, scale, wk, wvt, wf_hi, wf_lo, fb, kng, pk, pq, ck, cq)


def _qg_body(h_ref, g_ref, sh_ref, sc_ref, wq_ref, wg_ref, qng_ref, q_ref, og_ref):
    hn = _rms_mod(h_ref[...], g_ref[...], sh_ref[0], sc_ref[0])
    hb = hn.astype(BF16)
    q = jnp.dot(hb, wq_ref[...], preferred_element_type=F32)
    q_ref[...] = (_head_rms(q, qng_ref[...]) * (HEAD_DIM ** -0.5 * LOG2E)).astype(BF16)
    og_ref[...] = jnp.dot(hb, wg_ref[...], preferred_element_type=F32).astype(BF16)


def _fox_qg(h2, g, shift, scale, wq, wg, qng, bsz, seq, tm):
    t, d = h2.shape
    nt = seq // tm
    row = lambda b, i: (b * nt + i, 0)
    const = lambda b, i: (0, 0)
    return pl.pallas_call(
        _qg_body,
        grid=(bsz, nt),
        in_specs=[pl.BlockSpec((tm, d), row),
                  pl.BlockSpec((1, d), const),
                  pl.BlockSpec((1, 1, d), lambda b, i: (b, 0, 0)),
                  pl.BlockSpec((1, 1, d), lambda b, i: (b, 0, 0)),
                  pl.BlockSpec((d, d), const),
                  pl.BlockSpec((d, d), const),
                  pl.BlockSpec((1, d), const)],
        out_specs=[pl.BlockSpec((tm, d), row), pl.BlockSpec((tm, d), row)],
        out_shape=[jax.ShapeDtypeStruct((t, d), BF16), jax.ShapeDtypeStruct((t, d), BF16)],
        compiler_params=_params("parallel", "parallel"),
        name="fox_qg",
    )(h2, g, shift, scale, wq, wg, qng)


def _attn_body(q_ref, qa_ref, k_ref, ka_ref, vt_ref, o_ref, m_ref, acc_ref, s0_ref, s1_ref, *, tq, hp):
    grp = pl.program_id(1)
    qi = pl.program_id(2)
    n_slabs = hp // 2
    lane_q = lax.broadcasted_iota(I32, (tq, LANES), 1)
    lo_q = lane_q < HEAD_DIM
    lo_v = lax.broadcasted_iota(I32, (LANES, tq), 0) < HEAD_DIM
    nt_dims = (((1,), (1,)), ((), ()))
    causal = lax.broadcasted_iota(I32, (tq, tq), 0) <= lax.broadcasted_iota(I32, (tq, tq), 1)

    qa = qa_ref[...]
    qcats = []
    for h in range(hp):
        sl, hh = h // 2, h % 2
        q2 = q_ref[:, sl * LANES:(sl + 1) * LANES]
        own = lo_q if hh == 0 else jnp.logical_not(lo_q)
        a0 = (grp * hp + h) * AUG_LANES_PER_HEAD
        own_a = (lane_q >= a0) & (lane_q < a0 + AUG_LANES_PER_HEAD)
        qcats.append(jnp.concatenate([jnp.where(own, q2, jnp.zeros_like(q2)),
                                      jnp.where(own_a, qa, jnp.zeros_like(qa))], axis=-1))

    m_ref[...] = jnp.full(m_ref.shape, NEG, F32)
    ones_r = jnp.ones((DEN_ROWS, tq), BF16)
    zeros_r = jnp.zeros((DEN_ROWS, tq), BF16)
    acc_row = lax.broadcasted_iota(I32, (LANES + 2 * DEN_ROWS, tq), 0)
    first_head_rows = (acc_row < HEAD_DIM) | ((acc_row >= LANES) & (acc_row < LANES + DEN_ROWS))
    acc_ref[...] = jnp.zeros(acc_ref.shape, F32)

    def scores(kb, dst_ref):
        s0 = pl.multiple_of(kb * tq, tq)
        for h in range(hp):
            sl = h // 2
            kcat = jnp.concatenate([k_ref[pl.ds(s0, tq), sl * LANES:(sl + 1) * LANES],
                                    ka_ref[pl.ds(s0, tq), :]], axis=-1)
            dst_ref[h] = lax.dot_general(kcat, qcats[h], nt_dims, preferred_element_type=F32)

    def consume(kb, src_ref, masked):
        s0 = pl.multiple_of(kb * tq, tq)
        for sl in range(n_slabs):
            vt = vt_ref[sl * LANES:(sl + 1) * LANES, pl.ds(s0, tq)]
            vts = (jnp.concatenate([jnp.where(lo_v, vt, jnp.zeros_like(vt)), ones_r, zeros_r], axis=0),
                   jnp.concatenate([jnp.where(lo_v, jnp.zeros_like(vt), vt), zeros_r, ones_r], axis=0))
            pv, alphas = None, []
            for hh in range(2):
                h = sl * 2 + hh
                st = src_ref[h]
                if masked:
                    st = jnp.where(causal, st, NEG)
                m_old = m_ref[h]
                m_new = jnp.maximum(m_old, jnp.max(st, axis=0, keepdims=True))
                alpha = jnp.exp2(m_old - m_new)
                p = jnp.exp2(st - m_new)
                m_ref[h] = m_new
                part = jnp.dot(vts[hh], p.astype(BF16), preferred_element_type=F32)
                pv = part if pv is None else pv + part
                alphas.append(alpha)
            acc_ref[sl] = jnp.where(first_head_rows, alphas[0], alphas[1]) * acc_ref[sl] + pv

    scores(0, s0_ref)

    def pair(j, c):
        scores(2 * j + 1, s1_ref)
        consume(2 * j, s0_ref, False)
        scores(2 * j + 2, s0_ref)
        consume(2 * j + 1, s1_ref, False)
        return c

    lax.fori_loop(0, qi // 2, pair, 0)

    @pl.when(qi % 2 == 0)
    def _():
        consume(qi, s0_ref, True)

    @pl.when(qi % 2 == 1)
    def _():
        scores(qi, s1_ref)
        consume(qi - 1, s0_ref, False)
        consume(qi, s1_ref, True)

    for sl in range(n_slabs):
        acc = acc_ref[sl]
        l2 = jnp.where(lo_v, acc[LANES:LANES + 1, :], acc[LANES + DEN_ROWS:LANES + DEN_ROWS + 1, :])
        o_ref[:, sl * LANES:(sl + 1) * LANES] = (acc[:LANES, :] / l2).T.astype(BF16)


def _fox_attention(q, qaug, k, kaug, vt, bsz, seq, tq, hp):
    t, d = q.shape
    w = hp * HEAD_DIM
    n_grp = d // w
    nq = seq // tq
    body = functools.partial(_attn_body, tq=tq, hp=hp)
    qrow = lambda b, j, i: (b * nq + i, j)
    krow = lambda b, j, i: (b, j)
    return pl.pallas_call(
        body,
        grid=(bsz, n_grp, nq),
        in_specs=[pl.BlockSpec((tq, w), qrow),
                  pl.BlockSpec((tq, LANES), lambda b, j, i: (b * nq + i, 0)),
                  pl.BlockSpec((seq, w), krow),
                  pl.BlockSpec((seq, LANES), lambda b, j, i: (b, 0)),
                  pl.BlockSpec((w, seq), lambda b, j, i: (b * n_grp + j, 0))],
        out_specs=pl.BlockSpec((tq, w), qrow),
        out_shape=jax.ShapeDtypeStruct((t, d), BF16),
        scratch_shapes=[pltpu.VMEM((hp, 1, tq), F32),
                        pltpu.VMEM((hp // 2, LANES + 2 * DEN_ROWS, tq), F32),
                        pltpu.VMEM((hp, tq, tq), F32), pltpu.VMEM((hp, tq, tq), F32)],
        compiler_params=_params("parallel", "parallel", "arbitrary"),
        name="fox_attention",
    )(q, qaug, k, kaug, vt)


def _fox_out_body(o_ref, og_ref, h_ref, w_ref, gate_ref, out_ref):
    z = o_ref[...].astype(F32) * jax.nn.sigmoid(og_ref[...].astype(F32))
    out_ref[...] = h_ref[...] + gate_ref[0] * jnp.dot(z.astype(BF16), w_ref[...], preferred_element_type=F32)


def _fox_out(o, og, h2, w_o, gate, bsz, seq, tm):
    t, d = h2.shape
    nt = seq // tm
    row = lambda b, i: (b * nt + i, 0)
    return pl.pallas_call(
        _fox_out_body,
        grid=(bsz, nt),
        in_specs=[pl.BlockSpec((tm, d), row), pl.BlockSpec((tm, d), row), pl.BlockSpec((tm, d), row),
                  pl.BlockSpec((d, d), lambda b, i: (0, 0)),
                  pl.BlockSpec((1, 1, d), lambda b, i: (b, 0, 0))],
        out_specs=pl.BlockSpec((tm, d), row),
        out_shape=jax.ShapeDtypeStruct((t, d), F32),
        compiler_params=_params("parallel", "parallel"),
        name="fox_out",
    )(o, og, h2, w_o, gate)


def _tiles(seq):
    tm = min(512, seq)
    te = min(256, seq)
    tq = min(512, seq)
    tc = min(32, seq)
    return tm, te, tq, tc


def kernel(x, c, ln_g, ada_w, ada_b, s5_w_in, s5_lambda_re, s5_lambda_im, s5_log_dt, s5_b_re, s5_b_im,
           s5_c_re, s5_c_im, s5_d, s5_w_out, kv_g, kv_ada_w, kv_ada_b, kv_w, kv_fb, k_norm_g,
           fox_w_qg, fox_q_norm_g, fox_w_o, moe_wg, moe_bg, moe_we, moe_be, moe_w1, moe_w3, moe_w2):
    bsz, seq, d = x.shape
    depth = ln_g.shape[0]
    n_a = s5_w_in.shape[0]
    n_heads = d // HEAD_DIM
    tm, te, tq, tc = _tiles(seq)

    mods = _adaln(c, ada_w.reshape(depth * 2, d, 3 * d), ada_b.reshape(depth * 2, 1, 3 * d))
    mods = mods.reshape(depth, 2, bsz, 3, 1, d)
    kv_mods = _adaln(c, kv_ada_w[None], kv_ada_b[None, None]).reshape(bsz, 2, 1, d)

    h = x.reshape(bsz * seq, d)
    k = kaug = qaug = vt = None
    for l in range(depth):
        shift, scale, gate = mods[l, 0, :, 0], mods[l, 0, :, 1], mods[l, 0, :, 2]
        g = ln_g[l, 0][None]
        if l < n_a:
            u2 = _s5_in(h, g, shift, scale, s5_w_in[l].astype(BF16), bsz, seq, tm)
            bblk, cblk, a_re, a_im = _s5_tables(s5_lambda_re[l], s5_lambda_im[l], s5_log_dt[l],
                                                s5_b_re[l], s5_b_im[l], s5_c_re[l], s5_c_im[l])
            y2 = _s5_scan(u2, bblk, cblk, a_re, a_im, bsz, seq, tc)
            h = _s5_out(y2, u2, h, s5_d[l][None], s5_w_out[l].astype(BF16), gate, bsz, seq, tm)
        else:
            j = l - n_a
            qng = jnp.tile(fox_q_norm_g[j], n_heads)[None]
            q, og = _fox_qg(h, g, shift, scale, fox_w_qg[j][:, :d].astype(BF16),
                            fox_w_qg[j][:, d:].astype(BF16), qng, bsz, seq, tm)
            o = _fox_attention(q, qaug, k, kaug, vt, bsz, seq, tq, ATTN_HEADS_PER_STEP)
            h = _fox_out(o, og, h, fox_w_o[j].astype(BF16), gate, bsz, seq, tm)

        shift, scale, gate = mods[l, 1, :, 0], mods[l, 1, :, 1], mods[l, 1, :, 2]
        h = _moe(h, ln_g[l, 1][None], shift, scale, gate, moe_wg[l], moe_bg[l], moe_we[l], moe_be[l],
                 moe_w1, moe_w3, moe_w2, l, bsz, seq, tm, te)

        if l == n_a - 1:
            wf = jnp.zeros((d, LANES), F32).at[:, :n_heads].set(kv_w[:, 2 * d:])
            fb = jnp.zeros((1, LANES), F32).at[0, :n_heads].set(kv_fb)
            kng = jnp.tile(k_norm_g, n_heads)[None]
            k, vt, kaug, qaug = _shared_kv(h, kv_g[None], kv_mods[:, 0], kv_mods[:, 1],
                                           kv_w[:, :d].astype(BF16), kv_w[:, d:2 * d].T.astype(BF16),
                                           wf, fb, kng, bsz, seq, tm)
    return h.reshape(bsz, seq, d)
```

```python
import functools
import math

import numpy as np
import jax
import jax.numpy as jnp
from jax import lax
from jax.experimental import pallas as pl
from jax.experimental.pallas import tpu as pltpu

F32 = jnp.float32
BF16 = jnp.bfloat16
I32 = jnp.int32

EPS = 1e-6
NEG = -1e30
LOG2E = math.log2(math.e)
LANES = 128
SUBLANES = 8
VMEM_LIMIT_BYTES = 56 * 1024 * 1024

S5_GROUPS_PER_BLOCK = 16
N_EXPERT_GROUPS = 4
EXPERTS_PER_GROUP = 8
N_EXPERTS = N_EXPERT_GROUPS * EXPERTS_PER_GROUP
N_BUCKETS = N_EXPERT_GROUPS * EXPERTS_PER_GROUP * EXPERTS_PER_GROUP
N_PAIR_BUCKETS = N_EXPERT_GROUPS * (EXPERTS_PER_GROUP * (EXPERTS_PER_GROUP - 1) // 2)
META_LANES = LANES
META_ROWS = SUBLANES
ROW_DMA_UNROLL = 8
PLAN_TILES_PER_STEP = 8
HEAD_DIM = 64
ATTN_HEADS_PER_STEP = 8
DEN_ROWS = 16
F_SPLIT = 3
AUG_LANES_PER_HEAD = 2 * F_SPLIT


def _params(*sem):
    return pltpu.CompilerParams(dimension_semantics=sem, vmem_limit_bytes=VMEM_LIMIT_BYTES)


def _rms_mod(x, g, shift, scale):
    ms = jnp.mean(x * x, axis=-1, keepdims=True)
    y = x * lax.rsqrt(ms + EPS) * g
    return y * (1.0 + scale) + shift


def _split_bf16(w):
    hi = w.astype(BF16)
    return hi, (w - hi.astype(F32)).astype(BF16)


def _dot_3pass(x, w_hi_ref, w_lo_ref):
    x_hi, x_lo = _split_bf16(x)
    w_hi = w_hi_ref[...]
    return (jnp.dot(x_hi, w_hi, preferred_element_type=F32)
            + jnp.dot(x_lo, w_hi, preferred_element_type=F32)
            + jnp.dot(x_hi, w_lo_ref[...], preferred_element_type=F32))


def _head_rms(x, g):
    tm, d = x.shape
    lane = lax.broadcasted_iota(I32, (tm, LANES), 1)
    lo = lane < HEAD_DIM
    outs = []
    for j in range(d // LANES):
        s = x[:, j * LANES:(j + 1) * LANES]
        sq = s * s
        s_lo = jnp.sum(jnp.where(lo, sq, 0.0), axis=-1, keepdims=True)
        s_hi = jnp.sum(jnp.where(lo, 0.0, sq), axis=-1, keepdims=True)
        r = jnp.where(lo, lax.rsqrt(s_lo / HEAD_DIM + EPS), lax.rsqrt(s_hi / HEAD_DIM + EPS))
        outs.append(s * r)
    return jnp.concatenate(outs, axis=-1) * g


def _adaln_body(c_ref, w_ref, b_ref, o_ref):
    c = c_ref[...]
    s = c * jax.nn.sigmoid(c)
    o_ref[0] = jnp.dot(s, w_ref[0], preferred_element_type=F32) + b_ref[0]


def _adaln(c, w, b):
    n_sets, d, n = w.shape
    bsz = c.shape[0]
    tn = 512 if n % 512 == 0 else n
    return pl.pallas_call(
        _adaln_body,
        grid=(n_sets, n // tn),
        in_specs=[pl.BlockSpec((bsz, d), lambda s, j: (0, 0)),
                  pl.BlockSpec((1, d, tn), lambda s, j: (s, 0, j)),
                  pl.BlockSpec((1, 1, tn), lambda s, j: (s, 0, j))],
        out_specs=pl.BlockSpec((1, bsz, tn), lambda s, j: (s, 0, j)),
        out_shape=jax.ShapeDtypeStruct((n_sets, bsz, n), F32),
        compiler_params=_params("parallel", "parallel"),
        name="adaln",
    )(c, w, b)


def _s5_in_body(x_ref, g_ref, sh_ref, sc_ref, w_ref, u_ref):
    hn = _rms_mod(x_ref[...], g_ref[...], sh_ref[0], sc_ref[0])
    u_ref[...] = jnp.dot(hn.astype(BF16), w_ref[...], preferred_element_type=F32)


def _s5_in(x2, g, shift, scale, w_in, bsz, seq, tm):
    d = x2.shape[1]
    nt = seq // tm
    row = lambda b, i: (b * nt + i, 0)
    return pl.pallas_call(
        _s5_in_body,
        grid=(bsz, nt),
        in_specs=[pl.BlockSpec((tm, d), row),
                  pl.BlockSpec((1, d), lambda b, i: (0, 0)),
                  pl.BlockSpec((1, 1, d), lambda b, i: (b, 0, 0)),
                  pl.BlockSpec((1, 1, d), lambda b, i: (b, 0, 0)),
                  pl.BlockSpec((d, d), lambda b, i: (0, 0))],
        out_specs=pl.BlockSpec((tm, d), row),
        out_shape=jax.ShapeDtypeStruct((bsz * seq, d), F32),
        compiler_params=_params("parallel", "parallel"),
        name="s5_in",
    )(x2, g, shift, scale, w_in)


def _s5_scan_body(u_hbm, bb_ref, cb_ref, are_ref, aim_ref, y_hbm,
                  ubuf, ybuf, bu_ref, st_ref, sem_in, sem_out, *, tc, nblk, sw, seq, n_chunks):
    bsz = SUBLANES
    cw = S5_GROUPS_PER_BLOCK * 16
    i = pl.program_id(0)
    slot = lax.rem(i, 2)

    def in_copy(chunk, sl, b):
        return pltpu.make_async_copy(u_hbm.at[pl.ds(b * seq + chunk * tc, tc)],
                                     ubuf.at[sl, :, b, :], sem_in.at[sl])

    def out_copy(chunk, sl, b):
        return pltpu.make_async_copy(ybuf.at[sl, :, b, :],
                                     y_hbm.at[pl.ds(b * seq + chunk * tc, tc)], sem_out.at[sl])

    @pl.when(i == 0)
    def _():
        st_ref[...] = jnp.zeros_like(st_ref)
        for b in range(bsz):
            in_copy(0, 0, b).start()

    @pl.when(i + 1 < n_chunks)
    def _():
        for b in range(bsz):
            in_copy(i + 1, 1 - slot, b).start()

    for b in range(bsz):
        in_copy(i, slot, b).wait()

    @pl.when(i >= 2)
    def _():
        for b in range(bsz):
            out_copy(i - 2, slot, b).wait()

    d = ubuf.shape[-1]
    u2 = ubuf[slot].reshape(tc * bsz, d).astype(BF16)
    for k in range(nblk):
        bu_ref[:, k * 2 * sw:(k + 1) * 2 * sw] = jnp.dot(
            u2[:, k * cw:(k + 1) * cw], bb_ref[k], preferred_element_type=F32)
        re0, im0 = k * 2 * sw, k * 2 * sw + sw
        a_re = jnp.broadcast_to(are_ref[k], (SUBLANES, sw))
        a_im = jnp.broadcast_to(aim_ref[k], (SUBLANES, sw))

        def step(t, carry, re0=re0, im0=im0, a_re=a_re, a_im=a_im):
            s_re, s_im = carry
            r0 = pl.multiple_of(t * SUBLANES, SUBLANES)
            n_re = a_re * s_re - a_im * s_im + bu_ref[pl.ds(r0, SUBLANES), re0:re0 + sw]
            n_im = a_re * s_im + a_im * s_re + bu_ref[pl.ds(r0, SUBLANES), im0:im0 + sw]
            bu_ref[pl.ds(r0, SUBLANES), re0:re0 + sw] = n_re
            bu_ref[pl.ds(r0, SUBLANES), im0:im0 + sw] = n_im
            return n_re, n_im

        s_re, s_im = lax.fori_loop(
            0, tc, step, (st_ref[:, re0:re0 + sw], st_ref[:, im0:im0 + sw]), unroll=True)
        st_ref[:, re0:re0 + sw] = s_re
        st_ref[:, im0:im0 + sw] = s_im
        s2 = bu_ref[:, k * 2 * sw:(k + 1) * 2 * sw].astype(BF16)
        yk = jnp.dot(s2, cb_ref[k], preferred_element_type=F32)
        ybuf[slot, :, :, k * cw:(k + 1) * cw] = yk.reshape(tc, bsz, cw)

    for b in range(bsz):
        out_copy(i, slot, b).start()

    @pl.when(i == n_chunks - 1)
    def _():
        if n_chunks >= 2:
            for b in range(bsz):
                out_copy(i - 1, 1 - slot, b).wait()
        for b in range(bsz):
            out_copy(i, slot, b).wait()


def _s5_scan(u2, bblk, cblk, a_re, a_im, bsz, seq, tc):
    assert bsz == SUBLANES, "the scan keeps the batch on the 8 sublanes of a vreg"
    rows, d = u2.shape
    nblk, cw, sw2 = bblk.shape
    sw = sw2 // 2
    n_chunks = seq // tc
    body = functools.partial(_s5_scan_body, tc=tc, nblk=nblk, sw=sw, seq=seq, n_chunks=n_chunks)
    return pl.pallas_call(
        body,
        grid=(n_chunks,),
        in_specs=[pl.BlockSpec(memory_space=pl.ANY),
                  pl.BlockSpec((nblk, cw, sw2), lambda i: (0, 0, 0)),
                  pl.BlockSpec((nblk, sw2, cw), lambda i: (0, 0, 0)),
                  pl.BlockSpec((nblk, 1, sw), lambda i: (0, 0, 0)),
                  pl.BlockSpec((nblk, 1, sw), lambda i: (0, 0, 0))],
        out_specs=pl.BlockSpec(memory_space=pl.ANY),
        out_shape=jax.ShapeDtypeStruct((rows, d), F32),
        scratch_shapes=[pltpu.VMEM((2, tc, bsz, d), F32), pltpu.VMEM((2, tc, bsz, d), F32),
                        pltpu.VMEM((tc * bsz, nblk * sw2), F32), pltpu.VMEM((bsz, nblk * sw2), F32),
                        pltpu.SemaphoreType.DMA((2,)), pltpu.SemaphoreType.DMA((2,))],
        compiler_params=_params("arbitrary"),
        name="s5_scan",
    )(u2, bblk, cblk, a_re, a_im)


def _s5_out_body(y_ref, u_ref, h_ref, d_ref, w_ref, gate_ref, o_ref):
    z = y_ref[...] + d_ref[...] * u_ref[...]
    act = jax.nn.gelu(z)
    vg = jnp.dot(act.astype(BF16), w_ref[...], preferred_element_type=F32)
    d = z.shape[-1]
    mix = vg[:, :d] * jax.nn.sigmoid(vg[:, d:])
    o_ref[...] = h_ref[...] + gate_ref[0] * mix


def _s5_out(y2, u2, h2, d_skip, w_out, gate, bsz, seq, tm):
    d = h2.shape[1]
    nt = seq // tm
    row = lambda b, i: (b * nt + i, 0)
    return pl.pallas_call(
        _s5_out_body,
        grid=(bsz, nt),
        in_specs=[pl.BlockSpec((tm, d), row),
                  pl.BlockSpec((tm, d), row),
                  pl.BlockSpec((tm, d), row),
                  pl.BlockSpec((1, d), lambda b, i: (0, 0)),
                  pl.BlockSpec((d, 2 * d), lambda b, i: (0, 0)),
                  pl.BlockSpec((1, 1, d), lambda b, i: (b, 0, 0))],
        out_specs=pl.BlockSpec((tm, d), row),
        out_shape=jax.ShapeDtypeStruct((bsz * seq, d), F32),
        compiler_params=_params("parallel", "parallel"),
        name="s5_out",
    )(y2, u2, h2, d_skip, w_out, gate)


def _s5_tables(lam_re, lam_im, log_dt, b_re, b_im, c_re, c_im):
    dt = jnp.exp(log_dt.astype(F32))[:, None]
    lr, li = lam_re.astype(F32), lam_im.astype(F32)
    mag = jnp.exp(lr * dt)
    a_re = mag * jnp.cos(li * dt)
    a_im = mag * jnp.sin(li * dt)
    den = lr * lr + li * li
    coef_re = ((a_re - 1.0) * lr + a_im * li) / den
    coef_im = (a_im * lr - (a_re - 1.0) * li) / den
    br_, bi_ = b_re.astype(F32), b_im.astype(F32)
    bbar_re = coef_re[..., None] * br_ - coef_im[..., None] * bi_
    bbar_im = coef_re[..., None] * bi_ + coef_im[..., None] * br_
    g, p, c = bbar_re.shape
    gb = S5_GROUPS_PER_BLOCK
    nblk = g // gb
    eye = jnp.eye(gb, dtype=F32)

    def in_blocks(m):
        return jnp.einsum('kgpc,gh->kgchp', m.reshape(nblk, gb, p, c), eye).reshape(nblk, gb * c, gb * p)

    def out_blocks(m):
        return jnp.einsum('kgcp,gh->kgphc', m.reshape(nblk, gb, c, p), eye).reshape(nblk, gb * p, gb * c)

    bblk = jnp.concatenate([in_blocks(bbar_re), in_blocks(bbar_im)], axis=-1).astype(BF16)
    cblk = jnp.concatenate([out_blocks(c_re.astype(F32)), -out_blocks(c_im.astype(F32))], axis=1).astype(BF16)
    return bblk, cblk, a_re.reshape(nblk, 1, gb * p), a_im.reshape(nblk, 1, gb * p)


def _router_body(h_ref, g_ref, sh_ref, sc_ref, wrh_ref, wrl_ref, br_ref, tri_ref,
                 x_ref, mt_ref, cnt_ref, carry_ref):
    tm, d = h_ref.shape
    ne, ng, epg = N_EXPERTS, N_EXPERT_GROUPS, EXPERTS_PER_GROUP

    @pl.when(pl.program_id(0) == 0)
    def _():
        carry_ref[...] = jnp.zeros_like(carry_ref)

    hn = _rms_mod(h_ref[...], g_ref[...], sh_ref[0], sc_ref[0])
    logits = _dot_3pass(hn, wrh_ref, wrl_ref) + br_ref[...]
    lane = lax.broadcasted_iota(I32, logits.shape, 1).astype(F32)
    big = jnp.float32(1e9)
    ninf = jnp.float32(-jnp.inf)

    gmask = (lane >= ne) & (lane < ne + ng)
    gmax = jnp.max(jnp.where(gmask, logits, ninf), axis=-1, keepdims=True)
    gsum = jnp.sum(jnp.where(gmask, jnp.exp(logits - gmax), 0.0), axis=-1, keepdims=True)
    p_g = 1.0 / gsum
    gidx = jnp.min(jnp.where(gmask & (logits == gmax), lane - ne, big), axis=-1, keepdims=True)

    emask = (lane < ne) & (jnp.floor(lane / epg) == gidx)
    v1 = jnp.max(jnp.where(emask, logits, ninf), axis=-1, keepdims=True)
    i1 = jnp.min(jnp.where(emask & (logits == v1), lane, big), axis=-1, keepdims=True)
    emask2 = emask & (lane != i1)
    v2 = jnp.max(jnp.where(emask2, logits, ninf), axis=-1, keepdims=True)
    i2 = jnp.min(jnp.where(emask2 & (logits == v2), lane, big), axis=-1, keepdims=True)
    e21 = jnp.exp(v2 - v1)
    w1 = p_g / (1.0 + e21)
    w2 = p_g * e21 / (1.0 + e21)

    first_lo = i1 < i2
    e_lo = jnp.where(first_lo, i1, i2)
    e_hi = jnp.where(first_lo, i2, i1)
    w_lo = jnp.where(first_lo, w1, w2)
    w_hi = jnp.where(first_lo, w2, w1)
    bucket = gidx * (epg * epg) + (e_lo - gidx * epg) * epg + (e_hi - gidx * epg)

    lane_b = lax.broadcasted_iota(I32, (tm, N_BUCKETS), 1).astype(F32)
    onehot = (lane_b == bucket).astype(F32)
    prefix = jnp.dot(tri_ref[...], onehot.astype(BF16), preferred_element_type=F32)
    carry = carry_ref[...]
    rank = jnp.sum(onehot * (prefix + carry), axis=-1, keepdims=True) - 1.0
    new_carry = carry + prefix[tm - 1:tm, :]
    carry_ref[...] = new_carry
    cnt_ref[...] = new_carry

    mlane = lax.broadcasted_iota(I32, (tm, META_LANES), 1)
    meta = jnp.where(mlane == 0, bucket,
           jnp.where(mlane == 1, rank,
           jnp.where(mlane == 2, w_lo,
           jnp.where(mlane == 3, w_hi, 0.0))))
    x_ref[:, :d] = hn
    x_ref[:, d:] = meta
    mt_ref[0] = meta.T[:META_ROWS, :]


def _router(h2, g, shift, scale, w_r, b_r, tri, nt_per_batch, tm):
    t, d = h2.shape
    wr_hi, wr_lo = _split_bf16(w_r)
    return pl.pallas_call(
        _router_body,
        grid=(t // tm,),
        in_specs=[pl.BlockSpec((tm, d), lambda i: (i, 0)),
                  pl.BlockSpec((1, d), lambda i: (0, 0)),
                  pl.BlockSpec((1, 1, d), lambda i: (i // nt_per_batch, 0, 0)),
                  pl.BlockSpec((1, 1, d), lambda i: (i // nt_per_batch, 0, 0)),
                  pl.BlockSpec((d, LANES), lambda i: (0, 0)),
                  pl.BlockSpec((d, LANES), lambda i: (0, 0)),
                  pl.BlockSpec((1, LANES), lambda i: (0, 0)),
                  pl.BlockSpec((tm, tm), lambda i: (0, 0))],
        out_specs=[pl.BlockSpec((tm, d + META_LANES), lambda i: (i, 0)),
                   pl.BlockSpec((1, META_ROWS, tm), lambda i: (i, 0, 0)),
                   pl.BlockSpec((1, N_BUCKETS), lambda i: (0, 0))],
        out_shape=[jax.ShapeDtypeStruct((t, d + META_LANES), F32),
                   jax.ShapeDtypeStruct((t // tm, META_ROWS, tm), F32),
                   jax.ShapeDtypeStruct((1, N_BUCKETS), F32)],
        scratch_shapes=[pltpu.VMEM((1, N_BUCKETS), F32)],
        compiler_params=_params("arbitrary"),
        name="moe_router",
    )(h2, g, shift, scale, wr_hi, wr_lo, b_r, tri)


def _plan_body(cnt_ref, mt_ref, pos_ref, maps_ref, start_ref, *, te, nwp):
    nb = N_BUCKETS
    epg = EXPERTS_PER_GROUP

    @pl.when(pl.program_id(0) == 0)
    def _():
        r = lax.broadcasted_iota(I32, (nb, nb), 0)
        c = lax.broadcasted_iota(I32, (nb, nb), 1)
        nt_dims = (((1,), (1,)), ((), ()))

        def column(mask, row_vals):
            row8 = jnp.broadcast_to(row_vals, (SUBLANES, nb)).astype(BF16)
            return lax.dot_general(mask.astype(BF16), row8, nt_dims, preferred_element_type=F32)[:, :1]

        cnt = cnt_ref[...]
        cnt_hi = jnp.floor(cnt / 256.0)
        cnt_lo = cnt - 256.0 * cnt_hi
        start = 256.0 * column(c < r, cnt_hi) + column(c < r, cnt_lo)
        count = 256.0 * column(c == r, cnt_hi) + column(c == r, cnt_lo)
        end = start + count
        start_ref[...] = start
        first_tile = jnp.floor(start / te)
        n_items = jnp.where(count > 0.0, jnp.floor((end - 1.0) / te) - first_tile + 1.0, 0.0)
        items8 = jnp.broadcast_to(n_items, (nb, LANES)).astype(BF16)
        item_end = jnp.dot((c <= r).astype(BF16), items8, preferred_element_type=F32)[:, :1]
        item_start = item_end - n_items
        n_total = item_end[nb - 1:nb, :]

        w = lax.broadcasted_iota(I32, (1, nwp), 1).astype(F32)
        wc = jnp.minimum(w, jnp.maximum(n_total - 1.0, 0.0))
        bucket = jnp.sum((item_end <= wc).astype(F32), axis=0, keepdims=True)
        sel = lax.broadcasted_iota(I32, (nb, nwp), 0).astype(F32) == bucket

        def pick(col):
            return jnp.sum(jnp.where(sel, col, 0.0), axis=0, keepdims=True)

        valid = (w < n_total).astype(F32)
        tile = pick(first_tile) + (wc - pick(item_start))
        row_lo = (jnp.maximum(pick(start), tile * te) - tile * te) * valid
        row_hi = (jnp.minimum(pick(end), (tile + 1.0) * te) - tile * te) * valid
        grp = jnp.floor(bucket / (epg * epg))
        within = bucket - grp * (epg * epg)
        lo = jnp.floor(within / epg)
        e_lo = grp * epg + lo
        e_hi = grp * epg + (within - lo * epg)
        row = lax.broadcasted_iota(I32, (SUBLANES, nwp), 0)
        maps = jnp.where(row == 0, tile, jnp.where(row == 1, e_lo, jnp.where(row == 2, e_hi,
               jnp.where(row == 3, valid, jnp.where(row == 4, row_lo, jnp.where(row == 5, row_hi, 0.0))))))
        maps_ref[...] = maps.astype(I32)

    n_sub, _, tm = mt_ref.shape
    rb = lax.broadcasted_iota(I32, (nb, tm), 0).astype(F32)
    for s in range(n_sub):
        bucket_row = mt_ref[s, 0:1, :]
        rank_row = mt_ref[s, 1:2, :]
        pos = jnp.sum(jnp.where(rb == bucket_row, start_ref[...], 0.0), axis=0, keepdims=True) + rank_row
        pos_ref[s] = pos.astype(I32)


def _plan(counts, meta_t, te, n_items_max):
    n_tt, _, tm = meta_t.shape
    nwp = ((n_items_max + LANES - 1) // LANES) * LANES
    body = functools.partial(_plan_body, te=te, nwp=nwp)
    n_sub = math.gcd(n_tt, PLAN_TILES_PER_STEP)
    return pl.pallas_call(
        body,
        grid=(n_tt // n_sub,),
        in_specs=[pl.BlockSpec((1, N_BUCKETS), lambda i: (0, 0)),
                  pl.BlockSpec((n_sub, META_ROWS, tm), lambda i: (i, 0, 0))],
        out_specs=[pl.BlockSpec((n_sub, 1, tm), lambda i: (i, 0, 0)),
                   pl.BlockSpec((SUBLANES, nwp), lambda i: (0, 0))],
        out_shape=[jax.ShapeDtypeStruct((n_tt, 1, tm), I32),
                   jax.ShapeDtypeStruct((SUBLANES, nwp), I32)],
        scratch_shapes=[pltpu.VMEM((N_BUCKETS, 1), F32)],
        compiler_params=_params("arbitrary"),
        name="moe_plan",
    )(counts, meta_t)


def _row_copy(src_ref, src_row, dst_ref, dst_row, sem):
    return pltpu.make_async_copy(src_ref.at[pl.ds(src_row, 1)], dst_ref.at[pl.ds(dst_row, 1)], sem)


def _dispatch_body(pos_ref, x_ref, xs_ref, sem):
    tm = x_ref.shape[0]

    def issue(grp, c):
        r0 = pl.multiple_of(grp * ROW_DMA_UNROLL, ROW_DMA_UNROLL)
        for u in range(ROW_DMA_UNROLL):
            _row_copy(x_ref, r0 + u, xs_ref, pos_ref[0, 0, r0 + u], sem).start(priority=u % 2)
        return c

    lax.fori_loop(0, tm // ROW_DMA_UNROLL, issue, 0, unroll=True)
    pltpu.make_async_copy(x_ref, xs_ref.at[pl.ds(0, tm)], sem).wait()


def _dispatch(xrow, pos3, tm):
    t, w = xrow.shape
    t_pad = t
    return pl.pallas_call(
        _dispatch_body,
        grid=(t // tm,),
        in_specs=[pl.BlockSpec((1, 1, tm), lambda i: (i, 0, 0), memory_space=pltpu.SMEM),
                  pl.BlockSpec((tm, w), lambda i: (i, 0))],
        out_specs=pl.BlockSpec(memory_space=pl.ANY),
        out_shape=jax.ShapeDtypeStruct((t_pad, w), F32),
        scratch_shapes=[pltpu.SemaphoreType.DMA(())],
        compiler_params=_params("arbitrary"),
        name="moe_dispatch",
    )(pos3, xrow)


def _expert_body(tile_ref, elo_ref, ehi_ref, valid_ref, rlo_ref, rhi_ref, x_ref,
                 w1_ref, w3_ref, w2_ref, o_ref, up_s, dn_s):
    del tile_ref
    te, d = o_ref.shape
    epg = EXPERTS_PER_GROUP
    j = pl.program_id(0)
    prev = jnp.maximum(j - 1, 0)

    @pl.when((j == 0) | (elo_ref[j] // epg != elo_ref[prev] // epg))
    def _():
        for e in range(epg):
            up_s[e, 0] = w1_ref[e].astype(BF16)
            up_s[e, 1] = w3_ref[e].astype(BF16)
            dn_s[e] = w2_ref[e].astype(BF16)

    @pl.when(valid_ref[j] == 1)
    def _():
        rows = lax.broadcasted_iota(I32, (te, 1), 0)
        live = (rows >= rlo_ref[j]) & (rows < rhi_ref[j])
        x = x_ref[:, :d].astype(BF16)

        def ffn(e, wt):
            a = jnp.dot(x, up_s[e, 0], preferred_element_type=F32)
            b = jnp.dot(x, up_s[e, 1], preferred_element_type=F32)
            mid = (a * jax.nn.sigmoid(a)) * b
            y = jnp.dot(mid.astype(BF16), dn_s[e], preferred_element_type=F32)
            return wt * y

        res = (ffn(elo_ref[j] % epg, jnp.where(live, x_ref[:, d + 2:d + 3], 0.0))
               + ffn(ehi_ref[j] % epg, jnp.where(live, x_ref[:, d + 3:d + 4], 0.0)))

        @pl.when(rlo_ref[j] == 0)
        def _():
            o_ref[...] = res

        @pl.when(rlo_ref[j] != 0)
        def _():
            o_ref[...] += res


def _experts(xs, w1, w3, w2, layer, maps, n_items, te):
    t, w = xs.shape
    _, n_e, d, f = w1.shape
    epg = EXPERTS_PER_GROUP
    x_spec = pl.BlockSpec((te, w), lambda j, tl, lo, hi, v, a, b: (tl[j], 0))
    up = pl.BlockSpec((None, epg, d, f), lambda j, tl, lo, hi, v, a, b: (layer, lo[j] // epg, 0, 0),
                      pipeline_mode=pl.Buffered(1))
    dn = pl.BlockSpec((None, epg, f, d), lambda j, tl, lo, hi, v, a, b: (layer, lo[j] // epg, 0, 0),
                      pipeline_mode=pl.Buffered(1))
    return pl.pallas_call(
        _expert_body,
        grid_spec=pltpu.PrefetchScalarGridSpec(
            num_scalar_prefetch=6,
            grid=(n_items,),
            in_specs=[x_spec, up, up, dn],
            out_specs=pl.BlockSpec((te, d), lambda j, tl, lo, hi, v, a, b: (tl[j], 0)),
            scratch_shapes=[pltpu.VMEM((epg, 2, d, f), BF16), pltpu.VMEM((epg, f, d), BF16)]),
        out_shape=jax.ShapeDtypeStruct((t, d), F32),
        compiler_params=_params("arbitrary"),
        name="moe_experts",
    )(*[maps[i, :n_items] for i in range(6)], xs, w1, w3, w2)


def _combine_body(pos_ref, ys_ref, h_ref, gate_ref, o_ref, ybuf, sem):
    tm = h_ref.shape[0]

    def issue(grp, c):
        r0 = pl.multiple_of(grp * ROW_DMA_UNROLL, ROW_DMA_UNROLL)
        for u in range(ROW_DMA_UNROLL):
            _row_copy(ys_ref, pos_ref[0, 0, r0 + u], ybuf, r0 + u, sem).start(priority=u % 2)
        return c

    lax.fori_loop(0, tm // ROW_DMA_UNROLL, issue, 0, unroll=True)
    pltpu.make_async_copy(ys_ref.at[pl.ds(0, tm)], ybuf, sem).wait()
    o_ref[...] = h_ref[...] + gate_ref[0] * ybuf[...]


def _combine(ys, pos3, h2, gate, nt_per_batch, tm):
    t, d = h2.shape
    return pl.pallas_call(
        _combine_body,
        grid=(t // tm,),
        in_specs=[pl.BlockSpec((1, 1, tm), lambda i: (i, 0, 0), memory_space=pltpu.SMEM),
                  pl.BlockSpec(memory_space=pl.ANY),
                  pl.BlockSpec((tm, d), lambda i: (i, 0)),
                  pl.BlockSpec((1, 1, d), lambda i: (i // nt_per_batch, 0, 0))],
        out_specs=pl.BlockSpec((tm, d), lambda i: (i, 0)),
        out_shape=jax.ShapeDtypeStruct((t, d), F32),
        scratch_shapes=[pltpu.VMEM((tm, d), F32), pltpu.SemaphoreType.DMA(())],
        compiler_params=_params("arbitrary"),
        name="moe_combine",
    )(pos3, ys, h2, gate)


def _moe(h2, g, shift, scale, gate, wg, bg, we, be, w1, w3, w2, layer, bsz, seq, tm, te):
    t, d = h2.shape
    nt_per_batch = seq // tm
    ne, ng = N_EXPERTS, N_EXPERT_GROUPS
    w_r = jnp.zeros((d, LANES), F32).at[:, :ne].set(we).at[:, ne:ne + ng].set(wg)
    b_r = jnp.zeros((1, LANES), F32).at[0, :ne].set(be).at[0, ne:ne + ng].set(bg)
    tri = jnp.asarray(np.tril(np.ones((tm, tm), np.float32)), BF16)
    xrow, meta_t, counts = _router(h2, g, shift, scale, w_r, b_r, tri, nt_per_batch, tm)
    n_items_max = t // te + N_PAIR_BUCKETS
    pos3, maps = _plan(counts, meta_t, te, n_items_max)
    xs = _dispatch(xrow, pos3, tm)
    ys = _experts(xs, w1, w3, w2, layer, maps, n_items_max, te)
    return _combine(ys, pos3, h2, gate, nt_per_batch, tm)


def _log_sigmoid(x):
    return jnp.minimum(x, 0.0) - jnp.log1p(jnp.exp(-jnp.abs(x)))


def _aug_tables(n_heads):
    assert n_heads * AUG_LANES_PER_HEAD <= LANES
    width = LANES
    pk = np.zeros((F_SPLIT * LANES, width), np.float32)
    pq = np.zeros((F_SPLIT * LANES, width), np.float32)
    ck = np.zeros((1, width), np.float32)
    cq = np.zeros((1, width), np.float32)
    for h in range(n_heads):
        base = h * AUG_LANES_PER_HEAD
        for j in range(F_SPLIT):
            pk[j * LANES + h, base + j] = -1.0
            pq[j * LANES + h, base + F_SPLIT + j] = 1.0
            ck[0, base + F_SPLIT + j] = 1.0
            cq[0, base + j] = 1.0
    return jnp.asarray(pk, BF16), jnp.asarray(pq, BF16), jnp.asarray(ck), jnp.asarray(cq)


def _kv_body(h_ref, g_ref, sh_ref, sc_ref, wk_ref, wvt_ref, wfh_ref, wfl_ref, fb_ref, kng_ref,
             pk_ref, pq_ref, ck_ref, cq_ref, k_ref, vt_ref, ka_ref, qa_ref, carry_ref):
    tm = h_ref.shape[0]

    @pl.when(pl.program_id(1) == 0)
    def _():
        carry_ref[...] = jnp.zeros_like(carry_ref)

    hn = _rms_mod(h_ref[...], g_ref[...], sh_ref[0], sc_ref[0])
    hb = hn.astype(BF16)
    k = jnp.dot(hb, wk_ref[...], preferred_element_type=F32)
    k_ref[...] = _head_rms(k, kng_ref[...]).astype(BF16)
    nt_dims = (((1,), (1,)), ((), ()))
    vt_ref[...] = lax.dot_general(wvt_ref[...], hb, nt_dims, preferred_element_type=F32).astype(BF16)
    fz = _dot_3pass(hn, wfh_ref, wfl_ref) + fb_ref[...]
    c = _log_sigmoid(fz)
    row = lax.broadcasted_iota(I32, c.shape, 0)
    shift = 1
    while shift < tm:
        c = c + jnp.where(row >= shift, pltpu.roll(c, shift, 0), 0.0)
        shift *= 2
    f = c + carry_ref[...]
    carry_ref[...] = f[tm - 1:tm, :]

    f2 = f * LOG2E
    hi = f2.astype(BF16)
    r1 = f2 - hi.astype(F32)
    mid = r1.astype(BF16)
    lo = (r1 - mid.astype(F32)).astype(BF16)
    pieces = jnp.concatenate([hi, mid, lo], axis=-1)
    ka_ref[...] = (jnp.dot(pieces, pk_ref[...], preferred_element_type=F32) + ck_ref[...]).astype(BF16)
    qa_ref[...] = (jnp.dot(pieces, pq_ref[...], preferred_element_type=F32) + cq_ref[...]).astype(BF16)


def _shared_kv(h2, g, shift, scale, wk, wvt, wf, fb, kng, bsz, seq, tm):
    t, d = h2.shape
    nt = seq // tm
    aw = LANES
    pk, pq, ck, cq = _aug_tables(d // HEAD_DIM)
    wf_hi, wf_lo = _split_bf16(wf)
    row = lambda b, i: (b * nt + i, 0)
    const = lambda b, i: (0, 0)
    return pl.pallas_call(
        _kv_body,
        grid=(bsz, nt),
        in_specs=[pl.BlockSpec((tm, d), row),
                  pl.BlockSpec((1, d), const),
                  pl.BlockSpec((1, 1, d), lambda b, i: (b, 0, 0)),
                  pl.BlockSpec((1, 1, d), lambda b, i: (b, 0, 0)),
                  pl.BlockSpec((d, d), const),
                  pl.BlockSpec((d, d), const),
                  pl.BlockSpec((d, LANES), const),
                  pl.BlockSpec((d, LANES), const),
                  pl.BlockSpec((1, LANES), const),
                  pl.BlockSpec((1, d), const),
                  pl.BlockSpec((F_SPLIT * LANES, aw), const),
                  pl.BlockSpec((F_SPLIT * LANES, aw), const),
                  pl.BlockSpec((1, aw), const),
                  pl.BlockSpec((1, aw), const)],
        out_specs=[pl.BlockSpec((tm, d), row),
                   pl.BlockSpec((d, tm), lambda b, i: (b, i)),
                   pl.BlockSpec((tm, aw), row),
                   pl.BlockSpec((tm, aw), row)],
        out_shape=[jax.ShapeDtypeStruct((t, d), BF16), jax.ShapeDtypeStruct((bsz * d, seq), BF16),
                   jax.ShapeDtypeStruct((t, aw), BF16), jax.ShapeDtypeStruct((t, aw), BF16)],
        scratch_shapes=[pltpu.VMEM((1, LANES), F32)],
        compiler_params=_params("parallel", "arbitrary"),
        name="shared_kv",
    )(h2, g, shift, scale, wk, wvt, wf_hi, wf_lo, fb, kng, pk, pq, ck, cq)


def _qg_body(h_ref, g_ref, sh_ref, sc_ref, wq_ref, wg_ref, qng_ref, q_ref, og_ref):
    hn = _rms_mod(h_ref[...], g_ref[...], sh_ref[0], sc_ref[0])
    hb = hn.astype(BF16)
    q = jnp.dot(hb, wq_ref[...], preferred_element_type=F32)
    q_ref[...] = (_head_rms(q, qng_ref[...]) * (HEAD_DIM ** -0.5 * LOG2E)).astype(BF16)
    og_ref[...] = jnp.dot(hb, wg_ref[...], preferred_element_type=F32).astype(BF16)


def _fox_qg(h2, g, shift, scale, wq, wg, qng, bsz, seq, tm):
    t, d = h2.shape
    nt = seq // tm
    row = lambda b, i: (b * nt + i, 0)
    const = lambda b, i: (0, 0)
    return pl.pallas_call(
        _qg_body,
        grid=(bsz, nt),
        in_specs=[pl.BlockSpec((tm, d), row),
                  pl.BlockSpec((1, d), const),
                  pl.BlockSpec((1, 1, d), lambda b, i: (b, 0, 0)),
                  pl.BlockSpec((1, 1, d), lambda b, i: (b, 0, 0)),
                  pl.BlockSpec((d, d), const),
                  pl.BlockSpec((d, d), const),
                  pl.BlockSpec((1, d), const)],
        out_specs=[pl.BlockSpec((tm, d), row), pl.BlockSpec((tm, d), row)],
        out_shape=[jax.ShapeDtypeStruct((t, d), BF16), jax.ShapeDtypeStruct((t, d), BF16)],
        compiler_params=_params("parallel", "parallel"),
        name="fox_qg",
    )(h2, g, shift, scale, wq, wg, qng)


def _attn_body(q_ref, qa_ref, k_ref, ka_ref, vt_ref, o_ref, m_ref, acc_ref, s0_ref, s1_ref, *, tq, hp):
    grp = pl.program_id(1)
    qi = pl.program_id(2)
    n_slabs = hp // 2
    lane_q = lax.broadcasted_iota(I32, (tq, LANES), 1)
    lo_q = lane_q < HEAD_DIM
    lo_v = lax.broadcasted_iota(I32, (LANES, tq), 0) < HEAD_DIM
    nt_dims = (((1,), (1,)), ((), ()))
    causal = lax.broadcasted_iota(I32, (tq, tq), 0) <= lax.broadcasted_iota(I32, (tq, tq), 1)

    qa = qa_ref[...]
    qcats = []
    for h in range(hp):
        sl, hh = h // 2, h % 2
        q2 = q_ref[:, sl * LANES:(sl + 1) * LANES]
        own = lo_q if hh == 0 else jnp.logical_not(lo_q)
        a0 = (grp * hp + h) * AUG_LANES_PER_HEAD
        own_a = (lane_q >= a0) & (lane_q < a0 + AUG_LANES_PER_HEAD)
        qcats.append(jnp.concatenate([jnp.where(own, q2, jnp.zeros_like(q2)),
                                      jnp.where(own_a, qa, jnp.zeros_like(qa))], axis=-1))

    m_ref[...] = jnp.full(m_ref.shape, NEG, F32)
    ones_r = jnp.ones((DEN_ROWS, tq), BF16)
    zeros_r = jnp.zeros((DEN_ROWS, tq), BF16)
    acc_row = lax.broadcasted_iota(I32, (LANES + 2 * DEN_ROWS, tq), 0)
    first_head_rows = (acc_row < HEAD_DIM) | ((acc_row >= LANES) & (acc_row < LANES + DEN_ROWS))
    acc_ref[...] = jnp.zeros(acc_ref.shape, F32)

    def scores(kb, dst_ref):
        s0 = pl.multiple_of(kb * tq, tq)
        for h in range(hp):
            sl = h // 2
            kcat = jnp.concatenate([k_ref[pl.ds(s0, tq), sl * LANES:(sl + 1) * LANES],
                                    ka_ref[pl.ds(s0, tq), :]], axis=-1)
            dst_ref[h] = lax.dot_general(kcat, qcats[h], nt_dims, preferred_element_type=F32)

    def consume(kb, src_ref, masked):
        s0 = pl.multiple_of(kb * tq, tq)
        for sl in range(n_slabs):
            vt = vt_ref[sl * LANES:(sl + 1) * LANES, pl.ds(s0, tq)]
            vts = (jnp.concatenate([jnp.where(lo_v, vt, jnp.zeros_like(vt)), ones_r, zeros_r], axis=0),
                   jnp.concatenate([jnp.where(lo_v, jnp.zeros_like(vt), vt), zeros_r, ones_r], axis=0))
            pv, alphas = None, []
            for hh in range(2):
                h = sl * 2 + hh
                st = src_ref[h]
                if masked:
                    st = jnp.where(causal, st, NEG)
                m_old = m_ref[h]
                m_new = jnp.maximum(m_old, jnp.max(st, axis=0, keepdims=True))
                alpha = jnp.exp2(m_old - m_new)
                p = jnp.exp2(st - m_new)
                m_ref[h] = m_new
                part = jnp.dot(vts[hh], p.astype(BF16), preferred_element_type=F32)
                pv = part if pv is None else pv + part
                alphas.append(alpha)
            acc_ref[sl] = jnp.where(first_head_rows, alphas[0], alphas[1]) * acc_ref[sl] + pv

    scores(0, s0_ref)

    def pair(j, c):
        scores(2 * j + 1, s1_ref)
        consume(2 * j, s0_ref, False)
        scores(2 * j + 2, s0_ref)
        consume(2 * j + 1, s1_ref, False)
        return c

    lax.fori_loop(0, qi // 2, pair, 0)

    @pl.when(qi % 2 == 0)
    def _():
        consume(qi, s0_ref, True)

    @pl.when(qi % 2 == 1)
    def _():
        scores(qi, s1_ref)
        consume(qi - 1, s0_ref, False)
        consume(qi, s1_ref, True)

    for sl in range(n_slabs):
        acc = acc_ref[sl]
        l2 = jnp.where(lo_v, acc[LANES:LANES + 1, :], acc[LANES + DEN_ROWS:LANES + DEN_ROWS + 1, :])
        o_ref[:, sl * LANES:(sl + 1) * LANES] = (acc[:LANES, :] / l2).T.astype(BF16)


def _fox_attention(q, qaug, k, kaug, vt, bsz, seq, tq, hp):
    t, d = q.shape
    w = hp * HEAD_DIM
    n_grp = d // w
    nq = seq // tq
    body = functools.partial(_attn_body, tq=tq, hp=hp)
    qrow = lambda b, j, i: (b * nq + i, j)
    krow = lambda b, j, i: (b, j)
    return pl.pallas_call(
        body,
        grid=(bsz, n_grp, nq),
        in_specs=[pl.BlockSpec((tq, w), qrow),
                  pl.BlockSpec((tq, LANES), lambda b, j, i: (b * nq + i, 0)),
                  pl.BlockSpec((seq, w), krow),
                  pl.BlockSpec((seq, LANES), lambda b, j, i: (b, 0)),
                  pl.BlockSpec((w, seq), lambda b, j, i: (b * n_grp + j, 0))],
        out_specs=pl.BlockSpec((tq, w), qrow),
        out_shape=jax.ShapeDtypeStruct((t, d), BF16),
        scratch_shapes=[pltpu.VMEM((hp, 1, tq), F32),
                        pltpu.VMEM((hp // 2, LANES + 2 * DEN_ROWS, tq), F32),
                        pltpu.VMEM((hp, tq, tq), F32), pltpu.VMEM((hp, tq, tq), F32)],
        compiler_params=_params("parallel", "parallel", "arbitrary"),
        name="fox_attention",
    )(q, qaug, k, kaug, vt)


def _fox_out_body(o_ref, og_ref, h_ref, w_ref, gate_ref, out_ref):
    z = o_ref[...].astype(F32) * jax.nn.sigmoid(og_ref[...].astype(F32))
    out_ref[...] = h_ref[...] + gate_ref[0] * jnp.dot(z.astype(BF16), w_ref[...], preferred_element_type=F32)


def _fox_out(o, og, h2, w_o, gate, bsz, seq, tm):
    t, d = h2.shape
    nt = seq // tm
    row = lambda b, i: (b * nt + i, 0)
    return pl.pallas_call(
        _fox_out_body,
        grid=(bsz, nt),
        in_specs=[pl.BlockSpec((tm, d), row), pl.BlockSpec((tm, d), row), pl.BlockSpec((tm, d), row),
                  pl.BlockSpec((d, d), lambda b, i: (0, 0)),
                  pl.BlockSpec((1, 1, d), lambda b, i: (b, 0, 0))],
        out_specs=pl.BlockSpec((tm, d), row),
        out_shape=jax.ShapeDtypeStruct((t, d), F32),
        compiler_params=_params("parallel", "parallel"),
        name="fox_out",
    )(o, og, h2, w_o, gate)


def _tiles(seq):
    tm = min(512, seq)
    te = min(256, seq)
    tq = min(512, seq)
    tc = min(32, seq)
    return tm, te, tq, tc


def kernel(x, c, ln_g, ada_w, ada_b, s5_w_in, s5_lambda_re, s5_lambda_im, s5_log_dt, s5_b_re, s5_b_im,
           s5_c_re, s5_c_im, s5_d, s5_w_out, kv_g, kv_ada_w, kv_ada_b, kv_w, kv_fb, k_norm_g,
           fox_w_qg, fox_q_norm_g, fox_w_o, moe_wg, moe_bg, moe_we, moe_be, moe_w1, moe_w3, moe_w2):
    bsz, seq, d = x.shape
    depth = ln_g.shape[0]
    n_a = s5_w_in.shape[0]
    n_heads = d // HEAD_DIM
    tm, te, tq, tc = _tiles(seq)

    mods = _adaln(c, ada_w.reshape(depth * 2, d, 3 * d), ada_b.reshape(depth * 2, 1, 3 * d))
    mods = mods.reshape(depth, 2, bsz, 3, 1, d)
    kv_mods = _adaln(c, kv_ada_w[None], kv_ada_b[None, None]).reshape(bsz, 2, 1, d)

    h = x.reshape(bsz * seq, d)
    k = kaug = qaug = vt = None
    for l in range(depth):
        shift, scale, gate = mods[l, 0, :, 0], mods[l, 0, :, 1], mods[l, 0, :, 2]
        g = ln_g[l, 0][None]
        if l < n_a:
            u2 = _s5_in(h, g, shift, scale, s5_w_in[l].astype(BF16), bsz, seq, tm)
            bblk, cblk, a_re, a_im = _s5_tables(s5_lambda_re[l], s5_lambda_im[l], s5_log_dt[l],
                                                s5_b_re[l], s5_b_im[l], s5_c_re[l], s5_c_im[l])
            y2 = _s5_scan(u2, bblk, cblk, a_re, a_im, bsz, seq, tc)
            h = _s5_out(y2, u2, h, s5_d[l][None], s5_w_out[l].astype(BF16), gate, bsz, seq, tm)
        else:
            j = l - n_a
            qng = jnp.tile(fox_q_norm_g[j], n_heads)[None]
            q, og = _fox_qg(h, g, shift, scale, fox_w_qg[j][:, :d].astype(BF16),
                            fox_w_qg[j][:, d:].astype(BF16), qng, bsz, seq, tm)
            o = _fox_attention(q, qaug, k, kaug, vt, bsz, seq, tq, min(ATTN_HEADS_PER_STEP, n_heads))
            h = _fox_out(o, og, h, fox_w_o[j].astype(BF16), gate, bsz, seq, tm)

        shift, scale, gate = mods[l, 1, :, 0], mods[l, 1, :, 1], mods[l, 1, :, 2]
        h = _moe(h, ln_g[l, 1][None], shift, scale, gate, moe_wg[l], moe_bg[l], moe_we[l], moe_be[l],
                 moe_w1, moe_w3, moe_w2, l, bsz, seq, tm, te)

        if l == n_a - 1:
            wf = jnp.zeros((d, LANES), F32).at[:, :n_heads].set(kv_w[:, 2 * d:])
            fb = jnp.zeros((1, LANES), F32).at[0, :n_heads].set(kv_fb)
            kng = jnp.tile(k_norm_g, n_heads)[None]
            k, vt, kaug, qaug = _shared_kv(h, kv_g[None], kv_mods[:, 0], kv_mods[:, 1],
                                           kv_w[:, :d].astype(BF16), kv_w[:, d:2 * d].T.astype(BF16),
                                           wf, fb, kng, bsz, seq, tm)
    return h.reshape(bsz, seq, d)
```

```python
import functools
import math

import numpy as np
import jax
import jax.numpy as jnp
from jax import lax
from jax.experimental import pallas as pl
from jax.experimental.pallas import tpu as pltpu

F32 = jnp.float32
BF16 = jnp.bfloat16
I32 = jnp.int32

EPS = 1e-6
NEG = -1e30
LOG2E = math.log2(math.e)
LANES = 128
SUBLANES = 8
VMEM_LIMIT_BYTES = 56 * 1024 * 1024

S5_GROUPS_PER_BLOCK = 16
N_EXPERT_GROUPS = 4
EXPERTS_PER_GROUP = 8
N_EXPERTS = N_EXPERT_GROUPS * EXPERTS_PER_GROUP
N_BUCKETS = N_EXPERT_GROUPS * EXPERTS_PER_GROUP * EXPERTS_PER_GROUP
N_PAIR_BUCKETS = N_EXPERT_GROUPS * (EXPERTS_PER_GROUP * (EXPERTS_PER_GROUP - 1) // 2)
META_LANES = LANES
META_ROWS = SUBLANES
ROW_DMA_UNROLL = 8
PLAN_TILES_PER_STEP = 8
HEAD_DIM = 64
ATTN_HEADS_PER_STEP = 8
DEN_ROWS = 16
F_SPLIT = 3
AUG_LANES_PER_HEAD = 2 * F_SPLIT


def _params(*sem):
    return pltpu.CompilerParams(dimension_semantics=sem, vmem_limit_bytes=VMEM_LIMIT_BYTES)


def _rms_mod(x, g, shift, scale):
    ms = jnp.mean(x * x, axis=-1, keepdims=True)
    y = x * lax.rsqrt(ms + EPS) * g
    return y * (1.0 + scale) + shift


def _split_bf16(w):
    hi = w.astype(BF16)
    return hi, (w - hi.astype(F32)).astype(BF16)


def _dot_3pass(x, w_hi_ref, w_lo_ref):
    x_hi, x_lo = _split_bf16(x)
    w_hi = w_hi_ref[...]
    return (jnp.dot(x_hi, w_hi, preferred_element_type=F32)
            + jnp.dot(x_lo, w_hi, preferred_element_type=F32)
            + jnp.dot(x_hi, w_lo_ref[...], preferred_element_type=F32))


def _head_rms(x, g):
    tm, d = x.shape
    lane = lax.broadcasted_iota(I32, (tm, LANES), 1)
    lo = lane < HEAD_DIM
    outs = []
    for j in range(d // LANES):
        s = x[:, j * LANES:(j + 1) * LANES]
        sq = s * s
        s_lo = jnp.sum(jnp.where(lo, sq, 0.0), axis=-1, keepdims=True)
        s_hi = jnp.sum(jnp.where(lo, 0.0, sq), axis=-1, keepdims=True)
        r = jnp.where(lo, lax.rsqrt(s_lo / HEAD_DIM + EPS), lax.rsqrt(s_hi / HEAD_DIM + EPS))
        outs.append(s * r)
    return jnp.concatenate(outs, axis=-1) * g


def _adaln_body(c_ref, w_ref, b_ref, o_ref):
    c = c_ref[...]
    s = c * jax.nn.sigmoid(c)
    o_ref[0] = jnp.dot(s, w_ref[0], preferred_element_type=F32) + b_ref[0]


def _adaln(c, w, b):
    n_sets, d, n = w.shape
    bsz = c.shape[0]
    tn = 512 if n % 512 == 0 else n
    return pl.pallas_call(
        _adaln_body,
        grid=(n_sets, n // tn),
        in_specs=[pl.BlockSpec((bsz, d), lambda s, j: (0, 0)),
                  pl.BlockSpec((1, d, tn), lambda s, j: (s, 0, j)),
                  pl.BlockSpec((1, 1, tn), lambda s, j: (s, 0, j))],
        out_specs=pl.BlockSpec((1, bsz, tn), lambda s, j: (s, 0, j)),
        out_shape=jax.ShapeDtypeStruct((n_sets, bsz, n), F32),
        compiler_params=_params("parallel", "parallel"),
        name="adaln",
    )(c, w, b)


def _s5_in_body(x_ref, g_ref, sh_ref, sc_ref, w_ref, u_ref):
    hn = _rms_mod(x_ref[...], g_ref[...], sh_ref[0], sc_ref[0])
    u_ref[...] = jnp.dot(hn.astype(BF16), w_ref[...], preferred_element_type=F32)


def _s5_in(x2, g, shift, scale, w_in, bsz, seq, tm):
    d = x2.shape[1]
    nt = seq // tm
    row = lambda b, i: (b * nt + i, 0)
    return pl.pallas_call(
        _s5_in_body,
        grid=(bsz, nt),
        in_specs=[pl.BlockSpec((tm, d), row),
                  pl.BlockSpec((1, d), lambda b, i: (0, 0)),
                  pl.BlockSpec((1, 1, d), lambda b, i: (b, 0, 0)),
                  pl.BlockSpec((1, 1, d), lambda b, i: (b, 0, 0)),
                  pl.BlockSpec((d, d), lambda b, i: (0, 0))],
        out_specs=pl.BlockSpec((tm, d), row),
        out_shape=jax.ShapeDtypeStruct((bsz * seq, d), F32),
        compiler_params=_params("parallel", "parallel"),
        name="s5_in",
    )(x2, g, shift, scale, w_in)


def _s5_scan_body(u_hbm, bb_ref, cb_ref, are_ref, aim_ref, y_hbm,
                  ubuf, ybuf, bu_ref, st_ref, sem_in, sem_out, *, tc, nblk, sw, seq, n_chunks):
    bsz = SUBLANES
    cw = S5_GROUPS_PER_BLOCK * 16
    i = pl.program_id(0)
    slot = lax.rem(i, 2)

    def in_copy(chunk, sl, b):
        return pltpu.make_async_copy(u_hbm.at[pl.ds(b * seq + chunk * tc, tc)],
                                     ubuf.at[sl, :, b, :], sem_in.at[sl])

    def out_copy(chunk, sl, b):
        return pltpu.make_async_copy(ybuf.at[sl, :, b, :],
                                     y_hbm.at[pl.ds(b * seq + chunk * tc, tc)], sem_out.at[sl])

    @pl.when(i == 0)
    def _():
        st_ref[...] = jnp.zeros_like(st_ref)
        for b in range(bsz):
            in_copy(0, 0, b).start()

    @pl.when(i + 1 < n_chunks)
    def _():
        for b in range(bsz):
            in_copy(i + 1, 1 - slot, b).start()

    for b in range(bsz):
        in_copy(i, slot, b).wait()

    @pl.when(i >= 2)
    def _():
        for b in range(bsz):
            out_copy(i - 2, slot, b).wait()

    d = ubuf.shape[-1]
    u2 = ubuf[slot].reshape(tc * bsz, d).astype(BF16)
    for k in range(nblk):
        bu_ref[:, k * 2 * sw:(k + 1) * 2 * sw] = jnp.dot(
            u2[:, k * cw:(k + 1) * cw], bb_ref[k], preferred_element_type=F32)
        re0, im0 = k * 2 * sw, k * 2 * sw + sw
        a_re = jnp.broadcast_to(are_ref[k], (SUBLANES, sw))
        a_im = jnp.broadcast_to(aim_ref[k], (SUBLANES, sw))

        def step(t, carry, re0=re0, im0=im0, a_re=a_re, a_im=a_im):
            s_re, s_im = carry
            r0 = pl.multiple_of(t * SUBLANES, SUBLANES)
            n_re = a_re * s_re - a_im * s_im + bu_ref[pl.ds(r0, SUBLANES), re0:re0 + sw]
            n_im = a_re * s_im + a_im * s_re + bu_ref[pl.ds(r0, SUBLANES), im0:im0 + sw]
            bu_ref[pl.ds(r0, SUBLANES), re0:re0 + sw] = n_re
            bu_ref[pl.ds(r0, SUBLANES), im0:im0 + sw] = n_im
            return n_re, n_im

        s_re, s_im = lax.fori_loop(
            0, tc, step, (st_ref[:, re0:re0 + sw], st_ref[:, im0:im0 + sw]), unroll=True)
        st_ref[:, re0:re0 + sw] = s_re
        st_ref[:, im0:im0 + sw] = s_im
        s2 = bu_ref[:, k * 2 * sw:(k + 1) * 2 * sw].astype(BF16)
        yk = jnp.dot(s2, cb_ref[k], preferred_element_type=F32)
        ybuf[slot, :, :, k * cw:(k + 1) * cw] = yk.reshape(tc, bsz, cw)

    for b in range(bsz):
        out_copy(i, slot, b).start()

    @pl.when(i == n_chunks - 1)
    def _():
        if n_chunks >= 2:
            for b in range(bsz):
                out_copy(i - 1, 1 - slot, b).wait()
        for b in range(bsz):
            out_copy(i, slot, b).wait()


def _s5_scan(u2, bblk, cblk, a_re, a_im, bsz, seq, tc):
    assert bsz == SUBLANES, "the scan keeps the batch on the 8 sublanes of a vreg"
    rows, d = u2.shape
    nblk, cw, sw2 = bblk.shape
    sw = sw2 // 2
    n_chunks = seq // tc
    body = functools.partial(_s5_scan_body, tc=tc, nblk=nblk, sw=sw, seq=seq, n_chunks=n_chunks)
    return pl.pallas_call(
        body,
        grid=(n_chunks,),
        in_specs=[pl.BlockSpec(memory_space=pl.ANY),
                  pl.BlockSpec((nblk, cw, sw2), lambda i: (0, 0, 0)),
                  pl.BlockSpec((nblk, sw2, cw), lambda i: (0, 0, 0)),
                  pl.BlockSpec((nblk, 1, sw), lambda i: (0, 0, 0)),
                  pl.BlockSpec((nblk, 1, sw), lambda i: (0, 0, 0))],
        out_specs=pl.BlockSpec(memory_space=pl.ANY),
        out_shape=jax.ShapeDtypeStruct((rows, d), F32),
        scratch_shapes=[pltpu.VMEM((2, tc, bsz, d), F32), pltpu.VMEM((2, tc, bsz, d), F32),
                        pltpu.VMEM((tc * bsz, nblk * sw2), F32), pltpu.VMEM((bsz, nblk * sw2), F32),
                        pltpu.SemaphoreType.DMA((2,)), pltpu.SemaphoreType.DMA((2,))],
        compiler_params=_params("arbitrary"),
        name="s5_scan",
    )(u2, bblk, cblk, a_re, a_im)


def _s5_out_body(y_ref, u_ref, h_ref, d_ref, w_ref, gate_ref, o_ref):
    z = y_ref[...] + d_ref[...] * u_ref[...]
    act = jax.nn.gelu(z)
    vg = jnp.dot(act.astype(BF16), w_ref[...], preferred_element_type=F32)
    d = z.shape[-1]
    mix = vg[:, :d] * jax.nn.sigmoid(vg[:, d:])
    o_ref[...] = h_ref[...] + gate_ref[0] * mix


def _s5_out(y2, u2, h2, d_skip, w_out, gate, bsz, seq, tm):
    d = h2.shape[1]
    nt = seq // tm
    row = lambda b, i: (b * nt + i, 0)
    return pl.pallas_call(
        _s5_out_body,
        grid=(bsz, nt),
        in_specs=[pl.BlockSpec((tm, d), row),
                  pl.BlockSpec((tm, d), row),
                  pl.BlockSpec((tm, d), row),
                  pl.BlockSpec((1, d), lambda b, i: (0, 0)),
                  pl.BlockSpec((d, 2 * d), lambda b, i: (0, 0)),
                  pl.BlockSpec((1, 1, d), lambda b, i: (b, 0, 0))],
        out_specs=pl.BlockSpec((tm, d), row),
        out_shape=jax.ShapeDtypeStruct((bsz * seq, d), F32),
        compiler_params=_params("parallel", "parallel"),
        name="s5_out",
    )(y2, u2, h2, d_skip, w_out, gate)


def _s5_tables(lam_re, lam_im, log_dt, b_re, b_im, c_re, c_im):
    dt = jnp.exp(log_dt.astype(F32))[:, None]
    lr, li = lam_re.astype(F32), lam_im.astype(F32)
    mag = jnp.exp(lr * dt)
    a_re = mag * jnp.cos(li * dt)
    a_im = mag * jnp.sin(li * dt)
    den = lr * lr + li * li
    coef_re = ((a_re - 1.0) * lr + a_im * li) / den
    coef_im = (a_im * lr - (a_re - 1.0) * li) / den
    br_, bi_ = b_re.astype(F32), b_im.astype(F32)
    bbar_re = coef_re[..., None] * br_ - coef_im[..., None] * bi_
    bbar_im = coef_re[..., None] * bi_ + coef_im[..., None] * br_
    g, p, c = bbar_re.shape
    gb = S5_GROUPS_PER_BLOCK
    nblk = g // gb
    eye = jnp.eye(gb, dtype=F32)

    def in_blocks(m):
        return jnp.einsum('kgpc,gh->kgchp', m.reshape(nblk, gb, p, c), eye).reshape(nblk, gb * c, gb * p)

    def out_blocks(m):
        return jnp.einsum('kgcp,gh->kgphc', m.reshape(nblk, gb, c, p), eye).reshape(nblk, gb * p, gb * c)

    bblk = jnp.concatenate([in_blocks(bbar_re), in_blocks(bbar_im)], axis=-1).astype(BF16)
    cblk = jnp.concatenate([out_blocks(c_re.astype(F32)), -out_blocks(c_im.astype(F32))], axis=1).astype(BF16)
    return bblk, cblk, a_re.reshape(nblk, 1, gb * p), a_im.reshape(nblk, 1, gb * p)


def _router_body(h_ref, g_ref, sh_ref, sc_ref, wrh_ref, wrl_ref, br_ref, tri_ref,
                 x_ref, mt_ref, cnt_ref, carry_ref):
    tm, d = h_ref.shape
    ne, ng, epg = N_EXPERTS, N_EXPERT_GROUPS, EXPERTS_PER_GROUP

    @pl.when(pl.program_id(0) == 0)
    def _():
        carry_ref[...] = jnp.zeros_like(carry_ref)

    hn = _rms_mod(h_ref[...], g_ref[...], sh_ref[0], sc_ref[0])
    logits = _dot_3pass(hn, wrh_ref, wrl_ref) + br_ref[...]
    lane = lax.broadcasted_iota(I32, logits.shape, 1).astype(F32)
    big = jnp.float32(1e9)
    ninf = jnp.float32(-jnp.inf)

    gmask = (lane >= ne) & (lane < ne + ng)
    gmax = jnp.max(jnp.where(gmask, logits, ninf), axis=-1, keepdims=True)
    gsum = jnp.sum(jnp.where(gmask, jnp.exp(logits - gmax), 0.0), axis=-1, keepdims=True)
    p_g = 1.0 / gsum
    gidx = jnp.min(jnp.where(gmask & (logits == gmax), lane - ne, big), axis=-1, keepdims=True)

    emask = (lane < ne) & (jnp.floor(lane / epg) == gidx)
    v1 = jnp.max(jnp.where(emask, logits, ninf), axis=-1, keepdims=True)
    i1 = jnp.min(jnp.where(emask & (logits == v1), lane, big), axis=-1, keepdims=True)
    emask2 = emask & (lane != i1)
    v2 = jnp.max(jnp.where(emask2, logits, ninf), axis=-1, keepdims=True)
    i2 = jnp.min(jnp.where(emask2 & (logits == v2), lane, big), axis=-1, keepdims=True)
    e21 = jnp.exp(v2 - v1)
    w1 = p_g / (1.0 + e21)
    w2 = p_g * e21 / (1.0 + e21)

    first_lo = i1 < i2
    e_lo = jnp.where(first_lo, i1, i2)
    e_hi = jnp.where(first_lo, i2, i1)
    w_lo = jnp.where(first_lo, w1, w2)
    w_hi = jnp.where(first_lo, w2, w1)
    bucket = gidx * (epg * epg) + (e_lo - gidx * epg) * epg + (e_hi - gidx * epg)

    lane_b = lax.broadcasted_iota(I32, (tm, N_BUCKETS), 1).astype(F32)
    onehot = (lane_b == bucket).astype(F32)
    prefix = jnp.dot(tri_ref[...], onehot.astype(BF16), preferred_element_type=F32)
    carry = carry_ref[...]
    rank = jnp.sum(onehot * (prefix + carry), axis=-1, keepdims=True) - 1.0
    new_carry = carry + prefix[tm - 1:tm, :]
    carry_ref[...] = new_carry
    cnt_ref[...] = new_carry

    mlane = lax.broadcasted_iota(I32, (tm, META_LANES), 1)
    meta = jnp.where(mlane == 0, bucket,
           jnp.where(mlane == 1, rank,
           jnp.where(mlane == 2, w_lo,
           jnp.where(mlane == 3, w_hi, 0.0))))
    x_ref[:, :d] = hn
    x_ref[:, d:] = meta
    mt_ref[0] = meta.T[:META_ROWS, :]


def _router(h2, g, shift, scale, w_r, b_r, tri, nt_per_batch, tm):
    t, d = h2.shape
    wr_hi, wr_lo = _split_bf16(w_r)
    return pl.pallas_call(
        _router_body,
        grid=(t // tm,),
        in_specs=[pl.BlockSpec((tm, d), lambda i: (i, 0)),
                  pl.BlockSpec((1, d), lambda i: (0, 0)),
                  pl.BlockSpec((1, 1, d), lambda i: (i // nt_per_batch, 0, 0)),
                  pl.BlockSpec((1, 1, d), lambda i: (i // nt_per_batch, 0, 0)),
                  pl.BlockSpec((d, LANES), lambda i: (0, 0)),
                  pl.BlockSpec((d, LANES), lambda i: (0, 0)),
                  pl.BlockSpec((1, LANES), lambda i: (0, 0)),
                  pl.BlockSpec((tm, tm), lambda i: (0, 0))],
        out_specs=[pl.BlockSpec((tm, d + META_LANES), lambda i: (i, 0)),
                   pl.BlockSpec((1, META_ROWS, tm), lambda i: (i, 0, 0)),
                   pl.BlockSpec((1, N_BUCKETS), lambda i: (0, 0))],
        out_shape=[jax.ShapeDtypeStruct((t, d + META_LANES), F32),
                   jax.ShapeDtypeStruct((t // tm, META_ROWS, tm), F32),
                   jax.ShapeDtypeStruct((1, N_BUCKETS), F32)],
        scratch_shapes=[pltpu.VMEM((1, N_BUCKETS), F32)],
        compiler_params=_params("arbitrary"),
        name="moe_router",
    )(h2, g, shift, scale, wr_hi, wr_lo, b_r, tri)


def _plan_body(cnt_ref, mt_ref, pos_ref, maps_ref, start_ref, *, te, nwp):
    nb = N_BUCKETS
    epg = EXPERTS_PER_GROUP

    @pl.when(pl.program_id(0) == 0)
    def _():
        r = lax.broadcasted_iota(I32, (nb, nb), 0)
        c = lax.broadcasted_iota(I32, (nb, nb), 1)
        nt_dims = (((1,), (1,)), ((), ()))

        def column(mask, row_vals):
            row8 = jnp.broadcast_to(row_vals, (SUBLANES, nb)).astype(BF16)
            return lax.dot_general(mask.astype(BF16), row8, nt_dims, preferred_element_type=F32)[:, :1]

        cnt = cnt_ref[...]
        cnt_hi = jnp.floor(cnt / 256.0)
        cnt_lo = cnt - 256.0 * cnt_hi
        start = 256.0 * column(c < r, cnt_hi) + column(c < r, cnt_lo)
        count = 256.0 * column(c == r, cnt_hi) + column(c == r, cnt_lo)
        end = start + count
        start_ref[...] = start
        first_tile = jnp.floor(start / te)
        n_items = jnp.where(count > 0.0, jnp.floor((end - 1.0) / te) - first_tile + 1.0, 0.0)
        items8 = jnp.broadcast_to(n_items, (nb, LANES)).astype(BF16)
        item_end = jnp.dot((c <= r).astype(BF16), items8, preferred_element_type=F32)[:, :1]
        item_start = item_end - n_items
        n_total = item_end[nb - 1:nb, :]

        w = lax.broadcasted_iota(I32, (1, nwp), 1).astype(F32)
        wc = jnp.minimum(w, jnp.maximum(n_total - 1.0, 0.0))
        bucket = jnp.sum((item_end <= wc).astype(F32), axis=0, keepdims=True)
        sel = lax.broadcasted_iota(I32, (nb, nwp), 0).astype(F32) == bucket

        def pick(col):
            return jnp.sum(jnp.where(sel, col, 0.0), axis=0, keepdims=True)

        valid = (w < n_total).astype(F32)
        tile = pick(first_tile) + (wc - pick(item_start))
        row_lo = (jnp.maximum(pick(start), tile * te) - tile * te) * valid
        row_hi = (jnp.minimum(pick(end), (tile + 1.0) * te) - tile * te) * valid
        grp = jnp.floor(bucket / (epg * epg))
        within = bucket - grp * (epg * epg)
        lo = jnp.floor(within / epg)
        e_lo = grp * epg + lo
        e_hi = grp * epg + (within - lo * epg)
        row = lax.broadcasted_iota(I32, (SUBLANES, nwp), 0)
        maps = jnp.where(row == 0, tile, jnp.where(row == 1, e_lo, jnp.where(row == 2, e_hi,
               jnp.where(row == 3, valid, jnp.where(row == 4, row_lo, jnp.where(row == 5, row_hi, 0.0))))))
        maps_ref[...] = maps.astype(I32)

    n_sub, _, tm = mt_ref.shape
    rb = lax.broadcasted_iota(I32, (nb, tm), 0).astype(F32)
    for s in range(n_sub):
        bucket_row = mt_ref[s, 0:1, :]
        rank_row = mt_ref[s, 1:2, :]
        pos = jnp.sum(jnp.where(rb == bucket_row, start_ref[...], 0.0), axis=0, keepdims=True) + rank_row
        pos_ref[s] = pos.astype(I32)


def _plan(counts, meta_t, te, n_items_max):
    n_tt, _, tm = meta_t.shape
    nwp = ((n_items_max + LANES - 1) // LANES) * LANES
    body = functools.partial(_plan_body, te=te, nwp=nwp)
    n_sub = math.gcd(n_tt, PLAN_TILES_PER_STEP)
    return pl.pallas_call(
        body,
        grid=(n_tt // n_sub,),
        in_specs=[pl.BlockSpec((1, N_BUCKETS), lambda i: (0, 0)),
                  pl.BlockSpec((n_sub, META_ROWS, tm), lambda i: (i, 0, 0))],
        out_specs=[pl.BlockSpec((n_sub, 1, tm), lambda i: (i, 0, 0)),
                   pl.BlockSpec((SUBLANES, nwp), lambda i: (0, 0))],
        out_shape=[jax.ShapeDtypeStruct((n_tt, 1, tm), I32),
                   jax.ShapeDtypeStruct((SUBLANES, nwp), I32)],
        scratch_shapes=[pltpu.VMEM((N_BUCKETS, 1), F32)],
        compiler_params=_params("arbitrary"),
        name="moe_plan",
    )(counts, meta_t)


def _row_copy(src_ref, src_row, dst_ref, dst_row, sem):
    return pltpu.make_async_copy(src_ref.at[pl.ds(src_row, 1)], dst_ref.at[pl.ds(dst_row, 1)], sem)


def _dispatch_body(pos_ref, x_ref, xs_ref, sem):
    tm = x_ref.shape[0]

    def issue(grp, c):
        r0 = pl.multiple_of(grp * ROW_DMA_UNROLL, ROW_DMA_UNROLL)
        for u in range(ROW_DMA_UNROLL):
            _row_copy(x_ref, r0 + u, xs_ref, pos_ref[0, 0, r0 + u], sem).start(priority=u % 2)
        return c

    lax.fori_loop(0, tm // ROW_DMA_UNROLL, issue, 0, unroll=True)
    pltpu.make_async_copy(x_ref, xs_ref.at[pl.ds(0, tm)], sem).wait()


def _dispatch(xrow, pos3, tm):
    t, w = xrow.shape
    t_pad = t
    return pl.pallas_call(
        _dispatch_body,
        grid=(t // tm,),
        in_specs=[pl.BlockSpec((1, 1, tm), lambda i: (i, 0, 0), memory_space=pltpu.SMEM),
                  pl.BlockSpec((tm, w), lambda i: (i, 0))],
        out_specs=pl.BlockSpec(memory_space=pl.ANY),
        out_shape=jax.ShapeDtypeStruct((t_pad, w), F32),
        scratch_shapes=[pltpu.SemaphoreType.DMA(())],
        compiler_params=_params("arbitrary"),
        name="moe_dispatch",
    )(pos3, xrow)


def _expert_body(tile_ref, elo_ref, ehi_ref, valid_ref, rlo_ref, rhi_ref, x_ref,
                 w1_ref, w3_ref, w2_ref, o_ref, up_s, dn_s):
    del tile_ref
    te, d = o_ref.shape
    epg = EXPERTS_PER_GROUP
    j = pl.program_id(0)
    prev = jnp.maximum(j - 1, 0)

    @pl.when((j == 0) | (elo_ref[j] // epg != elo_ref[prev] // epg))
    def _():
        for e in range(epg):
            up_s[e, 0] = w1_ref[e].astype(BF16)
            up_s[e, 1] = w3_ref[e].astype(BF16)
            dn_s[e] = w2_ref[e].astype(BF16)

    @pl.when(valid_ref[j] == 1)
    def _():
        rows = lax.broadcasted_iota(I32, (te, 1), 0)
        live = (rows >= rlo_ref[j]) & (rows < rhi_ref[j])
        x = x_ref[:, :d].astype(BF16)

        def ffn(e, wt):
            a = jnp.dot(x, up_s[e, 0], preferred_element_type=F32)
            b = jnp.dot(x, up_s[e, 1], preferred_element_type=F32)
            mid = (a * jax.nn.sigmoid(a)) * b * wt
            return jnp.dot(mid.astype(BF16), dn_s[e], preferred_element_type=F32)

        res = (ffn(elo_ref[j] % epg, jnp.where(live, x_ref[:, d + 2:d + 3], 0.0))
               + ffn(ehi_ref[j] % epg, jnp.where(live, x_ref[:, d + 3:d + 4], 0.0)))

        @pl.when(rlo_ref[j] == 0)
        def _():
            o_ref[...] = res

        @pl.when(rlo_ref[j] != 0)
        def _():
            o_ref[...] += res


def _experts(xs, w1, w3, w2, layer, maps, n_items, te):
    t, w = xs.shape
    _, n_e, d, f = w1.shape
    epg = EXPERTS_PER_GROUP
    x_spec = pl.BlockSpec((te, w), lambda j, tl, lo, hi, v, a, b: (tl[j], 0))
    up = pl.BlockSpec((None, epg, d, f), lambda j, tl, lo, hi, v, a, b: (layer, lo[j] // epg, 0, 0),
                      pipeline_mode=pl.Buffered(1))
    dn = pl.BlockSpec((None, epg, f, d), lambda j, tl, lo, hi, v, a, b: (layer, lo[j] // epg, 0, 0),
                      pipeline_mode=pl.Buffered(1))
    return pl.pallas_call(
        _expert_body,
        grid_spec=pltpu.PrefetchScalarGridSpec(
            num_scalar_prefetch=6,
            grid=(n_items,),
            in_specs=[x_spec, up, up, dn],
            out_specs=pl.BlockSpec((te, d), lambda j, tl, lo, hi, v, a, b: (tl[j], 0)),
            scratch_shapes=[pltpu.VMEM((epg, 2, d, f), BF16), pltpu.VMEM((epg, f, d), BF16)]),
        out_shape=jax.ShapeDtypeStruct((t, d), F32),
        compiler_params=_params("arbitrary"),
        name="moe_experts",
    )(*[maps[i, :n_items] for i in range(6)], xs, w1, w3, w2)


def _combine_body(pos_ref, ys_ref, h_ref, gate_ref, o_ref, ybuf, sem):
    tm = h_ref.shape[0]

    def issue(grp, c):
        r0 = pl.multiple_of(grp * ROW_DMA_UNROLL, ROW_DMA_UNROLL)
        for u in range(ROW_DMA_UNROLL):
            _row_copy(ys_ref, pos_ref[0, 0, r0 + u], ybuf, r0 + u, sem).start(priority=u % 2)
        return c

    lax.fori_loop(0, tm // ROW_DMA_UNROLL, issue, 0, unroll=True)
    pltpu.make_async_copy(ys_ref.at[pl.ds(0, tm)], ybuf, sem).wait()
    o_ref[...] = h_ref[...] + gate_ref[0] * ybuf[...]


def _combine(ys, pos3, h2, gate, nt_per_batch, tm):
    t, d = h2.shape
    return pl.pallas_call(
        _combine_body,
        grid=(t // tm,),
        in_specs=[pl.BlockSpec((1, 1, tm), lambda i: (i, 0, 0), memory_space=pltpu.SMEM),
                  pl.BlockSpec(memory_space=pl.ANY),
                  pl.BlockSpec((tm, d), lambda i: (i, 0)),
                  pl.BlockSpec((1, 1, d), lambda i: (i // nt_per_batch, 0, 0))],
        out_specs=pl.BlockSpec((tm, d), lambda i: (i, 0)),
        out_shape=jax.ShapeDtypeStruct((t, d), F32),
        scratch_shapes=[pltpu.VMEM((tm, d), F32), pltpu.SemaphoreType.DMA(())],
        compiler_params=_params("arbitrary"),
        name="moe_combine",
    )(pos3, ys, h2, gate)


def _moe(h2, g, shift, scale, gate, wg, bg, we, be, w1, w3, w2, layer, bsz, seq, tm, te):
    t, d = h2.shape
    nt_per_batch = seq // tm
    ne, ng = N_EXPERTS, N_EXPERT_GROUPS
    w_r = jnp.zeros((d, LANES), F32).at[:, :ne].set(we).at[:, ne:ne + ng].set(wg)
    b_r = jnp.zeros((1, LANES), F32).at[0, :ne].set(be).at[0, ne:ne + ng].set(bg)
    tri = jnp.asarray(np.tril(np.ones((tm, tm), np.float32)), BF16)
    xrow, meta_t, counts = _router(h2, g, shift, scale, w_r, b_r, tri, nt_per_batch, tm)
    n_items_max = t // te + N_PAIR_BUCKETS
    pos3, maps = _plan(counts, meta_t, te, n_items_max)
    xs = _dispatch(xrow, pos3, tm)
    ys = _experts(xs, w1, w3, w2, layer, maps, n_items_max, te)
    return _combine(ys, pos3, h2, gate, nt_per_batch, tm)


def _log_sigmoid(x):
    return jnp.minimum(x, 0.0) - jnp.log1p(jnp.exp(-jnp.abs(x)))


def _aug_tables(n_heads):
    assert n_heads * AUG_LANES_PER_HEAD <= LANES
    width = LANES
    pk = np.zeros((F_SPLIT * LANES, width), np.float32)
    pq = np.zeros((F_SPLIT * LANES, width), np.float32)
    ck = np.zeros((1, width), np.float32)
    cq = np.zeros((1, width), np.float32)
    for h in range(n_heads):
        base = h * AUG_LANES_PER_HEAD
        for j in range(F_SPLIT):
            pk[j * LANES + h, base + j] = -1.0
            pq[j * LANES + h, base + F_SPLIT + j] = 1.0
            ck[0, base + F_SPLIT + j] = 1.0
            cq[0, base + j] = 1.0
    return jnp.asarray(pk, BF16), jnp.asarray(pq, BF16), jnp.asarray(ck), jnp.asarray(cq)


def _kv_body(h_ref, g_ref, sh_ref, sc_ref, wk_ref, wvt_ref, wfh_ref, wfl_ref, fb_ref, kng_ref,
             pk_ref, pq_ref, ck_ref, cq_ref, k_ref, vt_ref, ka_ref, qa_ref, carry_ref):
    tm = h_ref.shape[0]

    @pl.when(pl.program_id(1) == 0)
    def _():
        carry_ref[...] = jnp.zeros_like(carry_ref)

    hn = _rms_mod(h_ref[...], g_ref[...], sh_ref[0], sc_ref[0])
    hb = hn.astype(BF16)
    k = jnp.dot(hb, wk_ref[...], preferred_element_type=F32)
    k_ref[...] = _head_rms(k, kng_ref[...]).astype(BF16)
    nt_dims = (((1,), (1,)), ((), ()))
    vt_ref[...] = lax.dot_general(wvt_ref[...], hb, nt_dims, preferred_element_type=F32).astype(BF16)
    fz = _dot_3pass(hn, wfh_ref, wfl_ref) + fb_ref[...]
    c = _log_sigmoid(fz)
    row = lax.broadcasted_iota(I32, c.shape, 0)
    shift = 1
    while shift < tm:
        c = c + jnp.where(row >= shift, pltpu.roll(c, shift, 0), 0.0)
        shift *= 2
    f = c + carry_ref[...]
    carry_ref[...] = f[tm - 1:tm, :]

    f2 = f * LOG2E
    hi = f2.astype(BF16)
    r1 = f2 - hi.astype(F32)
    mid = r1.astype(BF16)
    lo = (r1 - mid.astype(F32)).astype(BF16)
    pieces = jnp.concatenate([hi, mid, lo], axis=-1)
    ka_ref[...] = (jnp.dot(pieces, pk_ref[...], preferred_element_type=F32) + ck_ref[...]).astype(BF16)
    qa_ref[...] = (jnp.dot(pieces, pq_ref[...], preferred_element_type=F32) + cq_ref[...]).astype(BF16)


def _shared_kv(h2, g, shift, scale, wk, wvt, wf, fb, kng, bsz, seq, tm):
    t, d = h2.shape
    nt = seq // tm
    aw = LANES
    pk, pq, ck, cq = _aug_tables(d // HEAD_DIM)
    wf_hi, wf_lo = _split_bf16(wf)
    row = lambda b, i: (b * nt + i, 0)
    const = lambda b, i: (0, 0)
    return pl.pallas_call(
        _kv_body,
        grid=(bsz, nt),
        in_specs=[pl.BlockSpec((tm, d), row),
                  pl.BlockSpec((1, d), const),
                  pl.BlockSpec((1, 1, d), lambda b, i: (b, 0, 0)),
                  pl.BlockSpec((1, 1, d), lambda b, i: (b, 0, 0)),
                  pl.BlockSpec((d, d), const),
                  pl.BlockSpec((d, d), const),
                  pl.BlockSpec((d, LANES), const),
                  pl.BlockSpec((d, LANES), const),
                  pl.BlockSpec((1, LANES), const),
                  pl.BlockSpec((1, d), const),
                  pl.BlockSpec((F_SPLIT * LANES, aw), const),
                  pl.BlockSpec((F_SPLIT * LANES, aw), const),
                  pl.BlockSpec((1, aw), const),
                  pl.BlockSpec((1, aw), const)],
        out_specs=[pl.BlockSpec((tm, d), row),
                   pl.BlockSpec((d, tm), lambda b, i: (b, i)),
                   pl.BlockSpec((tm, aw), row),
                   pl.BlockSpec((tm, aw), row)],
        out_shape=[jax.ShapeDtypeStruct((t, d), BF16), jax.ShapeDtypeStruct((bsz * d, seq), BF16),
                   jax.ShapeDtypeStruct((t, aw), BF16), jax.ShapeDtypeStruct((t, aw), BF16)],
        scratch_shapes=[pltpu.VMEM((1, LANES), F32)],
        compiler_params=_params("parallel", "arbitrary"),
        name="shared_kv",
    )(h2, g, shift, scale, wk, wvt, wf_hi, wf_lo, fb, kng, pk, pq, ck, cq)


def _qg_body(h_ref, g_ref, sh_ref, sc_ref, wq_ref, wg_ref, qng_ref, q_ref, og_ref):
    hn = _rms_mod(h_ref[...], g_ref[...], sh_ref[0], sc_ref[0])
    hb = hn.astype(BF16)
    q = jnp.dot(hb, wq_ref[...], preferred_element_type=F32)
    q_ref[...] = (_head_rms(q, qng_ref[...]) * (HEAD_DIM ** -0.5 * LOG2E)).astype(BF16)
    og_ref[...] = jnp.dot(hb, wg_ref[...], preferred_element_type=F32).astype(BF16)


def _fox_qg(h2, g, shift, scale, wq, wg, qng, bsz, seq, tm):
    t, d = h2.shape
    nt = seq // tm
    row = lambda b, i: (b * nt + i, 0)
    const = lambda b, i: (0, 0)
    return pl.pallas_call(
        _qg_body,
        grid=(bsz, nt),
        in_specs=[pl.BlockSpec((tm, d), row),
                  pl.BlockSpec((1, d), const),
                  pl.BlockSpec((1, 1, d), lambda b, i: (b, 0, 0)),
                  pl.BlockSpec((1, 1, d), lambda b, i: (b, 0, 0)),
                  pl.BlockSpec((d, d), const),
                  pl.BlockSpec((d, d), const),
                  pl.BlockSpec((1, d), const)],
        out_specs=[pl.BlockSpec((tm, d), row), pl.BlockSpec((tm, d), row)],
        out_shape=[jax.ShapeDtypeStruct((t, d), BF16), jax.ShapeDtypeStruct((t, d), BF16)],
        compiler_params=_params("parallel", "parallel"),
        name="fox_qg",
    )(h2, g, shift, scale, wq, wg, qng)


def _attn_body(q_ref, qa_ref, k_ref, ka_ref, vt_ref, o_ref, m_ref, acc_ref, s0_ref, s1_ref, *, tq, hp):
    grp = pl.program_id(1)
    qi = pl.program_id(2)
    n_slabs = hp // 2
    lane_q = lax.broadcasted_iota(I32, (tq, LANES), 1)
    lo_q = lane_q < HEAD_DIM
    lo_v = lax.broadcasted_iota(I32, (LANES, tq), 0) < HEAD_DIM
    nt_dims = (((1,), (1,)), ((), ()))
    causal = lax.broadcasted_iota(I32, (tq, tq), 0) <= lax.broadcasted_iota(I32, (tq, tq), 1)

    qa = qa_ref[...]
    qcats = []
    for h in range(hp):
        sl, hh = h // 2, h % 2
        q2 = q_ref[:, sl * LANES:(sl + 1) * LANES]
        own = lo_q if hh == 0 else jnp.logical_not(lo_q)
        a0 = (grp * hp + h) * AUG_LANES_PER_HEAD
        own_a = (lane_q >= a0) & (lane_q < a0 + AUG_LANES_PER_HEAD)
        qcats.append(jnp.concatenate([jnp.where(own, q2, jnp.zeros_like(q2)),
                                      jnp.where(own_a, qa, jnp.zeros_like(qa))], axis=-1))

    m_ref[...] = jnp.full(m_ref.shape, NEG, F32)
    ones_r = jnp.ones((DEN_ROWS, tq), BF16)
    zeros_r = jnp.zeros((DEN_ROWS, tq), BF16)
    acc_row = lax.broadcasted_iota(I32, (LANES + 2 * DEN_ROWS, tq), 0)
    first_head_rows = (acc_row < HEAD_DIM) | ((acc_row >= LANES) & (acc_row < LANES + DEN_ROWS))
    acc_ref[...] = jnp.zeros(acc_ref.shape, F32)

    def scores(kb, dst_ref):
        s0 = pl.multiple_of(kb * tq, tq)
        for h in range(hp):
            sl = h // 2
            kcat = jnp.concatenate([k_ref[pl.ds(s0, tq), sl * LANES:(sl + 1) * LANES],
                                    ka_ref[pl.ds(s0, tq), :]], axis=-1)
            dst_ref[h] = lax.dot_general(kcat, qcats[h], nt_dims, preferred_element_type=F32)

    def consume(kb, src_ref, masked):
        s0 = pl.multiple_of(kb * tq, tq)
        for sl in range(n_slabs):
            vt = vt_ref[sl * LANES:(sl + 1) * LANES, pl.ds(s0, tq)]
            vts = (jnp.concatenate([jnp.where(lo_v, vt, jnp.zeros_like(vt)), ones_r, zeros_r], axis=0),
                   jnp.concatenate([jnp.where(lo_v, jnp.zeros_like(vt), vt), zeros_r, ones_r], axis=0))
            pv, alphas = None, []
            for hh in range(2):
                h = sl * 2 + hh
                st = src_ref[h]
                if masked:
                    st = jnp.where(causal, st, NEG)
                m_old = m_ref[h]
                m_new = jnp.maximum(m_old, jnp.max(st, axis=0, keepdims=True))
                alpha = jnp.exp2(m_old - m_new)
                p = jnp.exp2(st - m_new)
                m_ref[h] = m_new
                part = jnp.dot(vts[hh], p.astype(BF16), preferred_element_type=F32)
                pv = part if pv is None else pv + part
                alphas.append(alpha)
            acc_ref[sl] = jnp.where(first_head_rows, alphas[0], alphas[1]) * acc_ref[sl] + pv

    scores(0, s0_ref)

    def pair(j, c):
        scores(2 * j + 1, s1_ref)
        consume(2 * j, s0_ref, False)
        scores(2 * j + 2, s0_ref)
        consume(2 * j + 1, s1_ref, False)
        return c

    lax.fori_loop(0, qi // 2, pair, 0)

    @pl.when(qi % 2 == 0)
    def _():
        consume(qi, s0_ref, True)

    @pl.when(qi % 2 == 1)
    def _():
        scores(qi, s1_ref)
        consume(qi - 1, s0_ref, False)
        consume(qi, s1_ref, True)

    for sl in range(n_slabs):
        acc = acc_ref[sl]
        l2 = jnp.where(lo_v, acc[LANES:LANES + 1, :], acc[LANES + DEN_ROWS:LANES + DEN_ROWS + 1, :])
        o_ref[:, sl * LANES:(sl + 1) * LANES] = (acc[:LANES, :] / l2).T.astype(BF16)


def _fox_attention(q, qaug, k, kaug, vt, bsz, seq, tq, hp):
    t, d = q.shape
    w = hp * HEAD_DIM
    n_grp = d // w
    nq = seq // tq
    body = functools.partial(_attn_body, tq=tq, hp=hp)
    qrow = lambda b, j, i: (b * nq + i, j)
    krow = lambda b, j, i: (b, j)
    return pl.pallas_call(
        body,
        grid=(bsz, n_grp, nq),
        in_specs=[pl.BlockSpec((tq, w), qrow),
                  pl.BlockSpec((tq, LANES), lambda b, j, i: (b * nq + i, 0)),
                  pl.BlockSpec((seq, w), krow),
                  pl.BlockSpec((seq, LANES), lambda b, j, i: (b, 0)),
                  pl.BlockSpec((w, seq), lambda b, j, i: (b * n_grp + j, 0))],
        out_specs=pl.BlockSpec((tq, w), qrow),
        out_shape=jax.ShapeDtypeStruct((t, d), BF16),
        scratch_shapes=[pltpu.VMEM((hp, 1, tq), F32),
                        pltpu.VMEM((hp // 2, LANES + 2 * DEN_ROWS, tq), F32),
                        pltpu.VMEM((hp, tq, tq), F32), pltpu.VMEM((hp, tq, tq), F32)],
        compiler_params=_params("parallel", "parallel", "arbitrary"),
        name="fox_attention",
    )(q, qaug, k, kaug, vt)


def _fox_out_body(o_ref, og_ref, h_ref, w_ref, gate_ref, out_ref):
    z = o_ref[...].astype(F32) * jax.nn.sigmoid(og_ref[...].astype(F32))
    out_ref[...] = h_ref[...] + gate_ref[0] * jnp.dot(z.astype(BF16), w_ref[...], preferred_element_type=F32)


def _fox_out(o, og, h2, w_o, gate, bsz, seq, tm):
    t, d = h2.shape
    nt = seq // tm
    row = lambda b, i: (b * nt + i, 0)
    return pl.pallas_call(
        _fox_out_body,
        grid=(bsz, nt),
        in_specs=[pl.BlockSpec((tm, d), row), pl.BlockSpec((tm, d), row), pl.BlockSpec((tm, d), row),
                  pl.BlockSpec((d, d), lambda b, i: (0, 0)),
                  pl.BlockSpec((1, 1, d), lambda b, i: (b, 0, 0))],
        out_specs=pl.BlockSpec((tm, d), row),
        out_shape=jax.ShapeDtypeStruct((t, d), F32),
        compiler_params=_params("parallel", "parallel"),
        name="fox_out",
    )(o, og, h2, w_o, gate)


def _tiles(seq):
    tm = min(512, seq)
    te = min(256, seq)
    tq = min(512, seq)
    tc = min(64, seq)
    return tm, te, tq, tc


def kernel(x, c, ln_g, ada_w, ada_b, s5_w_in, s5_lambda_re, s5_lambda_im, s5_log_dt, s5_b_re, s5_b_im,
           s5_c_re, s5_c_im, s5_d, s5_w_out, kv_g, kv_ada_w, kv_ada_b, kv_w, kv_fb, k_norm_g,
           fox_w_qg, fox_q_norm_g, fox_w_o, moe_wg, moe_bg, moe_we, moe_be, moe_w1, moe_w3, moe_w2):
    bsz, seq, d = x.shape
    depth = ln_g.shape[0]
    n_a = s5_w_in.shape[0]
    n_heads = d // HEAD_DIM
    tm, te, tq, tc = _tiles(seq)

    mods = _adaln(c, ada_w.reshape(depth * 2, d, 3 * d), ada_b.reshape(depth * 2, 1, 3 * d))
    mods = mods.reshape(depth, 2, bsz, 3, 1, d)
    kv_mods = _adaln(c, kv_ada_w[None], kv_ada_b[None, None]).reshape(bsz, 2, 1, d)

    h = x.reshape(bsz * seq, d)
    k = kaug = qaug = vt = None
    for l in range(depth):
        shift, scale, gate = mods[l, 0, :, 0], mods[l, 0, :, 1], mods[l, 0, :, 2]
        g = ln_g[l, 0][None]
        if l < n_a:
            u2 = _s5_in(h, g, shift, scale, s5_w_in[l].astype(BF16), bsz, seq, tm)
            bblk, cblk, a_re, a_im = _s5_tables(s5_lambda_re[l], s5_lambda_im[l], s5_log_dt[l],
                                                s5_b_re[l], s5_b_im[l], s5_c_re[l], s5_c_im[l])
            y2 = _s5_scan(u2, bblk, cblk, a_re, a_im, bsz, seq, tc)
            h = _s5_out(y2, u2, h, s5_d[l][None], s5_w_out[l].astype(BF16), gate, bsz, seq, tm)
        else:
            j = l - n_a
            qng = jnp.tile(fox_q_norm_g[j], n_heads)[None]
            q, og = _fox_qg(h, g, shift, scale, fox_w_qg[j][:, :d].astype(BF16),
                            fox_w_qg[j][:, d:].astype(BF16), qng, bsz, seq, tm)
            o = _fox_attention(q, qaug, k, kaug, vt, bsz, seq, tq, min(ATTN_HEADS_PER_STEP, n_heads))
            h = _fox_out(o, og, h, fox_w_o[j].astype(BF16), gate, bsz, seq, tm)

        shift, scale, gate = mods[l, 1, :, 0], mods[l, 1, :, 1], mods[l, 1, :, 2]
        h = _moe(h, ln_g[l, 1][None], shift, scale, gate, moe_wg[l], moe_bg[l], moe_we[l], moe_be[l],
                 moe_w1, moe_w3, moe_w2, l, bsz, seq, tm, te)

        if l == n_a - 1:
            wf = jnp.zeros((d, LANES), F32).at[:, :n_heads].set(kv_w[:, 2 * d:])
            fb = jnp.zeros((1, LANES), F32).at[0, :n_heads].set(kv_fb)
            kng = jnp.tile(k_norm_g, n_heads)[None]
            k, vt, kaug, qaug = _shared_kv(h, kv_g[None], kv_mods[:, 0], kv_mods[:, 1],
                                           kv_w[:, :d].astype(BF16), kv_w[:, d:2 * d].T.astype(BF16),
                                           wf, fb, kng, bsz, seq, tm)
    return h.reshape(bsz, seq, d)
```

```python
import functools
import math

import numpy as np
import jax
import jax.numpy as jnp
from jax import lax
from jax.experimental import pallas as pl
from jax.experimental.pallas import tpu as pltpu

F32 = jnp.float32
BF16 = jnp.bfloat16
I32 = jnp.int32

EPS = 1e-6
NEG = -1e30
LOG2E = math.log2(math.e)
LANES = 128
SUBLANES = 8
VMEM_LIMIT_BYTES = 56 * 1024 * 1024

S5_GROUPS_PER_BLOCK = 16
N_EXPERT_GROUPS = 4
EXPERTS_PER_GROUP = 8
N_EXPERTS = N_EXPERT_GROUPS * EXPERTS_PER_GROUP
N_BUCKETS = N_EXPERT_GROUPS * EXPERTS_PER_GROUP * EXPERTS_PER_GROUP
N_PAIR_BUCKETS = N_EXPERT_GROUPS * (EXPERTS_PER_GROUP * (EXPERTS_PER_GROUP - 1) // 2)
META_LANES = LANES
META_ROWS = SUBLANES
ROW_DMA_UNROLL = 8
PLAN_TILES_PER_STEP = 8
HEAD_DIM = 64
ATTN_HEADS_PER_STEP = 8
DEN_ROWS = 16
F_SPLIT = 3
AUG_LANES_PER_HEAD = 2 * F_SPLIT


def _params(*sem):
    return pltpu.CompilerParams(dimension_semantics=sem, vmem_limit_bytes=VMEM_LIMIT_BYTES)


def _rms_mod(x, g, shift, scale):
    ms = jnp.mean(x * x, axis=-1, keepdims=True)
    y = x * lax.rsqrt(ms + EPS) * g
    return y * (1.0 + scale) + shift


def _split_bf16(w):
    hi = w.astype(BF16)
    return hi, (w - hi.astype(F32)).astype(BF16)


def _dot_3pass(x, w_hi_ref, w_lo_ref):
    x_hi, x_lo = _split_bf16(x)
    w_hi = w_hi_ref[...]
    return (jnp.dot(x_hi, w_hi, preferred_element_type=F32)
            + jnp.dot(x_lo, w_hi, preferred_element_type=F32)
            + jnp.dot(x_hi, w_lo_ref[...], preferred_element_type=F32))


def _head_rms(x, g):
    tm, d = x.shape
    lane = lax.broadcasted_iota(I32, (tm, LANES), 1)
    lo = lane < HEAD_DIM
    outs = []
    for j in range(d // LANES):
        s = x[:, j * LANES:(j + 1) * LANES]
        sq = s * s
        s_lo = jnp.sum(jnp.where(lo, sq, 0.0), axis=-1, keepdims=True)
        s_hi = jnp.sum(jnp.where(lo, 0.0, sq), axis=-1, keepdims=True)
        r = jnp.where(lo, lax.rsqrt(s_lo / HEAD_DIM + EPS), lax.rsqrt(s_hi / HEAD_DIM + EPS))
        outs.append(s * r)
    return jnp.concatenate(outs, axis=-1) * g


def _adaln_body(c_ref, w_ref, b_ref, o_ref):
    c = c_ref[...]
    s = c * jax.nn.sigmoid(c)
    o_ref[0] = jnp.dot(s, w_ref[0], preferred_element_type=F32) + b_ref[0]


def _adaln(c, w, b):
    n_sets, d, n = w.shape
    bsz = c.shape[0]
    tn = 512 if n % 512 == 0 else n
    return pl.pallas_call(
        _adaln_body,
        grid=(n_sets, n // tn),
        in_specs=[pl.BlockSpec((bsz, d), lambda s, j: (0, 0)),
                  pl.BlockSpec((1, d, tn), lambda s, j: (s, 0, j)),
                  pl.BlockSpec((1, 1, tn), lambda s, j: (s, 0, j))],
        out_specs=pl.BlockSpec((1, bsz, tn), lambda s, j: (s, 0, j)),
        out_shape=jax.ShapeDtypeStruct((n_sets, bsz, n), F32),
        compiler_params=_params("parallel", "parallel"),
        name="adaln",
    )(c, w, b)


def _s5_in_body(x_ref, g_ref, sh_ref, sc_ref, w_ref, u_ref):
    hn = _rms_mod(x_ref[...], g_ref[...], sh_ref[0], sc_ref[0])
    u_ref[...] = jnp.dot(hn.astype(BF16), w_ref[...], preferred_element_type=F32)


def _s5_in(x2, g, shift, scale, w_in, bsz, seq, tm):
    d = x2.shape[1]
    nt = seq // tm
    row = lambda b, i: (b * nt + i, 0)
    return pl.pallas_call(
        _s5_in_body,
        grid=(bsz, nt),
        in_specs=[pl.BlockSpec((tm, d), row),
                  pl.BlockSpec((1, d), lambda b, i: (0, 0)),
                  pl.BlockSpec((1, 1, d), lambda b, i: (b, 0, 0)),
                  pl.BlockSpec((1, 1, d), lambda b, i: (b, 0, 0)),
                  pl.BlockSpec((d, d), lambda b, i: (0, 0))],
        out_specs=pl.BlockSpec((tm, d), row),
        out_shape=jax.ShapeDtypeStruct((bsz * seq, d), F32),
        compiler_params=_params("parallel", "parallel"),
        name="s5_in",
    )(x2, g, shift, scale, w_in)


def _s5_scan_body(u_hbm, bb_ref, cb_ref, are_ref, aim_ref, y_hbm,
                  ubuf, ybuf, bu_ref, st_ref, sem_in, sem_out, *, tc, nblk, sw, seq, n_chunks):
    bsz = SUBLANES
    cw = S5_GROUPS_PER_BLOCK * 16
    i = pl.program_id(0)
    slot = lax.rem(i, 2)

    def in_copy(chunk, sl, b):
        return pltpu.make_async_copy(u_hbm.at[pl.ds(b * seq + chunk * tc, tc)],
                                     ubuf.at[sl, :, b, :], sem_in.at[sl])

    def out_copy(chunk, sl, b):
        return pltpu.make_async_copy(ybuf.at[sl, :, b, :],
                                     y_hbm.at[pl.ds(b * seq + chunk * tc, tc)], sem_out.at[sl])

    @pl.when(i == 0)
    def _():
        st_ref[...] = jnp.zeros_like(st_ref)
        for b in range(bsz):
            in_copy(0, 0, b).start()

    @pl.when(i + 1 < n_chunks)
    def _():
        for b in range(bsz):
            in_copy(i + 1, 1 - slot, b).start()

    for b in range(bsz):
        in_copy(i, slot, b).wait()

    @pl.when(i >= 2)
    def _():
        for b in range(bsz):
            out_copy(i - 2, slot, b).wait()

    d = ubuf.shape[-1]
    u2 = ubuf[slot].reshape(tc * bsz, d).astype(BF16)
    for k in range(nblk):
        bu_ref[:, k * 2 * sw:(k + 1) * 2 * sw] = jnp.dot(
            u2[:, k * cw:(k + 1) * cw], bb_ref[k], preferred_element_type=F32)
        re0, im0 = k * 2 * sw, k * 2 * sw + sw
        a_re = jnp.broadcast_to(are_ref[k], (SUBLANES, sw))
        a_im = jnp.broadcast_to(aim_ref[k], (SUBLANES, sw))

        def step(t, carry, re0=re0, im0=im0, a_re=a_re, a_im=a_im):
            s_re, s_im = carry
            r0 = pl.multiple_of(t * SUBLANES, SUBLANES)
            n_re = a_re * s_re - a_im * s_im + bu_ref[pl.ds(r0, SUBLANES), re0:re0 + sw]
            n_im = a_re * s_im + a_im * s_re + bu_ref[pl.ds(r0, SUBLANES), im0:im0 + sw]
            bu_ref[pl.ds(r0, SUBLANES), re0:re0 + sw] = n_re
            bu_ref[pl.ds(r0, SUBLANES), im0:im0 + sw] = n_im
            return n_re, n_im

        s_re, s_im = lax.fori_loop(
            0, tc, step, (st_ref[:, re0:re0 + sw], st_ref[:, im0:im0 + sw]), unroll=True)
        st_ref[:, re0:re0 + sw] = s_re
        st_ref[:, im0:im0 + sw] = s_im
        s2 = bu_ref[:, k * 2 * sw:(k + 1) * 2 * sw].astype(BF16)
        yk = jnp.dot(s2, cb_ref[k], preferred_element_type=F32)
        ybuf[slot, :, :, k * cw:(k + 1) * cw] = yk.reshape(tc, bsz, cw)

    for b in range(bsz):
        out_copy(i, slot, b).start()

    @pl.when(i == n_chunks - 1)
    def _():
        if n_chunks >= 2:
            for b in range(bsz):
                out_copy(i - 1, 1 - slot, b).wait()
        for b in range(bsz):
            out_copy(i, slot, b).wait()


def _s5_scan(u2, bblk, cblk, a_re, a_im, bsz, seq, tc):
    assert bsz == SUBLANES, "the scan keeps the batch on the 8 sublanes of a vreg"
    rows, d = u2.shape
    nblk, cw, sw2 = bblk.shape
    sw = sw2 // 2
    n_chunks = seq // tc
    body = functools.partial(_s5_scan_body, tc=tc, nblk=nblk, sw=sw, seq=seq, n_chunks=n_chunks)
    return pl.pallas_call(
        body,
        grid=(n_chunks,),
        in_specs=[pl.BlockSpec(memory_space=pl.ANY),
                  pl.BlockSpec((nblk, cw, sw2), lambda i: (0, 0, 0)),
                  pl.BlockSpec((nblk, sw2, cw), lambda i: (0, 0, 0)),
                  pl.BlockSpec((nblk, 1, sw), lambda i: (0, 0, 0)),
                  pl.BlockSpec((nblk, 1, sw), lambda i: (0, 0, 0))],
        out_specs=pl.BlockSpec(memory_space=pl.ANY),
        out_shape=jax.ShapeDtypeStruct((rows, d), F32),
        scratch_shapes=[pltpu.VMEM((2, tc, bsz, d), F32), pltpu.VMEM((2, tc, bsz, d), F32),
                        pltpu.VMEM((tc * bsz, nblk * sw2), F32), pltpu.VMEM((bsz, nblk * sw2), F32),
                        pltpu.SemaphoreType.DMA((2,)), pltpu.SemaphoreType.DMA((2,))],
        compiler_params=_params("arbitrary"),
        name="s5_scan",
    )(u2, bblk, cblk, a_re, a_im)


def _s5_out_body(y_ref, u_ref, h_ref, d_ref, w_ref, gate_ref, o_ref):
    z = y_ref[...] + d_ref[...] * u_ref[...]
    act = jax.nn.gelu(z)
    vg = jnp.dot(act.astype(BF16), w_ref[...], preferred_element_type=F32)
    d = z.shape[-1]
    mix = vg[:, :d] * jax.nn.sigmoid(vg[:, d:])
    o_ref[...] = h_ref[...] + gate_ref[0] * mix


def _s5_out(y2, u2, h2, d_skip, w_out, gate, bsz, seq, tm):
    d = h2.shape[1]
    nt = seq // tm
    row = lambda b, i: (b * nt + i, 0)
    return pl.pallas_call(
        _s5_out_body,
        grid=(bsz, nt),
        in_specs=[pl.BlockSpec((tm, d), row),
                  pl.BlockSpec((tm, d), row),
                  pl.BlockSpec((tm, d), row),
                  pl.BlockSpec((1, d), lambda b, i: (0, 0)),
                  pl.BlockSpec((d, 2 * d), lambda b, i: (0, 0)),
                  pl.BlockSpec((1, 1, d), lambda b, i: (b, 0, 0))],
        out_specs=pl.BlockSpec((tm, d), row),
        out_shape=jax.ShapeDtypeStruct((bsz * seq, d), F32),
        compiler_params=_params("parallel", "parallel"),
        name="s5_out",
    )(y2, u2, h2, d_skip, w_out, gate)


def _s5_tables(lam_re, lam_im, log_dt, b_re, b_im, c_re, c_im):
    dt = jnp.exp(log_dt.astype(F32))[:, None]
    lr, li = lam_re.astype(F32), lam_im.astype(F32)
    mag = jnp.exp(lr * dt)
    a_re = mag * jnp.cos(li * dt)
    a_im = mag * jnp.sin(li * dt)
    den = lr * lr + li * li
    coef_re = ((a_re - 1.0) * lr + a_im * li) / den
    coef_im = (a_im * lr - (a_re - 1.0) * li) / den
    br_, bi_ = b_re.astype(F32), b_im.astype(F32)
    bbar_re = coef_re[..., None] * br_ - coef_im[..., None] * bi_
    bbar_im = coef_re[..., None] * bi_ + coef_im[..., None] * br_
    g, p, c = bbar_re.shape
    gb = S5_GROUPS_PER_BLOCK
    nblk = g // gb
    eye = jnp.eye(gb, dtype=F32)

    def in_blocks(m):
        return jnp.einsum('kgpc,gh->kgchp', m.reshape(nblk, gb, p, c), eye).reshape(nblk, gb * c, gb * p)

    def out_blocks(m):
        return jnp.einsum('kgcp,gh->kgphc', m.reshape(nblk, gb, c, p), eye).reshape(nblk, gb * p, gb * c)

    bblk = jnp.concatenate([in_blocks(bbar_re), in_blocks(bbar_im)], axis=-1).astype(BF16)
    cblk = jnp.concatenate([out_blocks(c_re.astype(F32)), -out_blocks(c_im.astype(F32))], axis=1).astype(BF16)
    return bblk, cblk, a_re.reshape(nblk, 1, gb * p), a_im.reshape(nblk, 1, gb * p)


def _router_body(h_ref, g_ref, sh_ref, sc_ref, wrh_ref, wrl_ref, br_ref, tri_ref,
                 x_ref, mt_ref, cnt_ref, carry_ref):
    tm, d = h_ref.shape
    ne, ng, epg = N_EXPERTS, N_EXPERT_GROUPS, EXPERTS_PER_GROUP

    @pl.when(pl.program_id(0) == 0)
    def _():
        carry_ref[...] = jnp.zeros_like(carry_ref)

    hn = _rms_mod(h_ref[...], g_ref[...], sh_ref[0], sc_ref[0])
    logits = _dot_3pass(hn, wrh_ref, wrl_ref) + br_ref[...]
    lane = lax.broadcasted_iota(I32, logits.shape, 1).astype(F32)
    big = jnp.float32(1e9)
    ninf = jnp.float32(-jnp.inf)

    gmask = (lane >= ne) & (lane < ne + ng)
    gmax = jnp.max(jnp.where(gmask, logits, ninf), axis=-1, keepdims=True)
    gsum = jnp.sum(jnp.where(gmask, jnp.exp(logits - gmax), 0.0), axis=-1, keepdims=True)
    p_g = 1.0 / gsum
    gidx = jnp.min(jnp.where(gmask & (logits == gmax), lane - ne, big), axis=-1, keepdims=True)

    emask = (lane < ne) & (jnp.floor(lane / epg) == gidx)
    v1 = jnp.max(jnp.where(emask, logits, ninf), axis=-1, keepdims=True)
    i1 = jnp.min(jnp.where(emask & (logits == v1), lane, big), axis=-1, keepdims=True)
    emask2 = emask & (lane != i1)
    v2 = jnp.max(jnp.where(emask2, logits, ninf), axis=-1, keepdims=True)
    i2 = jnp.min(jnp.where(emask2 & (logits == v2), lane, big), axis=-1, keepdims=True)
    e21 = jnp.exp(v2 - v1)
    w1 = p_g / (1.0 + e21)
    w2 = p_g * e21 / (1.0 + e21)

    first_lo = i1 < i2
    e_lo = jnp.where(first_lo, i1, i2)
    e_hi = jnp.where(first_lo, i2, i1)
    w_lo = jnp.where(first_lo, w1, w2)
    w_hi = jnp.where(first_lo, w2, w1)
    bucket = gidx * (epg * epg) + (e_lo - gidx * epg) * epg + (e_hi - gidx * epg)

    lane_b = lax.broadcasted_iota(I32, (tm, N_BUCKETS), 1).astype(F32)
    onehot = (lane_b == bucket).astype(F32)
    prefix = jnp.dot(tri_ref[...], onehot.astype(BF16), preferred_element_type=F32)
    carry = carry_ref[...]
    rank = jnp.sum(onehot * (prefix + carry), axis=-1, keepdims=True) - 1.0
    new_carry = carry + prefix[tm - 1:tm, :]
    carry_ref[...] = new_carry
    cnt_ref[...] = new_carry

    mlane = lax.broadcasted_iota(I32, (tm, META_LANES), 1)
    meta = jnp.where(mlane == 0, bucket,
           jnp.where(mlane == 1, rank,
           jnp.where(mlane == 2, w_lo,
           jnp.where(mlane == 3, w_hi, 0.0))))
    x_ref[:, :d] = hn
    x_ref[:, d:] = meta
    mt_ref[0] = meta.T[:META_ROWS, :]


def _router(h2, g, shift, scale, w_r, b_r, tri, nt_per_batch, tm):
    t, d = h2.shape
    wr_hi, wr_lo = _split_bf16(w_r)
    return pl.pallas_call(
        _router_body,
        grid=(t // tm,),
        in_specs=[pl.BlockSpec((tm, d), lambda i: (i, 0)),
                  pl.BlockSpec((1, d), lambda i: (0, 0)),
                  pl.BlockSpec((1, 1, d), lambda i: (i // nt_per_batch, 0, 0)),
                  pl.BlockSpec((1, 1, d), lambda i: (i // nt_per_batch, 0, 0)),
                  pl.BlockSpec((d, LANES), lambda i: (0, 0)),
                  pl.BlockSpec((d, LANES), lambda i: (0, 0)),
                  pl.BlockSpec((1, LANES), lambda i: (0, 0)),
                  pl.BlockSpec((tm, tm), lambda i: (0, 0))],
        out_specs=[pl.BlockSpec((tm, d + META_LANES), lambda i: (i, 0)),
                   pl.BlockSpec((1, META_ROWS, tm), lambda i: (i, 0, 0)),
                   pl.BlockSpec((1, N_BUCKETS), lambda i: (0, 0))],
        out_shape=[jax.ShapeDtypeStruct((t, d + META_LANES), F32),
                   jax.ShapeDtypeStruct((t // tm, META_ROWS, tm), F32),
                   jax.ShapeDtypeStruct((1, N_BUCKETS), F32)],
        scratch_shapes=[pltpu.VMEM((1, N_BUCKETS), F32)],
        compiler_params=_params("arbitrary"),
        name="moe_router",
    )(h2, g, shift, scale, wr_hi, wr_lo, b_r, tri)


def _plan_body(cnt_ref, mt_ref, pos_ref, maps_ref, start_ref, *, te, nwp):
    nb = N_BUCKETS
    epg = EXPERTS_PER_GROUP

    @pl.when(pl.program_id(0) == 0)
    def _():
        r = lax.broadcasted_iota(I32, (nb, nb), 0)
        c = lax.broadcasted_iota(I32, (nb, nb), 1)
        nt_dims = (((1,), (1,)), ((), ()))

        def column(mask, row_vals):
            row8 = jnp.broadcast_to(row_vals, (SUBLANES, nb)).astype(BF16)
            return lax.dot_general(mask.astype(BF16), row8, nt_dims, preferred_element_type=F32)[:, :1]

        cnt = cnt_ref[...]
        cnt_hi = jnp.floor(cnt / 256.0)
        cnt_lo = cnt - 256.0 * cnt_hi
        start = 256.0 * column(c < r, cnt_hi) + column(c < r, cnt_lo)
        count = 256.0 * column(c == r, cnt_hi) + column(c == r, cnt_lo)
        end = start + count
        start_ref[...] = start
        first_tile = jnp.floor(start / te)
        n_items = jnp.where(count > 0.0, jnp.floor((end - 1.0) / te) - first_tile + 1.0, 0.0)
        items8 = jnp.broadcast_to(n_items, (nb, LANES)).astype(BF16)
        item_end = jnp.dot((c <= r).astype(BF16), items8, preferred_element_type=F32)[:, :1]
        item_start = item_end - n_items
        n_total = item_end[nb - 1:nb, :]

        w = lax.broadcasted_iota(I32, (1, nwp), 1).astype(F32)
        wc = jnp.minimum(w, jnp.maximum(n_total - 1.0, 0.0))
        bucket = jnp.sum((item_end <= wc).astype(F32), axis=0, keepdims=True)
        sel = lax.broadcasted_iota(I32, (nb, nwp), 0).astype(F32) == bucket

        def pick(col):
            return jnp.sum(jnp.where(sel, col, 0.0), axis=0, keepdims=True)

        valid = (w < n_total).astype(F32)
        tile = pick(first_tile) + (wc - pick(item_start))
        row_lo = (jnp.maximum(pick(start), tile * te) - tile * te) * valid
        row_hi = (jnp.minimum(pick(end), (tile + 1.0) * te) - tile * te) * valid
        grp = jnp.floor(bucket / (epg * epg))
        within = bucket - grp * (epg * epg)
        lo = jnp.floor(within / epg)
        e_lo = grp * epg + lo
        e_hi = grp * epg + (within - lo * epg)
        row = lax.broadcasted_iota(I32, (SUBLANES, nwp), 0)
        maps = jnp.where(row == 0, tile, jnp.where(row == 1, e_lo, jnp.where(row == 2, e_hi,
               jnp.where(row == 3, valid, jnp.where(row == 4, row_lo, jnp.where(row == 5, row_hi, 0.0))))))
        maps_ref[...] = maps.astype(I32)

    n_sub, _, tm = mt_ref.shape
    rb = lax.broadcasted_iota(I32, (nb, tm), 0).astype(F32)
    for s in range(n_sub):
        bucket_row = mt_ref[s, 0:1, :]
        rank_row = mt_ref[s, 1:2, :]
        pos = jnp.sum(jnp.where(rb == bucket_row, start_ref[...], 0.0), axis=0, keepdims=True) + rank_row
        pos_ref[s] = pos.astype(I32)


def _plan(counts, meta_t, te, n_items_max):
    n_tt, _, tm = meta_t.shape
    nwp = ((n_items_max + LANES - 1) // LANES) * LANES
    body = functools.partial(_plan_body, te=te, nwp=nwp)
    n_sub = math.gcd(n_tt, PLAN_TILES_PER_STEP)
    return pl.pallas_call(
        body,
        grid=(n_tt // n_sub,),
        in_specs=[pl.BlockSpec((1, N_BUCKETS), lambda i: (0, 0)),
                  pl.BlockSpec((n_sub, META_ROWS, tm), lambda i: (i, 0, 0))],
        out_specs=[pl.BlockSpec((n_sub, 1, tm), lambda i: (i, 0, 0)),
                   pl.BlockSpec((SUBLANES, nwp), lambda i: (0, 0))],
        out_shape=[jax.ShapeDtypeStruct((n_tt, 1, tm), I32),
                   jax.ShapeDtypeStruct((SUBLANES, nwp), I32)],
        scratch_shapes=[pltpu.VMEM((N_BUCKETS, 1), F32)],
        compiler_params=_params("arbitrary"),
        name="moe_plan",
    )(counts, meta_t)


def _row_copy(src_ref, src_row, dst_ref, dst_row, sem):
    return pltpu.make_async_copy(src_ref.at[pl.ds(src_row, 1)], dst_ref.at[pl.ds(dst_row, 1)], sem)


def _dispatch_body(pos_ref, x_ref, xs_ref, sem):
    tm = x_ref.shape[0]

    def issue(grp, c):
        r0 = pl.multiple_of(grp * ROW_DMA_UNROLL, ROW_DMA_UNROLL)
        for u in range(ROW_DMA_UNROLL):
            _row_copy(x_ref, r0 + u, xs_ref, pos_ref[0, 0, r0 + u], sem).start(priority=u % 2)
        return c

    lax.fori_loop(0, tm // ROW_DMA_UNROLL, issue, 0, unroll=True)
    pltpu.make_async_copy(x_ref, xs_ref.at[pl.ds(0, tm)], sem).wait()


def _dispatch(xrow, pos3, tm):
    t, w = xrow.shape
    t_pad = t
    return pl.pallas_call(
        _dispatch_body,
        grid=(t // tm,),
        in_specs=[pl.BlockSpec((1, 1, tm), lambda i: (i, 0, 0), memory_space=pltpu.SMEM),
                  pl.BlockSpec((tm, w), lambda i: (i, 0))],
        out_specs=pl.BlockSpec(memory_space=pl.ANY),
        out_shape=jax.ShapeDtypeStruct((t_pad, w), F32),
        scratch_shapes=[pltpu.SemaphoreType.DMA(())],
        compiler_params=_params("arbitrary"),
        name="moe_dispatch",
    )(pos3, xrow)


def _expert_body(tile_ref, elo_ref, ehi_ref, valid_ref, rlo_ref, rhi_ref, x_ref,
                 w1_ref, w3_ref, w2_ref, o_ref, up_s, dn_s):
    del tile_ref
    te, d = o_ref.shape
    epg = EXPERTS_PER_GROUP
    j = pl.program_id(0)
    prev = jnp.maximum(j - 1, 0)

    @pl.when((j == 0) | (elo_ref[j] // epg != elo_ref[prev] // epg))
    def _():
        for e in range(epg):
            up_s[e, 0] = w1_ref[e].astype(BF16)
            up_s[e, 1] = w3_ref[e].astype(BF16)
            dn_s[e] = w2_ref[e].astype(BF16)

    @pl.when(valid_ref[j] == 1)
    def _():
        rows = lax.broadcasted_iota(I32, (te, 1), 0)
        live = (rows >= rlo_ref[j]) & (rows < rhi_ref[j])
        x = x_ref[:, :d].astype(BF16)

        def ffn(e, wt):
            a = jnp.dot(x, up_s[e, 0], preferred_element_type=F32)
            b = jnp.dot(x, up_s[e, 1], preferred_element_type=F32)
            mid = (a * jax.nn.sigmoid(a)) * b * wt
            return jnp.dot(mid.astype(BF16), dn_s[e], preferred_element_type=F32)

        res = (ffn(elo_ref[j] % epg, jnp.where(live, x_ref[:, d + 2:d + 3], 0.0))
               + ffn(ehi_ref[j] % epg, jnp.where(live, x_ref[:, d + 3:d + 4], 0.0)))

        @pl.when(rlo_ref[j] == 0)
        def _():
            o_ref[...] = res

        @pl.when(rlo_ref[j] != 0)
        def _():
            o_ref[...] += res


def _experts(xs, w1, w3, w2, layer, maps, n_items, te):
    t, w = xs.shape
    _, n_e, d, f = w1.shape
    epg = EXPERTS_PER_GROUP
    x_spec = pl.BlockSpec((te, w), lambda j, tl, lo, hi, v, a, b: (tl[j], 0))
    up = pl.BlockSpec((None, epg, d, f), lambda j, tl, lo, hi, v, a, b: (layer, lo[j] // epg, 0, 0),
                      pipeline_mode=pl.Buffered(1))
    dn = pl.BlockSpec((None, epg, f, d), lambda j, tl, lo, hi, v, a, b: (layer, lo[j] // epg, 0, 0),
                      pipeline_mode=pl.Buffered(1))
    return pl.pallas_call(
        _expert_body,
        grid_spec=pltpu.PrefetchScalarGridSpec(
            num_scalar_prefetch=6,
            grid=(n_items,),
            in_specs=[x_spec, up, up, dn],
            out_specs=pl.BlockSpec((te, d), lambda j, tl, lo, hi, v, a, b: (tl[j], 0)),
            scratch_shapes=[pltpu.VMEM((epg, 2, d, f), BF16), pltpu.VMEM((epg, f, d), BF16)]),
        out_shape=jax.ShapeDtypeStruct((t, d), F32),
        compiler_params=_params("arbitrary"),
        name="moe_experts",
    )(*[maps[i, :n_items] for i in range(6)], xs, w1, w3, w2)


def _combine_body(pos_ref, pos_next_ref, ys_ref, h_ref, gate_ref, o_ref, ybuf, sem, *, n_steps):
    tm = h_ref.shape[0]
    i = pl.program_id(0)
    slot = lax.rem(i, 2)

    def gather(p_ref, sl):
        def issue(grp, c):
            r0 = pl.multiple_of(grp * ROW_DMA_UNROLL, ROW_DMA_UNROLL)
            for u in range(ROW_DMA_UNROLL):
                _row_copy(ys_ref, p_ref[0, 0, r0 + u], ybuf.at[sl], r0 + u, sem.at[sl]).start(priority=u % 2)
            return c

        lax.fori_loop(0, tm // ROW_DMA_UNROLL, issue, 0, unroll=True)

    @pl.when(i == 0)
    def _():
        gather(pos_ref, 0)

    @pl.when(i + 1 < n_steps)
    def _():
        gather(pos_next_ref, 1 - slot)

    pltpu.make_async_copy(ys_ref.at[pl.ds(0, tm)], ybuf.at[slot], sem.at[slot]).wait()
    o_ref[...] = h_ref[...] + gate_ref[0] * ybuf[slot]


def _combine(ys, pos3, h2, gate, nt_per_batch, tm):
    t, d = h2.shape
    n_steps = t // tm
    body = functools.partial(_combine_body, n_steps=n_steps)
    return pl.pallas_call(
        body,
        grid=(n_steps,),
        in_specs=[pl.BlockSpec((1, 1, tm), lambda i: (i, 0, 0), memory_space=pltpu.SMEM),
                  pl.BlockSpec((1, 1, tm), lambda i: (jnp.minimum(i + 1, n_steps - 1), 0, 0),
                               memory_space=pltpu.SMEM),
                  pl.BlockSpec(memory_space=pl.ANY),
                  pl.BlockSpec((tm, d), lambda i: (i, 0)),
                  pl.BlockSpec((1, 1, d), lambda i: (i // nt_per_batch, 0, 0))],
        out_specs=pl.BlockSpec((tm, d), lambda i: (i, 0)),
        out_shape=jax.ShapeDtypeStruct((t, d), F32),
        scratch_shapes=[pltpu.VMEM((2, tm, d), F32), pltpu.SemaphoreType.DMA((2,))],
        compiler_params=_params("arbitrary"),
        name="moe_combine",
    )(pos3, pos3, ys, h2, gate)


def _moe(h2, g, shift, scale, gate, wg, bg, we, be, w1, w3, w2, layer, bsz, seq, tm, te):
    t, d = h2.shape
    nt_per_batch = seq // tm
    ne, ng = N_EXPERTS, N_EXPERT_GROUPS
    w_r = jnp.zeros((d, LANES), F32).at[:, :ne].set(we).at[:, ne:ne + ng].set(wg)
    b_r = jnp.zeros((1, LANES), F32).at[0, :ne].set(be).at[0, ne:ne + ng].set(bg)
    tri = jnp.asarray(np.tril(np.ones((tm, tm), np.float32)), BF16)
    xrow, meta_t, counts = _router(h2, g, shift, scale, w_r, b_r, tri, nt_per_batch, tm)
    n_items_max = t // te + N_PAIR_BUCKETS
    pos3, maps = _plan(counts, meta_t, te, n_items_max)
    xs = _dispatch(xrow, pos3, tm)
    ys = _experts(xs, w1, w3, w2, layer, maps, n_items_max, te)
    return _combine(ys, pos3, h2, gate, nt_per_batch, tm)


def _log_sigmoid(x):
    return jnp.minimum(x, 0.0) - jnp.log1p(jnp.exp(-jnp.abs(x)))


def _aug_tables(n_heads):
    assert n_heads * AUG_LANES_PER_HEAD <= LANES
    width = LANES
    pk = np.zeros((F_SPLIT * LANES, width), np.float32)
    pq = np.zeros((F_SPLIT * LANES, width), np.float32)
    ck = np.zeros((1, width), np.float32)
    cq = np.zeros((1, width), np.float32)
    for h in range(n_heads):
        base = h * AUG_LANES_PER_HEAD
        for j in range(F_SPLIT):
            pk[j * LANES + h, base + j] = -1.0
            pq[j * LANES + h, base + F_SPLIT + j] = 1.0
            ck[0, base + F_SPLIT + j] = 1.0
            cq[0, base + j] = 1.0
    return jnp.asarray(pk, BF16), jnp.asarray(pq, BF16), jnp.asarray(ck), jnp.asarray(cq)


def _kv_body(h_ref, g_ref, sh_ref, sc_ref, wk_ref, wvt_ref, wfh_ref, wfl_ref, fb_ref, kng_ref,
             pk_ref, pq_ref, ck_ref, cq_ref, k_ref, vt_ref, ka_ref, qa_ref, carry_ref):
    tm = h_ref.shape[0]

    @pl.when(pl.program_id(1) == 0)
    def _():
        carry_ref[...] = jnp.zeros_like(carry_ref)

    hn = _rms_mod(h_ref[...], g_ref[...], sh_ref[0], sc_ref[0])
    hb = hn.astype(BF16)
    k = jnp.dot(hb, wk_ref[...], preferred_element_type=F32)
    k_ref[...] = _head_rms(k, kng_ref[...]).astype(BF16)
    nt_dims = (((1,), (1,)), ((), ()))
    vt_ref[...] = lax.dot_general(wvt_ref[...], hb, nt_dims, preferred_element_type=F32).astype(BF16)
    fz = _dot_3pass(hn, wfh_ref, wfl_ref) + fb_ref[...]
    c = _log_sigmoid(fz)
    row = lax.broadcasted_iota(I32, c.shape, 0)
    shift = 1
    while shift < tm:
        c = c + jnp.where(row >= shift, pltpu.roll(c, shift, 0), 0.0)
        shift *= 2
    f = c + carry_ref[...]
    carry_ref[...] = f[tm - 1:tm, :]

    f2 = f * LOG2E
    hi = f2.astype(BF16)
    r1 = f2 - hi.astype(F32)
    mid = r1.astype(BF16)
    lo = (r1 - mid.astype(F32)).astype(BF16)
    pieces = jnp.concatenate([hi, mid, lo], axis=-1)
    ka_ref[...] = (jnp.dot(pieces, pk_ref[...], preferred_element_type=F32) + ck_ref[...]).astype(BF16)
    qa_ref[...] = (jnp.dot(pieces, pq_ref[...], preferred_element_type=F32) + cq_ref[...]).astype(BF16)


def _shared_kv(h2, g, shift, scale, wk, wvt, wf, fb, kng, bsz, seq, tm):
    t, d = h2.shape
    nt = seq // tm
    aw = LANES
    pk, pq, ck, cq = _aug_tables(d // HEAD_DIM)
    wf_hi, wf_lo = _split_bf16(wf)
    row = lambda b, i: (b * nt + i, 0)
    const = lambda b, i: (0, 0)
    return pl.pallas_call(
        _kv_body,
        grid=(bsz, nt),
        in_specs=[pl.BlockSpec((tm, d), row),
                  pl.BlockSpec((1, d), const),
                  pl.BlockSpec((1, 1, d), lambda b, i: (b, 0, 0)),
                  pl.BlockSpec((1, 1, d), lambda b, i: (b, 0, 0)),
                  pl.BlockSpec((d, d), const),
                  pl.BlockSpec((d, d), const),
                  pl.BlockSpec((d, LANES), const),
                  pl.BlockSpec((d, LANES), const),
                  pl.BlockSpec((1, LANES), const),
                  pl.BlockSpec((1, d), const),
                  pl.BlockSpec((F_SPLIT * LANES, aw), const),
                  pl.BlockSpec((F_SPLIT * LANES, aw), const),
                  pl.BlockSpec((1, aw), const),
                  pl.BlockSpec((1, aw), const)],
        out_specs=[pl.BlockSpec((tm, d), row),
                   pl.BlockSpec((d, tm), lambda b, i: (b, i)),
                   pl.BlockSpec((tm, aw), row),
                   pl.BlockSpec((tm, aw), row)],
        out_shape=[jax.ShapeDtypeStruct((t, d), BF16), jax.ShapeDtypeStruct((bsz * d, seq), BF16),
                   jax.ShapeDtypeStruct((t, aw), BF16), jax.ShapeDtypeStruct((t, aw), BF16)],
        scratch_shapes=[pltpu.VMEM((1, LANES), F32)],
        compiler_params=_params("parallel", "arbitrary"),
        name="shared_kv",
    )(h2, g, shift, scale, wk, wvt, wf_hi, wf_lo, fb, kng, pk, pq, ck, cq)


def _qg_body(h_ref, g_ref, sh_ref, sc_ref, wq_ref, wg_ref, qng_ref, q_ref, og_ref):
    hn = _rms_mod(h_ref[...], g_ref[...], sh_ref[0], sc_ref[0])
    hb = hn.astype(BF16)
    q = jnp.dot(hb, wq_ref[...], preferred_element_type=F32)
    q_ref[...] = (_head_rms(q, qng_ref[...]) * (HEAD_DIM ** -0.5 * LOG2E)).astype(BF16)
    og_ref[...] = jnp.dot(hb, wg_ref[...], preferred_element_type=F32).astype(BF16)


def _fox_qg(h2, g, shift, scale, wq, wg, qng, bsz, seq, tm):
    t, d = h2.shape
    nt = seq // tm
    row = lambda b, i: (b * nt + i, 0)
    const = lambda b, i: (0, 0)
    return pl.pallas_call(
        _qg_body,
        grid=(bsz, nt),
        in_specs=[pl.BlockSpec((tm, d), row),
                  pl.BlockSpec((1, d), const),
                  pl.BlockSpec((1, 1, d), lambda b, i: (b, 0, 0)),
                  pl.BlockSpec((1, 1, d), lambda b, i: (b, 0, 0)),
                  pl.BlockSpec((d, d), const),
                  pl.BlockSpec((d, d), const),
                  pl.BlockSpec((1, d), const)],
        out_specs=[pl.BlockSpec((tm, d), row), pl.BlockSpec((tm, d), row)],
        out_shape=[jax.ShapeDtypeStruct((t, d), BF16), jax.ShapeDtypeStruct((t, d), BF16)],
        compiler_params=_params("parallel", "parallel"),
        name="fox_qg",
    )(h2, g, shift, scale, wq, wg, qng)


def _attn_body(q_ref, qa_ref, k_ref, ka_ref, vt_ref, o_ref, m_ref, acc_ref, s0_ref, s1_ref, *, tq, hp):
    grp = pl.program_id(1)
    qi = pl.program_id(2)
    n_slabs = hp // 2
    lane_q = lax.broadcasted_iota(I32, (tq, LANES), 1)
    lo_q = lane_q < HEAD_DIM
    lo_v = lax.broadcasted_iota(I32, (LANES, tq), 0) < HEAD_DIM
    nt_dims = (((1,), (1,)), ((), ()))
    causal = lax.broadcasted_iota(I32, (tq, tq), 0) <= lax.broadcasted_iota(I32, (tq, tq), 1)

    qa = qa_ref[...]
    qcats = []
    for h in range(hp):
        sl, hh = h // 2, h % 2
        q2 = q_ref[:, sl * LANES:(sl + 1) * LANES]
        own = lo_q if hh == 0 else jnp.logical_not(lo_q)
        a0 = (grp * hp + h) * AUG_LANES_PER_HEAD
        own_a = (lane_q >= a0) & (lane_q < a0 + AUG_LANES_PER_HEAD)
        qcats.append(jnp.concatenate([jnp.where(own, q2, jnp.zeros_like(q2)),
                                      jnp.where(own_a, qa, jnp.zeros_like(qa))], axis=-1))

    m_ref[...] = jnp.full(m_ref.shape, NEG, F32)
    ones_r = jnp.ones((DEN_ROWS, tq), BF16)
    zeros_r = jnp.zeros((DEN_ROWS, tq), BF16)
    acc_row = lax.broadcasted_iota(I32, (LANES + 2 * DEN_ROWS, tq), 0)
    first_head_rows = (acc_row < HEAD_DIM) | ((acc_row >= LANES) & (acc_row < LANES + DEN_ROWS))
    acc_ref[...] = jnp.zeros(acc_ref.shape, F32)

    def scores(kb, dst_ref):
        s0 = pl.multiple_of(kb * tq, tq)
        for h in range(hp):
            sl = h // 2
            kcat = jnp.concatenate([k_ref[pl.ds(s0, tq), sl * LANES:(sl + 1) * LANES],
                                    ka_ref[pl.ds(s0, tq), :]], axis=-1)
            dst_ref[h] = lax.dot_general(kcat, qcats[h], nt_dims, preferred_element_type=F32)

    def consume(kb, src_ref, masked):
        s0 = pl.multiple_of(kb * tq, tq)
        for sl in range(n_slabs):
            vt = vt_ref[sl * LANES:(sl + 1) * LANES, pl.ds(s0, tq)]
            vts = (jnp.concatenate([jnp.where(lo_v, vt, jnp.zeros_like(vt)), ones_r, zeros_r], axis=0),
                   jnp.concatenate([jnp.where(lo_v, jnp.zeros_like(vt), vt), zeros_r, ones_r], axis=0))
            pv, alphas = None, []
            for hh in range(2):
                h = sl * 2 + hh
                st = src_ref[h]
                if masked:
                    st = jnp.where(causal, st, NEG)
                m_old = m_ref[h]
                m_new = jnp.maximum(m_old, jnp.max(st, axis=0, keepdims=True))
                alpha = jnp.exp2(m_old - m_new)
                p = jnp.exp2(st - m_new)
                m_ref[h] = m_new
                part = jnp.dot(vts[hh], p.astype(BF16), preferred_element_type=F32)
                pv = part if pv is None else pv + part
                alphas.append(alpha)
            acc_ref[sl] = jnp.where(first_head_rows, alphas[0], alphas[1]) * acc_ref[sl] + pv

    scores(0, s0_ref)

    def pair(j, c):
        scores(2 * j + 1, s1_ref)
        consume(2 * j, s0_ref, False)
        scores(2 * j + 2, s0_ref)
        consume(2 * j + 1, s1_ref, False)
        return c

    lax.fori_loop(0, qi // 2, pair, 0)

    @pl.when(qi % 2 == 0)
    def _():
        consume(qi, s0_ref, True)

    @pl.when(qi % 2 == 1)
    def _():
        scores(qi, s1_ref)
        consume(qi - 1, s0_ref, False)
        consume(qi, s1_ref, True)

    for sl in range(n_slabs):
        acc = acc_ref[sl]
        l2 = jnp.where(lo_v, acc[LANES:LANES + 1, :], acc[LANES + DEN_ROWS:LANES + DEN_ROWS + 1, :])
        o_ref[:, sl * LANES:(sl + 1) * LANES] = (acc[:LANES, :] / l2).T.astype(BF16)


def _fox_attention(q, qaug, k, kaug, vt, bsz, seq, tq, hp):
    t, d = q.shape
    w = hp * HEAD_DIM
    n_grp = d // w
    nq = seq // tq
    body = functools.partial(_attn_body, tq=tq, hp=hp)
    qrow = lambda b, j, i: (b * nq + i, j)
    krow = lambda b, j, i: (b, j)
    return pl.pallas_call(
        body,
        grid=(bsz, n_grp, nq),
        in_specs=[pl.BlockSpec((tq, w), qrow),
                  pl.BlockSpec((tq, LANES), lambda b, j, i: (b * nq + i, 0)),
                  pl.BlockSpec((seq, w), krow),
                  pl.BlockSpec((seq, LANES), lambda b, j, i: (b, 0)),
                  pl.BlockSpec((w, seq), lambda b, j, i: (b * n_grp + j, 0))],
        out_specs=pl.BlockSpec((tq, w), qrow),
        out_shape=jax.ShapeDtypeStruct((t, d), BF16),
        scratch_shapes=[pltpu.VMEM((hp, 1, tq), F32),
                        pltpu.VMEM((hp // 2, LANES + 2 * DEN_ROWS, tq), F32),
                        pltpu.VMEM((hp, tq, tq), F32), pltpu.VMEM((hp, tq, tq), F32)],
        compiler_params=_params("parallel", "parallel", "arbitrary"),
        name="fox_attention",
    )(q, qaug, k, kaug, vt)


def _fox_out_body(o_ref, og_ref, h_ref, w_ref, gate_ref, out_ref):
    z = o_ref[...].astype(F32) * jax.nn.sigmoid(og_ref[...].astype(F32))
    out_ref[...] = h_ref[...] + gate_ref[0] * jnp.dot(z.astype(BF16), w_ref[...], preferred_element_type=F32)


def _fox_out(o, og, h2, w_o, gate, bsz, seq, tm):
    t, d = h2.shape
    nt = seq // tm
    row = lambda b, i: (b * nt + i, 0)
    return pl.pallas_call(
        _fox_out_body,
        grid=(bsz, nt),
        in_specs=[pl.BlockSpec((tm, d), row), pl.BlockSpec((tm, d), row), pl.BlockSpec((tm, d), row),
                  pl.BlockSpec((d, d), lambda b, i: (0, 0)),
                  pl.BlockSpec((1, 1, d), lambda b, i: (b, 0, 0))],
        out_specs=pl.BlockSpec((tm, d), row),
        out_shape=jax.ShapeDtypeStruct((t, d), F32),
        compiler_params=_params("parallel", "parallel"),
        name="fox_out",
    )(o, og, h2, w_o, gate)


def _tiles(seq):
    tm = min(512, seq)
    te = min(256, seq)
    tq = min(512, seq)
    tc = min(64, seq)
    return tm, te, tq, tc


def kernel(x, c, ln_g, ada_w, ada_b, s5_w_in, s5_lambda_re, s5_lambda_im, s5_log_dt, s5_b_re, s5_b_im,
           s5_c_re, s5_c_im, s5_d, s5_w_out, kv_g, kv_ada_w, kv_ada_b, kv_w, kv_fb, k_norm_g,
           fox_w_qg, fox_q_norm_g, fox_w_o, moe_wg, moe_bg, moe_we, moe_be, moe_w1, moe_w3, moe_w2):
    bsz, seq, d = x.shape
    depth = ln_g.shape[0]
    n_a = s5_w_in.shape[0]
    n_heads = d // HEAD_DIM
    tm, te, tq, tc = _tiles(seq)

    mods = _adaln(c, ada_w.reshape(depth * 2, d, 3 * d), ada_b.reshape(depth * 2, 1, 3 * d))
    mods = mods.reshape(depth, 2, bsz, 3, 1, d)
    kv_mods = _adaln(c, kv_ada_w[None], kv_ada_b[None, None]).reshape(bsz, 2, 1, d)

    h = x.reshape(bsz * seq, d)
    k = kaug = qaug = vt = None
    for l in range(depth):
        shift, scale, gate = mods[l, 0, :, 0], mods[l, 0, :, 1], mods[l, 0, :, 2]
        g = ln_g[l, 0][None]
        if l < n_a:
            u2 = _s5_in(h, g, shift, scale, s5_w_in[l].astype(BF16), bsz, seq, tm)
            bblk, cblk, a_re, a_im = _s5_tables(s5_lambda_re[l], s5_lambda_im[l], s5_log_dt[l],
                                                s5_b_re[l], s5_b_im[l], s5_c_re[l], s5_c_im[l])
            y2 = _s5_scan(u2, bblk, cblk, a_re, a_im, bsz, seq, tc)
            h = _s5_out(y2, u2, h, s5_d[l][None], s5_w_out[l].astype(BF16), gate, bsz, seq, tm)
        else:
            j = l - n_a
            qng = jnp.tile(fox_q_norm_g[j], n_heads)[None]
            q, og = _fox_qg(h, g, shift, scale, fox_w_qg[j][:, :d].astype(BF16),
                            fox_w_qg[j][:, d:].astype(BF16), qng, bsz, seq, tm)
            o = _fox_attention(q, qaug, k, kaug, vt, bsz, seq, tq, min(ATTN_HEADS_PER_STEP, n_heads))
            h = _fox_out(o, og, h, fox_w_o[j].astype(BF16), gate, bsz, seq, tm)

        shift, scale, gate = mods[l, 1, :, 0], mods[l, 1, :, 1], mods[l, 1, :, 2]
        h = _moe(h, ln_g[l, 1][None], shift, scale, gate, moe_wg[l], moe_bg[l], moe_we[l], moe_be[l],
                 moe_w1, moe_w3, moe_w2, l, bsz, seq, tm, te)

        if l == n_a - 1:
            wf = jnp.zeros((d, LANES), F32).at[:, :n_heads].set(kv_w[:, 2 * d:])
            fb = jnp.zeros((1, LANES), F32).at[0, :n_heads].set(kv_fb)
            kng = jnp.tile(k_norm_g, n_heads)[None]
            k, vt, kaug, qaug = _shared_kv(h, kv_g[None], kv_mods[:, 0], kv_mods[:, 1],
                                           kv_w[:, :d].astype(BF16), kv_w[:, d:2 * d].T.astype(BF16),
                                           wf, fb, kng, bsz, seq, tm)
    return h.reshape(bsz, seq, d)
```

```python
import functools
import math

import numpy as np
import jax
import jax.numpy as jnp
from jax import lax
from jax.experimental import pallas as pl
from jax.experimental.pallas import tpu as pltpu

F32 = jnp.float32
BF16 = jnp.bfloat16
I32 = jnp.int32

EPS = 1e-6
NEG = -1e30
LOG2E = math.log2(math.e)
LANES = 128
SUBLANES = 8
VMEM_LIMIT_BYTES = 56 * 1024 * 1024

S5_GROUPS_PER_BLOCK = 16
N_EXPERT_GROUPS = 4
EXPERTS_PER_GROUP = 8
N_EXPERTS = N_EXPERT_GROUPS * EXPERTS_PER_GROUP
N_BUCKETS = N_EXPERT_GROUPS * EXPERTS_PER_GROUP * EXPERTS_PER_GROUP
N_PAIR_BUCKETS = N_EXPERT_GROUPS * (EXPERTS_PER_GROUP * (EXPERTS_PER_GROUP - 1) // 2)
META_LANES = LANES
META_ROWS = SUBLANES
ROW_DMA_UNROLL = 8
PLAN_TILES_PER_STEP = 8
HEAD_DIM = 64
ATTN_HEADS_PER_STEP = 8
DEN_ROWS = 16
F_SPLIT = 3
AUG_LANES_PER_HEAD = 2 * F_SPLIT


def _params(*sem):
    return pltpu.CompilerParams(dimension_semantics=sem, vmem_limit_bytes=VMEM_LIMIT_BYTES)


def _rms_mod(x, g, shift, scale):
    ms = jnp.mean(x * x, axis=-1, keepdims=True)
    y = x * lax.rsqrt(ms + EPS) * g
    return y * (1.0 + scale) + shift


def _split_bf16(w):
    hi = w.astype(BF16)
    return hi, (w - hi.astype(F32)).astype(BF16)


def _dot_3pass(x, w_hi_ref, w_lo_ref):
    x_hi, x_lo = _split_bf16(x)
    w_hi = w_hi_ref[...]
    return (jnp.dot(x_hi, w_hi, preferred_element_type=F32)
            + jnp.dot(x_lo, w_hi, preferred_element_type=F32)
            + jnp.dot(x_hi, w_lo_ref[...], preferred_element_type=F32))


def _head_rms(x, g):
    tm, d = x.shape
    lane = lax.broadcasted_iota(I32, (tm, LANES), 1)
    lo = lane < HEAD_DIM
    outs = []
    for j in range(d // LANES):
        s = x[:, j * LANES:(j + 1) * LANES]
        sq = s * s
        s_lo = jnp.sum(jnp.where(lo, sq, 0.0), axis=-1, keepdims=True)
        s_hi = jnp.sum(jnp.where(lo, 0.0, sq), axis=-1, keepdims=True)
        r = jnp.where(lo, lax.rsqrt(s_lo / HEAD_DIM + EPS), lax.rsqrt(s_hi / HEAD_DIM + EPS))
        outs.append(s * r)
    return jnp.concatenate(outs, axis=-1) * g


def _adaln_body(c_ref, w_ref, b_ref, o_ref):
    c = c_ref[...]
    s = c * jax.nn.sigmoid(c)
    o_ref[0] = jnp.dot(s, w_ref[0], preferred_element_type=F32) + b_ref[0]


def _adaln(c, w, b):
    n_sets, d, n = w.shape
    bsz = c.shape[0]
    tn = 512 if n % 512 == 0 else n
    return pl.pallas_call(
        _adaln_body,
        grid=(n_sets, n // tn),
        in_specs=[pl.BlockSpec((bsz, d), lambda s, j: (0, 0)),
                  pl.BlockSpec((1, d, tn), lambda s, j: (s, 0, j)),
                  pl.BlockSpec((1, 1, tn), lambda s, j: (s, 0, j))],
        out_specs=pl.BlockSpec((1, bsz, tn), lambda s, j: (s, 0, j)),
        out_shape=jax.ShapeDtypeStruct((n_sets, bsz, n), F32),
        compiler_params=_params("parallel", "parallel"),
        name="adaln",
    )(c, w, b)


def _s5_in_body(x_ref, g_ref, sh_ref, sc_ref, w_ref, u_ref):
    hn = _rms_mod(x_ref[...], g_ref[...], sh_ref[0], sc_ref[0])
    u_ref[...] = jnp.dot(hn.astype(BF16), w_ref[...], preferred_element_type=F32)


def _s5_in(x2, g, shift, scale, w_in, bsz, seq, tm):
    d = x2.shape[1]
    nt = seq // tm
    row = lambda b, i: (b * nt + i, 0)
    return pl.pallas_call(
        _s5_in_body,
        grid=(bsz, nt),
        in_specs=[pl.BlockSpec((tm, d), row),
                  pl.BlockSpec((1, d), lambda b, i: (0, 0)),
                  pl.BlockSpec((1, 1, d), lambda b, i: (b, 0, 0)),
                  pl.BlockSpec((1, 1, d), lambda b, i: (b, 0, 0)),
                  pl.BlockSpec((d, d), lambda b, i: (0, 0))],
        out_specs=pl.BlockSpec((tm, d), row),
        out_shape=jax.ShapeDtypeStruct((bsz * seq, d), F32),
        compiler_params=_params("parallel", "parallel"),
        name="s5_in",
    )(x2, g, shift, scale, w_in)


def _s5_scan_body(u_hbm, bb_ref, cb_ref, are_ref, aim_ref, y_hbm,
                  ubuf, ybuf, bu_ref, st_ref, sem_in, sem_out, *, tc, nblk, sw, seq, n_chunks):
    bsz = SUBLANES
    cw = S5_GROUPS_PER_BLOCK * 16
    i = pl.program_id(0)
    slot = lax.rem(i, 2)

    def in_copy(chunk, sl, b):
        return pltpu.make_async_copy(u_hbm.at[pl.ds(b * seq + chunk * tc, tc)],
                                     ubuf.at[sl, :, b, :], sem_in.at[sl])

    def out_copy(chunk, sl, b):
        return pltpu.make_async_copy(ybuf.at[sl, :, b, :],
                                     y_hbm.at[pl.ds(b * seq + chunk * tc, tc)], sem_out.at[sl])

    @pl.when(i == 0)
    def _():
        st_ref[...] = jnp.zeros_like(st_ref)
        for b in range(bsz):
            in_copy(0, 0, b).start()

    @pl.when(i + 1 < n_chunks)
    def _():
        for b in range(bsz):
            in_copy(i + 1, 1 - slot, b).start()

    for b in range(bsz):
        in_copy(i, slot, b).wait()

    @pl.when(i >= 2)
    def _():
        for b in range(bsz):
            out_copy(i - 2, slot, b).wait()

    d = ubuf.shape[-1]
    u2 = ubuf[slot].reshape(tc * bsz, d).astype(BF16)
    for k in range(nblk):
        bu_ref[:, k * 2 * sw:(k + 1) * 2 * sw] = jnp.dot(
            u2[:, k * cw:(k + 1) * cw], bb_ref[k], preferred_element_type=F32)
        re0, im0 = k * 2 * sw, k * 2 * sw + sw
        a_re = jnp.broadcast_to(are_ref[k], (SUBLANES, sw))
        a_im = jnp.broadcast_to(aim_ref[k], (SUBLANES, sw))

        def step(t, carry, re0=re0, im0=im0, a_re=a_re, a_im=a_im):
            s_re, s_im = carry
            r0 = pl.multiple_of(t * SUBLANES, SUBLANES)
            n_re = a_re * s_re - a_im * s_im + bu_ref[pl.ds(r0, SUBLANES), re0:re0 + sw]
            n_im = a_re * s_im + a_im * s_re + bu_ref[pl.ds(r0, SUBLANES), im0:im0 + sw]
            bu_ref[pl.ds(r0, SUBLANES), re0:re0 + sw] = n_re
            bu_ref[pl.ds(r0, SUBLANES), im0:im0 + sw] = n_im
            return n_re, n_im

        s_re, s_im = lax.fori_loop(
            0, tc, step, (st_ref[:, re0:re0 + sw], st_ref[:, im0:im0 + sw]), unroll=True)
        st_ref[:, re0:re0 + sw] = s_re
        st_ref[:, im0:im0 + sw] = s_im
        s2 = bu_ref[:, k * 2 * sw:(k + 1) * 2 * sw].astype(BF16)
        yk = jnp.dot(s2, cb_ref[k], preferred_element_type=F32)
        ybuf[slot, :, :, k * cw:(k + 1) * cw] = yk.reshape(tc, bsz, cw)

    for b in range(bsz):
        out_copy(i, slot, b).start()

    @pl.when(i == n_chunks - 1)
    def _():
        if n_chunks >= 2:
            for b in range(bsz):
                out_copy(i - 1, 1 - slot, b).wait()
        for b in range(bsz):
            out_copy(i, slot, b).wait()


def _s5_scan(u2, bblk, cblk, a_re, a_im, bsz, seq, tc):
    assert bsz == SUBLANES, "the scan keeps the batch on the 8 sublanes of a vreg"
    rows, d = u2.shape
    nblk, cw, sw2 = bblk.shape
    sw = sw2 // 2
    n_chunks = seq // tc
    body = functools.partial(_s5_scan_body, tc=tc, nblk=nblk, sw=sw, seq=seq, n_chunks=n_chunks)
    return pl.pallas_call(
        body,
        grid=(n_chunks,),
        in_specs=[pl.BlockSpec(memory_space=pl.ANY),
                  pl.BlockSpec((nblk, cw, sw2), lambda i: (0, 0, 0)),
                  pl.BlockSpec((nblk, sw2, cw), lambda i: (0, 0, 0)),
                  pl.BlockSpec((nblk, 1, sw), lambda i: (0, 0, 0)),
                  pl.BlockSpec((nblk, 1, sw), lambda i: (0, 0, 0))],
        out_specs=pl.BlockSpec(memory_space=pl.ANY),
        out_shape=jax.ShapeDtypeStruct((rows, d), F32),
        scratch_shapes=[pltpu.VMEM((2, tc, bsz, d), F32), pltpu.VMEM((2, tc, bsz, d), F32),
                        pltpu.VMEM((tc * bsz, nblk * sw2), F32), pltpu.VMEM((bsz, nblk * sw2), F32),
                        pltpu.SemaphoreType.DMA((2,)), pltpu.SemaphoreType.DMA((2,))],
        compiler_params=_params("arbitrary"),
        name="s5_scan",
    )(u2, bblk, cblk, a_re, a_im)


def _s5_out_body(y_ref, u_ref, h_ref, d_ref, w_ref, gate_ref, o_ref):
    z = y_ref[...] + d_ref[...] * u_ref[...]
    act = jax.nn.gelu(z)
    vg = jnp.dot(act.astype(BF16), w_ref[...], preferred_element_type=F32)
    d = z.shape[-1]
    mix = vg[:, :d] * jax.nn.sigmoid(vg[:, d:])
    o_ref[...] = h_ref[...] + gate_ref[0] * mix


def _s5_out(y2, u2, h2, d_skip, w_out, gate, bsz, seq, tm):
    d = h2.shape[1]
    nt = seq // tm
    row = lambda b, i: (b * nt + i, 0)
    return pl.pallas_call(
        _s5_out_body,
        grid=(bsz, nt),
        in_specs=[pl.BlockSpec((tm, d), row),
                  pl.BlockSpec((tm, d), row),
                  pl.BlockSpec((tm, d), row),
                  pl.BlockSpec((1, d), lambda b, i: (0, 0)),
                  pl.BlockSpec((d, 2 * d), lambda b, i: (0, 0)),
                  pl.BlockSpec((1, 1, d), lambda b, i: (b, 0, 0))],
        out_specs=pl.BlockSpec((tm, d), row),
        out_shape=jax.ShapeDtypeStruct((bsz * seq, d), F32),
        compiler_params=_params("parallel", "parallel"),
        name="s5_out",
    )(y2, u2, h2, d_skip, w_out, gate)


def _s5_tables(lam_re, lam_im, log_dt, b_re, b_im, c_re, c_im):
    dt = jnp.exp(log_dt.astype(F32))[:, None]
    lr, li = lam_re.astype(F32), lam_im.astype(F32)
    mag = jnp.exp(lr * dt)
    a_re = mag * jnp.cos(li * dt)
    a_im = mag * jnp.sin(li * dt)
    den = lr * lr + li * li
    coef_re = ((a_re - 1.0) * lr + a_im * li) / den
    coef_im = (a_im * lr - (a_re - 1.0) * li) / den
    br_, bi_ = b_re.astype(F32), b_im.astype(F32)
    bbar_re = coef_re[..., None] * br_ - coef_im[..., None] * bi_
    bbar_im = coef_re[..., None] * bi_ + coef_im[..., None] * br_
    g, p, c = bbar_re.shape
    gb = S5_GROUPS_PER_BLOCK
    nblk = g // gb
    eye = jnp.eye(gb, dtype=F32)

    def in_blocks(m):
        return jnp.einsum('kgpc,gh->kgchp', m.reshape(nblk, gb, p, c), eye).reshape(nblk, gb * c, gb * p)

    def out_blocks(m):
        return jnp.einsum('kgcp,gh->kgphc', m.reshape(nblk, gb, c, p), eye).reshape(nblk, gb * p, gb * c)

    bblk = jnp.concatenate([in_blocks(bbar_re), in_blocks(bbar_im)], axis=-1).astype(BF16)
    cblk = jnp.concatenate([out_blocks(c_re.astype(F32)), -out_blocks(c_im.astype(F32))], axis=1).astype(BF16)
    return bblk, cblk, a_re.reshape(nblk, 1, gb * p), a_im.reshape(nblk, 1, gb * p)


def _router_body(h_ref, g_ref, sh_ref, sc_ref, wrh_ref, wrl_ref, br_ref, tri_ref,
                 x_ref, mt_ref, cnt_ref, carry_ref):
    tm, d = h_ref.shape
    ne, ng, epg = N_EXPERTS, N_EXPERT_GROUPS, EXPERTS_PER_GROUP

    @pl.when(pl.program_id(0) == 0)
    def _():
        carry_ref[...] = jnp.zeros_like(carry_ref)

    hn = _rms_mod(h_ref[...], g_ref[...], sh_ref[0], sc_ref[0])
    logits = _dot_3pass(hn, wrh_ref, wrl_ref) + br_ref[...]
    lane = lax.broadcasted_iota(I32, logits.shape, 1).astype(F32)
    big = jnp.float32(1e9)
    ninf = jnp.float32(-jnp.inf)

    gmask = (lane >= ne) & (lane < ne + ng)
    gmax = jnp.max(jnp.where(gmask, logits, ninf), axis=-1, keepdims=True)
    gsum = jnp.sum(jnp.where(gmask, jnp.exp(logits - gmax), 0.0), axis=-1, keepdims=True)
    p_g = 1.0 / gsum
    gidx = jnp.min(jnp.where(gmask & (logits == gmax), lane - ne, big), axis=-1, keepdims=True)

    emask = (lane < ne) & (jnp.floor(lane / epg) == gidx)
    v1 = jnp.max(jnp.where(emask, logits, ninf), axis=-1, keepdims=True)
    i1 = jnp.min(jnp.where(emask & (logits == v1), lane, big), axis=-1, keepdims=True)
    emask2 = emask & (lane != i1)
    v2 = jnp.max(jnp.where(emask2, logits, ninf), axis=-1, keepdims=True)
    i2 = jnp.min(jnp.where(emask2 & (logits == v2), lane, big), axis=-1, keepdims=True)
    e21 = jnp.exp(v2 - v1)
    w1 = p_g / (1.0 + e21)
    w2 = p_g * e21 / (1.0 + e21)

    first_lo = i1 < i2
    e_lo = jnp.where(first_lo, i1, i2)
    e_hi = jnp.where(first_lo, i2, i1)
    w_lo = jnp.where(first_lo, w1, w2)
    w_hi = jnp.where(first_lo, w2, w1)
    bucket = gidx * (epg * epg) + (e_lo - gidx * epg) * epg + (e_hi - gidx * epg)

    lane_b = lax.broadcasted_iota(I32, (tm, N_BUCKETS), 1).astype(F32)
    onehot = (lane_b == bucket).astype(F32)
    prefix = jnp.dot(tri_ref[...], onehot.astype(BF16), preferred_element_type=F32)
    carry = carry_ref[...]
    rank = jnp.sum(onehot * (prefix + carry), axis=-1, keepdims=True) - 1.0
    new_carry = carry + prefix[tm - 1:tm, :]
    carry_ref[...] = new_carry
    cnt_ref[...] = new_carry

    mlane = lax.broadcasted_iota(I32, (tm, META_LANES), 1)
    meta = jnp.where(mlane == 0, bucket,
           jnp.where(mlane == 1, rank,
           jnp.where(mlane == 2, w_lo,
           jnp.where(mlane == 3, w_hi, 0.0))))
    x_ref[:, :d] = hn
    x_ref[:, d:] = meta
    mt_ref[0] = meta.T[:META_ROWS, :]


def _router(h2, g, shift, scale, w_r, b_r, tri, nt_per_batch, tm):
    t, d = h2.shape
    wr_hi, wr_lo = _split_bf16(w_r)
    return pl.pallas_call(
        _router_body,
        grid=(t // tm,),
        in_specs=[pl.BlockSpec((tm, d), lambda i: (i, 0)),
                  pl.BlockSpec((1, d), lambda i: (0, 0)),
                  pl.BlockSpec((1, 1, d), lambda i: (i // nt_per_batch, 0, 0)),
                  pl.BlockSpec((1, 1, d), lambda i: (i // nt_per_batch, 0, 0)),
                  pl.BlockSpec((d, LANES), lambda i: (0, 0)),
                  pl.BlockSpec((d, LANES), lambda i: (0, 0)),
                  pl.BlockSpec((1, LANES), lambda i: (0, 0)),
                  pl.BlockSpec((tm, tm), lambda i: (0, 0))],
        out_specs=[pl.BlockSpec((tm, d + META_LANES), lambda i: (i, 0)),
                   pl.BlockSpec((1, META_ROWS, tm), lambda i: (i, 0, 0)),
                   pl.BlockSpec((1, N_BUCKETS), lambda i: (0, 0))],
        out_shape=[jax.ShapeDtypeStruct((t, d + META_LANES), F32),
                   jax.ShapeDtypeStruct((t // tm, META_ROWS, tm), F32),
                   jax.ShapeDtypeStruct((1, N_BUCKETS), F32)],
        scratch_shapes=[pltpu.VMEM((1, N_BUCKETS), F32)],
        compiler_params=_params("arbitrary"),
        name="moe_router",
    )(h2, g, shift, scale, wr_hi, wr_lo, b_r, tri)


def _plan_body(cnt_ref, mt_ref, pos_ref, maps_ref, start_ref, *, te, nwp):
    nb = N_BUCKETS
    epg = EXPERTS_PER_GROUP

    @pl.when(pl.program_id(0) == 0)
    def _():
        r = lax.broadcasted_iota(I32, (nb, nb), 0)
        c = lax.broadcasted_iota(I32, (nb, nb), 1)
        nt_dims = (((1,), (1,)), ((), ()))

        def column(mask, row_vals):
            row8 = jnp.broadcast_to(row_vals, (SUBLANES, nb)).astype(BF16)
            return lax.dot_general(mask.astype(BF16), row8, nt_dims, preferred_element_type=F32)[:, :1]

        cnt = cnt_ref[...]
        cnt_hi = jnp.floor(cnt / 256.0)
        cnt_lo = cnt - 256.0 * cnt_hi
        start = 256.0 * column(c < r, cnt_hi) + column(c < r, cnt_lo)
        count = 256.0 * column(c == r, cnt_hi) + column(c == r, cnt_lo)
        end = start + count
        start_ref[...] = start
        first_tile = jnp.floor(start / te)
        n_items = jnp.where(count > 0.0, jnp.floor((end - 1.0) / te) - first_tile + 1.0, 0.0)
        items8 = jnp.broadcast_to(n_items, (nb, LANES)).astype(BF16)
        item_end = jnp.dot((c <= r).astype(BF16), items8, preferred_element_type=F32)[:, :1]
        item_start = item_end - n_items
        n_total = item_end[nb - 1:nb, :]

        w = lax.broadcasted_iota(I32, (1, nwp), 1).astype(F32)
        wc = jnp.minimum(w, jnp.maximum(n_total - 1.0, 0.0))
        bucket = jnp.sum((item_end <= wc).astype(F32), axis=0, keepdims=True)
        sel = lax.broadcasted_iota(I32, (nb, nwp), 0).astype(F32) == bucket

        def pick(col):
            return jnp.sum(jnp.where(sel, col, 0.0), axis=0, keepdims=True)

        valid = (w < n_total).astype(F32)
        tile = pick(first_tile) + (wc - pick(item_start))
        row_lo = (jnp.maximum(pick(start), tile * te) - tile * te) * valid
        row_hi = (jnp.minimum(pick(end), (tile + 1.0) * te) - tile * te) * valid
        grp = jnp.floor(bucket / (epg * epg))
        within = bucket - grp * (epg * epg)
        lo = jnp.floor(within / epg)
        e_lo = grp * epg + lo
        e_hi = grp * epg + (within - lo * epg)
        row = lax.broadcasted_iota(I32, (SUBLANES, nwp), 0)
        maps = jnp.where(row == 0, tile, jnp.where(row == 1, e_lo, jnp.where(row == 2, e_hi,
               jnp.where(row == 3, valid, jnp.where(row == 4, row_lo, jnp.where(row == 5, row_hi, 0.0))))))
        maps_ref[...] = maps.astype(I32)

    n_sub, _, tm = mt_ref.shape
    rb = lax.broadcasted_iota(I32, (nb, tm), 0).astype(F32)
    for s in range(n_sub):
        bucket_row = mt_ref[s, 0:1, :]
        rank_row = mt_ref[s, 1:2, :]
        pos = jnp.sum(jnp.where(rb == bucket_row, start_ref[...], 0.0), axis=0, keepdims=True) + rank_row
        pos_ref[s] = pos.astype(I32)


def _plan(counts, meta_t, te, n_items_max):
    n_tt, _, tm = meta_t.shape
    nwp = ((n_items_max + LANES - 1) // LANES) * LANES
    body = functools.partial(_plan_body, te=te, nwp=nwp)
    n_sub = math.gcd(n_tt, PLAN_TILES_PER_STEP)
    return pl.pallas_call(
        body,
        grid=(n_tt // n_sub,),
        in_specs=[pl.BlockSpec((1, N_BUCKETS), lambda i: (0, 0)),
                  pl.BlockSpec((n_sub, META_ROWS, tm), lambda i: (i, 0, 0))],
        out_specs=[pl.BlockSpec((n_sub, 1, tm), lambda i: (i, 0, 0)),
                   pl.BlockSpec((SUBLANES, nwp), lambda i: (0, 0))],
        out_shape=[jax.ShapeDtypeStruct((n_tt, 1, tm), I32),
                   jax.ShapeDtypeStruct((SUBLANES, nwp), I32)],
        scratch_shapes=[pltpu.VMEM((N_BUCKETS, 1), F32)],
        compiler_params=_params("arbitrary"),
        name="moe_plan",
    )(counts, meta_t)


def _row_copy(src_ref, src_row, dst_ref, dst_row, sem):
    return pltpu.make_async_copy(src_ref.at[pl.ds(src_row, 1)], dst_ref.at[pl.ds(dst_row, 1)], sem)


def _dispatch_body(pos_ref, x_hbm, xs_ref, xbuf, sem_in, sem_out, *, n_steps):
    tm = xbuf.shape[1]
    i = pl.program_id(0)
    slot = lax.rem(i, 2)

    def tile_in(tile, sl):
        return pltpu.make_async_copy(x_hbm.at[pl.ds(tile * tm, tm)], xbuf.at[sl], sem_in.at[sl])

    def rows_out_wait(sl):
        pltpu.make_async_copy(xbuf.at[sl], xs_ref.at[pl.ds(0, tm)], sem_out.at[sl]).wait()

    @pl.when(i == 0)
    def _():
        tile_in(0, 0).start()

    @pl.when(i >= 1)
    def _():
        rows_out_wait(1 - slot)

    @pl.when(i + 1 < n_steps)
    def _():
        tile_in(i + 1, 1 - slot).start()

    tile_in(i, slot).wait()

    def issue(grp, c):
        r0 = pl.multiple_of(grp * ROW_DMA_UNROLL, ROW_DMA_UNROLL)
        for u in range(ROW_DMA_UNROLL):
            _row_copy(xbuf.at[slot], r0 + u, xs_ref, pos_ref[0, 0, r0 + u], sem_out.at[slot]).start(
                priority=u % 2)
        return c

    lax.fori_loop(0, tm // ROW_DMA_UNROLL, issue, 0, unroll=True)

    @pl.when(i == n_steps - 1)
    def _():
        rows_out_wait(slot)


def _dispatch(xrow, pos3, tm):
    t, w = xrow.shape
    n_steps = t // tm
    body = functools.partial(_dispatch_body, n_steps=n_steps)
    return pl.pallas_call(
        body,
        grid=(n_steps,),
        in_specs=[pl.BlockSpec((1, 1, tm), lambda i: (i, 0, 0), memory_space=pltpu.SMEM),
                  pl.BlockSpec(memory_space=pl.ANY)],
        out_specs=pl.BlockSpec(memory_space=pl.ANY),
        out_shape=jax.ShapeDtypeStruct((t, w), F32),
        scratch_shapes=[pltpu.VMEM((2, tm, w), F32), pltpu.SemaphoreType.DMA((2,)),
                        pltpu.SemaphoreType.DMA((2,))],
        compiler_params=_params("arbitrary"),
        name="moe_dispatch",
    )(pos3, xrow)


def _expert_body(tile_ref, elo_ref, ehi_ref, valid_ref, rlo_ref, rhi_ref, x_ref,
                 w1_ref, w3_ref, w2_ref, o_ref, up_s, dn_s):
    del tile_ref
    te, d = o_ref.shape
    epg = EXPERTS_PER_GROUP
    j = pl.program_id(0)
    prev = jnp.maximum(j - 1, 0)

    @pl.when((j == 0) | (elo_ref[j] // epg != elo_ref[prev] // epg))
    def _():
        for e in range(epg):
            up_s[e, 0] = w1_ref[e].astype(BF16)
            up_s[e, 1] = w3_ref[e].astype(BF16)
            dn_s[e] = w2_ref[e].astype(BF16)

    @pl.when(valid_ref[j] == 1)
    def _():
        rows = lax.broadcasted_iota(I32, (te, 1), 0)
        live = (rows >= rlo_ref[j]) & (rows < rhi_ref[j])
        x = x_ref[:, :d].astype(BF16)

        def ffn(e, wt):
            a = jnp.dot(x, up_s[e, 0], preferred_element_type=F32)
            b = jnp.dot(x, up_s[e, 1], preferred_element_type=F32)
            mid = (a * jax.nn.sigmoid(a)) * b * wt
            return jnp.dot(mid.astype(BF16), dn_s[e], preferred_element_type=F32)

        res = (ffn(elo_ref[j] % epg, jnp.where(live, x_ref[:, d + 2:d + 3], 0.0))
               + ffn(ehi_ref[j] % epg, jnp.where(live, x_ref[:, d + 3:d + 4], 0.0)))

        @pl.when(rlo_ref[j] == 0)
        def _():
            o_ref[...] = res

        @pl.when(rlo_ref[j] != 0)
        def _():
            o_ref[...] += res


def _experts(xs, w1, w3, w2, layer, maps, n_items, te):
    t, w = xs.shape
    _, n_e, d, f = w1.shape
    epg = EXPERTS_PER_GROUP
    x_spec = pl.BlockSpec((te, w), lambda j, tl, lo, hi, v, a, b: (tl[j], 0))
    up = pl.BlockSpec((None, epg, d, f), lambda j, tl, lo, hi, v, a, b: (layer, lo[j] // epg, 0, 0),
                      pipeline_mode=pl.Buffered(1))
    dn = pl.BlockSpec((None, epg, f, d), lambda j, tl, lo, hi, v, a, b: (layer, lo[j] // epg, 0, 0),
                      pipeline_mode=pl.Buffered(1))
    return pl.pallas_call(
        _expert_body,
        grid_spec=pltpu.PrefetchScalarGridSpec(
            num_scalar_prefetch=6,
            grid=(n_items,),
            in_specs=[x_spec, up, up, dn],
            out_specs=pl.BlockSpec((te, d), lambda j, tl, lo, hi, v, a, b: (tl[j], 0)),
            scratch_shapes=[pltpu.VMEM((epg, 2, d, f), BF16), pltpu.VMEM((epg, f, d), BF16)]),
        out_shape=jax.ShapeDtypeStruct((t, d), F32),
        compiler_params=_params("arbitrary"),
        name="moe_experts",
    )(*[maps[i, :n_items] for i in range(6)], xs, w1, w3, w2)


def _combine_body(pos_ref, pos_next_ref, ys_ref, h_ref, gate_ref, o_ref, ybuf, sem, *, n_steps):
    tm = h_ref.shape[0]
    i = pl.program_id(0)
    slot = lax.rem(i, 2)

    def gather(p_ref, sl):
        def issue(grp, c):
            r0 = pl.multiple_of(grp * ROW_DMA_UNROLL, ROW_DMA_UNROLL)
            for u in range(ROW_DMA_UNROLL):
                _row_copy(ys_ref, p_ref[0, 0, r0 + u], ybuf.at[sl], r0 + u, sem.at[sl]).start(priority=u % 2)
            return c

        lax.fori_loop(0, tm // ROW_DMA_UNROLL, issue, 0, unroll=True)

    @pl.when(i == 0)
    def _():
        gather(pos_ref, 0)

    @pl.when(i + 1 < n_steps)
    def _():
        gather(pos_next_ref, 1 - slot)

    pltpu.make_async_copy(ys_ref.at[pl.ds(0, tm)], ybuf.at[slot], sem.at[slot]).wait()
    o_ref[...] = h_ref[...] + gate_ref[0] * ybuf[slot]


def _combine(ys, pos3, h2, gate, nt_per_batch, tm):
    t, d = h2.shape
    n_steps = t // tm
    body = functools.partial(_combine_body, n_steps=n_steps)
    return pl.pallas_call(
        body,
        grid=(n_steps,),
        in_specs=[pl.BlockSpec((1, 1, tm), lambda i: (i, 0, 0), memory_space=pltpu.SMEM),
                  pl.BlockSpec((1, 1, tm), lambda i: (jnp.minimum(i + 1, n_steps - 1), 0, 0),
                               memory_space=pltpu.SMEM),
                  pl.BlockSpec(memory_space=pl.ANY),
                  pl.BlockSpec((tm, d), lambda i: (i, 0)),
                  pl.BlockSpec((1, 1, d), lambda i: (i // nt_per_batch, 0, 0))],
        out_specs=pl.BlockSpec((tm, d), lambda i: (i, 0)),
        out_shape=jax.ShapeDtypeStruct((t, d), F32),
        scratch_shapes=[pltpu.VMEM((2, tm, d), F32), pltpu.SemaphoreType.DMA((2,))],
        compiler_params=_params("arbitrary"),
        name="moe_combine",
    )(pos3, pos3, ys, h2, gate)


def _moe(h2, g, shift, scale, gate, wg, bg, we, be, w1, w3, w2, layer, bsz, seq, tm, te):
    t, d = h2.shape
    nt_per_batch = seq // tm
    ne, ng = N_EXPERTS, N_EXPERT_GROUPS
    w_r = jnp.zeros((d, LANES), F32).at[:, :ne].set(we).at[:, ne:ne + ng].set(wg)
    b_r = jnp.zeros((1, LANES), F32).at[0, :ne].set(be).at[0, ne:ne + ng].set(bg)
    tri = jnp.asarray(np.tril(np.ones((tm, tm), np.float32)), BF16)
    xrow, meta_t, counts = _router(h2, g, shift, scale, w_r, b_r, tri, nt_per_batch, tm)
    n_items_max = t // te + N_PAIR_BUCKETS
    pos3, maps = _plan(counts, meta_t, te, n_items_max)
    xs = _dispatch(xrow, pos3, tm)
    ys = _experts(xs, w1, w3, w2, layer, maps, n_items_max, te)
    return _combine(ys, pos3, h2, gate, nt_per_batch, tm)


def _log_sigmoid(x):
    return jnp.minimum(x, 0.0) - jnp.log1p(jnp.exp(-jnp.abs(x)))


def _aug_tables(n_heads):
    assert n_heads * AUG_LANES_PER_HEAD <= LANES
    width = LANES
    pk = np.zeros((F_SPLIT * LANES, width), np.float32)
    pq = np.zeros((F_SPLIT * LANES, width), np.float32)
    ck = np.zeros((1, width), np.float32)
    cq = np.zeros((1, width), np.float32)
    for h in range(n_heads):
        base = h * AUG_LANES_PER_HEAD
        for j in range(F_SPLIT):
            pk[j * LANES + h, base + j] = -1.0
            pq[j * LANES + h, base + F_SPLIT + j] = 1.0
            ck[0, base + F_SPLIT + j] = 1.0
            cq[0, base + j] = 1.0
    return jnp.asarray(pk, BF16), jnp.asarray(pq, BF16), jnp.asarray(ck), jnp.asarray(cq)


def _kv_body(h_ref, g_ref, sh_ref, sc_ref, wk_ref, wvt_ref, wfh_ref, wfl_ref, fb_ref, kng_ref,
             pk_ref, pq_ref, ck_ref, cq_ref, k_ref, vt_ref, ka_ref, qa_ref, carry_ref):
    tm = h_ref.shape[0]

    @pl.when(pl.program_id(1) == 0)
    def _():
        carry_ref[...] = jnp.zeros_like(carry_ref)

    hn = _rms_mod(h_ref[...], g_ref[...], sh_ref[0], sc_ref[0])
    hb = hn.astype(BF16)
    k = jnp.dot(hb, wk_ref[...], preferred_element_type=F32)
    k_ref[...] = _head_rms(k, kng_ref[...]).astype(BF16)
    nt_dims = (((1,), (1,)), ((), ()))
    vt_ref[...] = lax.dot_general(wvt_ref[...], hb, nt_dims, preferred_element_type=F32).astype(BF16)
    fz = _dot_3pass(hn, wfh_ref, wfl_ref) + fb_ref[...]
    c = _log_sigmoid(fz)
    row = lax.broadcasted_iota(I32, c.shape, 0)
    shift = 1
    while shift < tm:
        c = c + jnp.where(row >= shift, pltpu.roll(c, shift, 0), 0.0)
        shift *= 2
    f = c + carry_ref[...]
    carry_ref[...] = f[tm - 1:tm, :]

    f2 = f * LOG2E
    hi = f2.astype(BF16)
    r1 = f2 - hi.astype(F32)
    mid = r1.astype(BF16)
    lo = (r1 - mid.astype(F32)).astype(BF16)
    pieces = jnp.concatenate([hi, mid, lo], axis=-1)
    ka_ref[...] = (jnp.dot(pieces, pk_ref[...], preferred_element_type=F32) + ck_ref[...]).astype(BF16)
    qa_ref[...] = (jnp.dot(pieces, pq_ref[...], preferred_element_type=F32) + cq_ref[...]).astype(BF16)


def _shared_kv(h2, g, shift, scale, wk, wvt, wf, fb, kng, bsz, seq, tm):
    t, d = h2.shape
    nt = seq // tm
    aw = LANES
    pk, pq, ck, cq = _aug_tables(d // HEAD_DIM)
    wf_hi, wf_lo = _split_bf16(wf)
    row = lambda b, i: (b * nt + i, 0)
    const = lambda b, i: (0, 0)
    return pl.pallas_call(
        _kv_body,
        grid=(bsz, nt),
        in_specs=[pl.BlockSpec((tm, d), row),
                  pl.BlockSpec((1, d), const),
                  pl.BlockSpec((1, 1, d), lambda b, i: (b, 0, 0)),
                  pl.BlockSpec((1, 1, d), lambda b, i: (b, 0, 0)),
                  pl.BlockSpec((d, d), const),
                  pl.BlockSpec((d, d), const),
                  pl.BlockSpec((d, LANES), const),
                  pl.BlockSpec((d, LANES), const),
                  pl.BlockSpec((1, LANES), const),
                  pl.BlockSpec((1, d), const),
                  pl.BlockSpec((F_SPLIT * LANES, aw), const),
                  pl.BlockSpec((F_SPLIT * LANES, aw), const),
                  pl.BlockSpec((1, aw), const),
                  pl.BlockSpec((1, aw), const)],
        out_specs=[pl.BlockSpec((tm, d), row),
                   pl.BlockSpec((d, tm), lambda b, i: (b, i)),
                   pl.BlockSpec((tm, aw), row),
                   pl.BlockSpec((tm, aw), row)],
        out_shape=[jax.ShapeDtypeStruct((t, d), BF16), jax.ShapeDtypeStruct((bsz * d, seq), BF16),
                   jax.ShapeDtypeStruct((t, aw), BF16), jax.ShapeDtypeStruct((t, aw), BF16)],
        scratch_shapes=[pltpu.VMEM((1, LANES), F32)],
        compiler_params=_params("parallel", "arbitrary"),
        name="shared_kv",
    )(h2, g, shift, scale, wk, wvt, wf_hi, wf_lo, fb, kng, pk, pq, ck, cq)


def _qg_body(h_ref, g_ref, sh_ref, sc_ref, wq_ref, wg_ref, qng_ref, q_ref, og_ref):
    hn = _rms_mod(h_ref[...], g_ref[...], sh_ref[0], sc_ref[0])
    hb = hn.astype(BF16)
    q = jnp.dot(hb, wq_ref[...], preferred_element_type=F32)
    q_ref[...] = (_head_rms(q, qng_ref[...]) * (HEAD_DIM ** -0.5 * LOG2E)).astype(BF16)
    og_ref[...] = jnp.dot(hb, wg_ref[...], preferred_element_type=F32).astype(BF16)


def _fox_qg(h2, g, shift, scale, wq, wg, qng, bsz, seq, tm):
    t, d = h2.shape
    nt = seq // tm
    row = lambda b, i: (b * nt + i, 0)
    const = lambda b, i: (0, 0)
    return pl.pallas_call(
        _qg_body,
        grid=(bsz, nt),
        in_specs=[pl.BlockSpec((tm, d), row),
                  pl.BlockSpec((1, d), const),
                  pl.BlockSpec((1, 1, d), lambda b, i: (b, 0, 0)),
                  pl.BlockSpec((1, 1, d), lambda b, i: (b, 0, 0)),
                  pl.BlockSpec((d, d), const),
                  pl.BlockSpec((d, d), const),
                  pl.BlockSpec((1, d), const)],
        out_specs=[pl.BlockSpec((tm, d), row), pl.BlockSpec((tm, d), row)],
        out_shape=[jax.ShapeDtypeStruct((t, d), BF16), jax.ShapeDtypeStruct((t, d), BF16)],
        compiler_params=_params("parallel", "parallel"),
        name="fox_qg",
    )(h2, g, shift, scale, wq, wg, qng)


def _attn_body(q_ref, qa_ref, k_ref, ka_ref, vt_ref, o_ref, m_ref, acc_ref, s0_ref, s1_ref, *, tq, hp):
    grp = pl.program_id(1)
    qi = pl.program_id(2)
    n_slabs = hp // 2
    lane_q = lax.broadcasted_iota(I32, (tq, LANES), 1)
    lo_q = lane_q < HEAD_DIM
    lo_v = lax.broadcasted_iota(I32, (LANES, tq), 0) < HEAD_DIM
    nt_dims = (((1,), (1,)), ((), ()))
    causal = lax.broadcasted_iota(I32, (tq, tq), 0) <= lax.broadcasted_iota(I32, (tq, tq), 1)

    qa = qa_ref[...]
    qcats = []
    for h in range(hp):
        sl, hh = h // 2, h % 2
        q2 = q_ref[:, sl * LANES:(sl + 1) * LANES]
        own = lo_q if hh == 0 else jnp.logical_not(lo_q)
        a0 = (grp * hp + h) * AUG_LANES_PER_HEAD
        own_a = (lane_q >= a0) & (lane_q < a0 + AUG_LANES_PER_HEAD)
        qcats.append(jnp.concatenate([jnp.where(own, q2, jnp.zeros_like(q2)),
                                      jnp.where(own_a, qa, jnp.zeros_like(qa))], axis=-1))

    m_ref[...] = jnp.full(m_ref.shape, NEG, F32)
    ones_r = jnp.ones((DEN_ROWS, tq), BF16)
    zeros_r = jnp.zeros((DEN_ROWS, tq), BF16)
    acc_row = lax.broadcasted_iota(I32, (LANES + 2 * DEN_ROWS, tq), 0)
    first_head_rows = (acc_row < HEAD_DIM) | ((acc_row >= LANES) & (acc_row < LANES + DEN_ROWS))
    acc_ref[...] = jnp.zeros(acc_ref.shape, F32)

    def scores(kb, dst_ref):
        s0 = pl.multiple_of(kb * tq, tq)
        for h in range(hp):
            sl = h // 2
            kcat = jnp.concatenate([k_ref[pl.ds(s0, tq), sl * LANES:(sl + 1) * LANES],
                                    ka_ref[pl.ds(s0, tq), :]], axis=-1)
            dst_ref[h] = lax.dot_general(kcat, qcats[h], nt_dims, preferred_element_type=F32)

    def consume(kb, src_ref, masked):
        s0 = pl.multiple_of(kb * tq, tq)
        for sl in range(n_slabs):
            vt = vt_ref[sl * LANES:(sl + 1) * LANES, pl.ds(s0, tq)]
            vts = (jnp.concatenate([jnp.where(lo_v, vt, jnp.zeros_like(vt)), ones_r, zeros_r], axis=0),
                   jnp.concatenate([jnp.where(lo_v, jnp.zeros_like(vt), vt), zeros_r, ones_r], axis=0))
            pv, alphas = None, []
            for hh in range(2):
                h = sl * 2 + hh
                st = src_ref[h]
                if masked:
                    st = jnp.where(causal, st, NEG)
                m_old = m_ref[h]
                m_new = jnp.maximum(m_old, jnp.max(st, axis=0, keepdims=True))
                alpha = jnp.exp2(m_old - m_new)
                p = jnp.exp2(st - m_new)
                m_ref[h] = m_new
                part = jnp.dot(vts[hh], p.astype(BF16), preferred_element_type=F32)
                pv = part if pv is None else pv + part
                alphas.append(alpha)
            acc_ref[sl] = jnp.where(first_head_rows, alphas[0], alphas[1]) * acc_ref[sl] + pv

    scores(0, s0_ref)

    def pair(j, c):
        scores(2 * j + 1, s1_ref)
        consume(2 * j, s0_ref, False)
        scores(2 * j + 2, s0_ref)
        consume(2 * j + 1, s1_ref, False)
        return c

    lax.fori_loop(0, qi // 2, pair, 0)

    @pl.when(qi % 2 == 0)
    def _():
        consume(qi, s0_ref, True)

    @pl.when(qi % 2 == 1)
    def _():
        scores(qi, s1_ref)
        consume(qi - 1, s0_ref, False)
        consume(qi, s1_ref, True)

    for sl in range(n_slabs):
        acc = acc_ref[sl]
        l2 = jnp.where(lo_v, acc[LANES:LANES + 1, :], acc[LANES + DEN_ROWS:LANES + DEN_ROWS + 1, :])
        o_ref[:, sl * LANES:(sl + 1) * LANES] = (acc[:LANES, :] / l2).T.astype(BF16)


def _fox_attention(q, qaug, k, kaug, vt, bsz, seq, tq, hp):
    t, d = q.shape
    w = hp * HEAD_DIM
    n_grp = d // w
    nq = seq // tq
    body = functools.partial(_attn_body, tq=tq, hp=hp)
    qrow = lambda b, j, i: (b * nq + i, j)
    krow = lambda b, j, i: (b, j)
    return pl.pallas_call(
        body,
        grid=(bsz, n_grp, nq),
        in_specs=[pl.BlockSpec((tq, w), qrow),
                  pl.BlockSpec((tq, LANES), lambda b, j, i: (b * nq + i, 0)),
                  pl.BlockSpec((seq, w), krow),
                  pl.BlockSpec((seq, LANES), lambda b, j, i: (b, 0)),
                  pl.BlockSpec((w, seq), lambda b, j, i: (b * n_grp + j, 0))],
        out_specs=pl.BlockSpec((tq, w), qrow),
        out_shape=jax.ShapeDtypeStruct((t, d), BF16),
        scratch_shapes=[pltpu.VMEM((hp, 1, tq), F32),
                        pltpu.VMEM((hp // 2, LANES + 2 * DEN_ROWS, tq), F32),
                        pltpu.VMEM((hp, tq, tq), F32), pltpu.VMEM((hp, tq, tq), F32)],
        compiler_params=_params("parallel", "parallel", "arbitrary"),
        name="fox_attention",
    )(q, qaug, k, kaug, vt)


def _fox_out_body(o_ref, og_ref, h_ref, w_ref, gate_ref, out_ref):
    z = o_ref[...].astype(F32) * jax.nn.sigmoid(og_ref[...].astype(F32))
    out_ref[...] = h_ref[...] + gate_ref[0] * jnp.dot(z.astype(BF16), w_ref[...], preferred_element_type=F32)


def _fox_out(o, og, h2, w_o, gate, bsz, seq, tm):
    t, d = h2.shape
    nt = seq // tm
    row = lambda b, i: (b * nt + i, 0)
    return pl.pallas_call(
        _fox_out_body,
        grid=(bsz, nt),
        in_specs=[pl.BlockSpec((tm, d), row), pl.BlockSpec((tm, d), row), pl.BlockSpec((tm, d), row),
                  pl.BlockSpec((d, d), lambda b, i: (0, 0)),
                  pl.BlockSpec((1, 1, d), lambda b, i: (b, 0, 0))],
        out_specs=pl.BlockSpec((tm, d), row),
        out_shape=jax.ShapeDtypeStruct((t, d), F32),
        compiler_params=_params("parallel", "parallel"),
        name="fox_out",
    )(o, og, h2, w_o, gate)


def _tiles(seq):
    tm = min(512, seq)
    te = min(256, seq)
    tq = min(512, seq)
    tc = min(64, seq)
    return tm, te, tq, tc


def kernel(x, c, ln_g, ada_w, ada_b, s5_w_in, s5_lambda_re, s5_lambda_im, s5_log_dt, s5_b_re, s5_b_im,
           s5_c_re, s5_c_im, s5_d, s5_w_out, kv_g, kv_ada_w, kv_ada_b, kv_w, kv_fb, k_norm_g,
           fox_w_qg, fox_q_norm_g, fox_w_o, moe_wg, moe_bg, moe_we, moe_be, moe_w1, moe_w3, moe_w2):
    bsz, seq, d = x.shape
    depth = ln_g.shape[0]
    n_a = s5_w_in.shape[0]
    n_heads = d // HEAD_DIM
    tm, te, tq, tc = _tiles(seq)

    mods = _adaln(c, ada_w.reshape(depth * 2, d, 3 * d), ada_b.reshape(depth * 2, 1, 3 * d))
    mods = mods.reshape(depth, 2, bsz, 3, 1, d)
    kv_mods = _adaln(c, kv_ada_w[None], kv_ada_b[None, None]).reshape(bsz, 2, 1, d)

    h = x.reshape(bsz * seq, d)
    k = kaug = qaug = vt = None
    for l in range(depth):
        shift, scale, gate = mods[l, 0, :, 0], mods[l, 0, :, 1], mods[l, 0, :, 2]
        g = ln_g[l, 0][None]
        if l < n_a:
            u2 = _s5_in(h, g, shift, scale, s5_w_in[l].astype(BF16), bsz, seq, tm)
            bblk, cblk, a_re, a_im = _s5_tables(s5_lambda_re[l], s5_lambda_im[l], s5_log_dt[l],
                                                s5_b_re[l], s5_b_im[l], s5_c_re[l], s5_c_im[l])
            y2 = _s5_scan(u2, bblk, cblk, a_re, a_im, bsz, seq, tc)
            h = _s5_out(y2, u2, h, s5_d[l][None], s5_w_out[l].astype(BF16), gate, bsz, seq, tm)
        else:
            j = l - n_a
            qng = jnp.tile(fox_q_norm_g[j], n_heads)[None]
            q, og = _fox_qg(h, g, shift, scale, fox_w_qg[j][:, :d].astype(BF16),
                            fox_w_qg[j][:, d:].astype(BF16), qng, bsz, seq, tm)
            o = _fox_attention(q, qaug, k, kaug, vt, bsz, seq, tq, min(ATTN_HEADS_PER_STEP, n_heads))
            h = _fox_out(o, og, h, fox_w_o[j].astype(BF16), gate, bsz, seq, tm)

        shift, scale, gate = mods[l, 1, :, 0], mods[l, 1, :, 1], mods[l, 1, :, 2]
        h = _moe(h, ln_g[l, 1][None], shift, scale, gate, moe_wg[l], moe_bg[l], moe_we[l], moe_be[l],
                 moe_w1, moe_w3, moe_w2, l, bsz, seq, tm, te)

        if l == n_a - 1:
            wf = jnp.zeros((d, LANES), F32).at[:, :n_heads].set(kv_w[:, 2 * d:])
            fb = jnp.zeros((1, LANES), F32).at[0, :n_heads].set(kv_fb)
            kng = jnp.tile(k_norm_g, n_heads)[None]
            k, vt, kaug, qaug = _shared_kv(h, kv_g[None], kv_mods[:, 0], kv_mods[:, 1],
                                           kv_w[:, :d].astype(BF16), kv_w[:, d:2 * d].T.astype(BF16),
                                           wf, fb, kng, bsz, seq, tm)
    return h.reshape(bsz, seq, d)
```

```python
import functools
import math

import numpy as np
import jax
import jax.numpy as jnp
from jax import lax
from jax.experimental import pallas as pl
from jax.experimental.pallas import tpu as pltpu

F32 = jnp.float32
BF16 = jnp.bfloat16
I32 = jnp.int32

EPS = 1e-6
NEG = -1e30
LOG2E = math.log2(math.e)
LANES = 128
SUBLANES = 8
VMEM_LIMIT_BYTES = 56 * 1024 * 1024

S5_GROUPS_PER_BLOCK = 16
N_EXPERT_GROUPS = 4
EXPERTS_PER_GROUP = 8
N_EXPERTS = N_EXPERT_GROUPS * EXPERTS_PER_GROUP
N_BUCKETS = N_EXPERT_GROUPS * EXPERTS_PER_GROUP * EXPERTS_PER_GROUP
N_PAIR_BUCKETS = N_EXPERT_GROUPS * (EXPERTS_PER_GROUP * (EXPERTS_PER_GROUP - 1) // 2)
META_LANES = LANES
META_ROWS = SUBLANES
ROW_DMA_UNROLL = 8
PLAN_TILES_PER_STEP = 8
HEAD_DIM = 64
ATTN_HEADS_PER_STEP = 8
DEN_ROWS = 16
F_SPLIT = 3
AUG_LANES_PER_HEAD = 2 * F_SPLIT


def _params(*sem):
    return pltpu.CompilerParams(dimension_semantics=sem, vmem_limit_bytes=VMEM_LIMIT_BYTES)


def _rms_mod(x, g, shift, scale):
    ms = jnp.mean(x * x, axis=-1, keepdims=True)
    y = x * lax.rsqrt(ms + EPS) * g
    return y * (1.0 + scale) + shift


def _split_bf16(w):
    hi = w.astype(BF16)
    return hi, (w - hi.astype(F32)).astype(BF16)


def _dot_3pass(x, w_hl_ref):
    n = w_hl_ref.shape[1] // 2
    x_hi, x_lo = _split_bf16(x)
    both = jnp.dot(x_hi, w_hl_ref[...], preferred_element_type=F32)
    return both[:, :n] + both[:, n:] + jnp.dot(x_lo, w_hl_ref[:, :n], preferred_element_type=F32)


def _head_rms(x, g):
    tm, d = x.shape
    lane = lax.broadcasted_iota(I32, (tm, LANES), 1)
    lo = lane < HEAD_DIM
    outs = []
    for j in range(d // LANES):
        s = x[:, j * LANES:(j + 1) * LANES]
        sq = s * s
        s_lo = jnp.sum(jnp.where(lo, sq, 0.0), axis=-1, keepdims=True)
        s_hi = jnp.sum(jnp.where(lo, 0.0, sq), axis=-1, keepdims=True)
        r = jnp.where(lo, lax.rsqrt(s_lo / HEAD_DIM + EPS), lax.rsqrt(s_hi / HEAD_DIM + EPS))
        outs.append(s * r)
    return jnp.concatenate(outs, axis=-1) * g


def _adaln_body(c_ref, w_ref, b_ref, o_ref):
    c = c_ref[...]
    s = c * jax.nn.sigmoid(c)
    o_ref[0] = jnp.dot(s, w_ref[0], preferred_element_type=F32) + b_ref[0]


def _adaln(c, w, b):
    n_sets, d, n = w.shape
    bsz = c.shape[0]
    tn = 512 if n % 512 == 0 else n
    return pl.pallas_call(
        _adaln_body,
        grid=(n_sets, n // tn),
        in_specs=[pl.BlockSpec((bsz, d), lambda s, j: (0, 0)),
                  pl.BlockSpec((1, d, tn), lambda s, j: (s, 0, j)),
                  pl.BlockSpec((1, 1, tn), lambda s, j: (s, 0, j))],
        out_specs=pl.BlockSpec((1, bsz, tn), lambda s, j: (s, 0, j)),
        out_shape=jax.ShapeDtypeStruct((n_sets, bsz, n), F32),
        compiler_params=_params("parallel", "parallel"),
        name="adaln",
    )(c, w, b)


def _s5_in_body(x_ref, g_ref, sh_ref, sc_ref, w_ref, u_ref):
    hn = _rms_mod(x_ref[...], g_ref[...], sh_ref[0], sc_ref[0])
    u_ref[...] = jnp.dot(hn.astype(BF16), w_ref[...], preferred_element_type=F32)


def _s5_in(x2, g, shift, scale, w_in, bsz, seq, tm):
    d = x2.shape[1]
    nt = seq // tm
    row = lambda b, i: (b * nt + i, 0)
    return pl.pallas_call(
        _s5_in_body,
        grid=(bsz, nt),
        in_specs=[pl.BlockSpec((tm, d), row),
                  pl.BlockSpec((1, d), lambda b, i: (0, 0)),
                  pl.BlockSpec((1, 1, d), lambda b, i: (b, 0, 0)),
                  pl.BlockSpec((1, 1, d), lambda b, i: (b, 0, 0)),
                  pl.BlockSpec((d, d), lambda b, i: (0, 0))],
        out_specs=pl.BlockSpec((tm, d), row),
        out_shape=jax.ShapeDtypeStruct((bsz * seq, d), F32),
        compiler_params=_params("parallel", "parallel"),
        name="s5_in",
    )(x2, g, shift, scale, w_in)


def _s5_scan_body(u_hbm, bb_ref, cb_ref, are_ref, aim_ref, y_hbm,
                  ubuf, ybuf, bu_ref, st_ref, sem_in, sem_out, *, tc, nblk, sw, seq, n_chunks):
    bsz = SUBLANES
    cw = S5_GROUPS_PER_BLOCK * 16
    i = pl.program_id(0)
    slot = lax.rem(i, 2)

    def in_copy(chunk, sl, b):
        return pltpu.make_async_copy(u_hbm.at[pl.ds(b * seq + chunk * tc, tc)],
                                     ubuf.at[sl, :, b, :], sem_in.at[sl])

    def out_copy(chunk, sl, b):
        return pltpu.make_async_copy(ybuf.at[sl, :, b, :],
                                     y_hbm.at[pl.ds(b * seq + chunk * tc, tc)], sem_out.at[sl])

    @pl.when(i == 0)
    def _():
        st_ref[...] = jnp.zeros_like(st_ref)
        for b in range(bsz):
            in_copy(0, 0, b).start()

    @pl.when(i + 1 < n_chunks)
    def _():
        for b in range(bsz):
            in_copy(i + 1, 1 - slot, b).start()

    for b in range(bsz):
        in_copy(i, slot, b).wait()

    @pl.when(i >= 2)
    def _():
        for b in range(bsz):
            out_copy(i - 2, slot, b).wait()

    d = ubuf.shape[-1]
    u2 = ubuf[slot].reshape(tc * bsz, d).astype(BF16)
    for k in range(nblk):
        bu_ref[:, k * 2 * sw:(k + 1) * 2 * sw] = jnp.dot(
            u2[:, k * cw:(k + 1) * cw], bb_ref[k], preferred_element_type=F32)
        re0, im0 = k * 2 * sw, k * 2 * sw + sw
        a_re = jnp.broadcast_to(are_ref[k], (SUBLANES, sw))
        a_im = jnp.broadcast_to(aim_ref[k], (SUBLANES, sw))

        def step(t, carry, re0=re0, im0=im0, a_re=a_re, a_im=a_im):
            s_re, s_im = carry
            r0 = pl.multiple_of(t * SUBLANES, SUBLANES)
            n_re = a_re * s_re - a_im * s_im + bu_ref[pl.ds(r0, SUBLANES), re0:re0 + sw]
            n_im = a_re * s_im + a_im * s_re + bu_ref[pl.ds(r0, SUBLANES), im0:im0 + sw]
            bu_ref[pl.ds(r0, SUBLANES), re0:re0 + sw] = n_re
            bu_ref[pl.ds(r0, SUBLANES), im0:im0 + sw] = n_im
            return n_re, n_im

        s_re, s_im = lax.fori_loop(
            0, tc, step, (st_ref[:, re0:re0 + sw], st_ref[:, im0:im0 + sw]), unroll=True)
        st_ref[:, re0:re0 + sw] = s_re
        st_ref[:, im0:im0 + sw] = s_im
        s2 = bu_ref[:, k * 2 * sw:(k + 1) * 2 * sw].astype(BF16)
        yk = jnp.dot(s2, cb_ref[k], preferred_element_type=F32)
        ybuf[slot, :, :, k * cw:(k + 1) * cw] = yk.reshape(tc, bsz, cw)

    for b in range(bsz):
        out_copy(i, slot, b).start()

    @pl.when(i == n_chunks - 1)
    def _():
        if n_chunks >= 2:
            for b in range(bsz):
                out_copy(i - 1, 1 - slot, b).wait()
        for b in range(bsz):
            out_copy(i, slot, b).wait()


def _s5_scan(u2, bblk, cblk, a_re, a_im, bsz, seq, tc):
    assert bsz == SUBLANES, "the scan keeps the batch on the 8 sublanes of a vreg"
    rows, d = u2.shape
    nblk, cw, sw2 = bblk.shape
    sw = sw2 // 2
    n_chunks = seq // tc
    body = functools.partial(_s5_scan_body, tc=tc, nblk=nblk, sw=sw, seq=seq, n_chunks=n_chunks)
    return pl.pallas_call(
        body,
        grid=(n_chunks,),
        in_specs=[pl.BlockSpec(memory_space=pl.ANY),
                  pl.BlockSpec((nblk, cw, sw2), lambda i: (0, 0, 0)),
                  pl.BlockSpec((nblk, sw2, cw), lambda i: (0, 0, 0)),
                  pl.BlockSpec((nblk, 1, sw), lambda i: (0, 0, 0)),
                  pl.BlockSpec((nblk, 1, sw), lambda i: (0, 0, 0))],
        out_specs=pl.BlockSpec(memory_space=pl.ANY),
        out_shape=jax.ShapeDtypeStruct((rows, d), F32),
        scratch_shapes=[pltpu.VMEM((2, tc, bsz, d), F32), pltpu.VMEM((2, tc, bsz, d), F32),
                        pltpu.VMEM((tc * bsz, nblk * sw2), F32), pltpu.VMEM((bsz, nblk * sw2), F32),
                        pltpu.SemaphoreType.DMA((2,)), pltpu.SemaphoreType.DMA((2,))],
        compiler_params=_params("arbitrary"),
        name="s5_scan",
    )(u2, bblk, cblk, a_re, a_im)


def _s5_out_body(y_ref, u_ref, h_ref, d_ref, w_ref, gate_ref, o_ref):
    z = y_ref[...] + d_ref[...] * u_ref[...]
    act = jax.nn.gelu(z)
    vg = jnp.dot(act.astype(BF16), w_ref[...], preferred_element_type=F32)
    d = z.shape[-1]
    mix = vg[:, :d] * jax.nn.sigmoid(vg[:, d:])
    o_ref[...] = h_ref[...] + gate_ref[0] * mix


def _s5_out(y2, u2, h2, d_skip, w_out, gate, bsz, seq, tm):
    d = h2.shape[1]
    nt = seq // tm
    row = lambda b, i: (b * nt + i, 0)
    return pl.pallas_call(
        _s5_out_body,
        grid=(bsz, nt),
        in_specs=[pl.BlockSpec((tm, d), row),
                  pl.BlockSpec((tm, d), row),
                  pl.BlockSpec((tm, d), row),
                  pl.BlockSpec((1, d), lambda b, i: (0, 0)),
                  pl.BlockSpec((d, 2 * d), lambda b, i: (0, 0)),
                  pl.BlockSpec((1, 1, d), lambda b, i: (b, 0, 0))],
        out_specs=pl.BlockSpec((tm, d), row),
        out_shape=jax.ShapeDtypeStruct((bsz * seq, d), F32),
        compiler_params=_params("parallel", "parallel"),
        name="s5_out",
    )(y2, u2, h2, d_skip, w_out, gate)


def _s5_tables(lam_re, lam_im, log_dt, b_re, b_im, c_re, c_im):
    dt = jnp.exp(log_dt.astype(F32))[:, None]
    lr, li = lam_re.astype(F32), lam_im.astype(F32)
    mag = jnp.exp(lr * dt)
    a_re = mag * jnp.cos(li * dt)
    a_im = mag * jnp.sin(li * dt)
    den = lr * lr + li * li
    coef_re = ((a_re - 1.0) * lr + a_im * li) / den
    coef_im = (a_im * lr - (a_re - 1.0) * li) / den
    br_, bi_ = b_re.astype(F32), b_im.astype(F32)
    bbar_re = coef_re[..., None] * br_ - coef_im[..., None] * bi_
    bbar_im = coef_re[..., None] * bi_ + coef_im[..., None] * br_
    g, p, c = bbar_re.shape
    gb = S5_GROUPS_PER_BLOCK
    nblk = g // gb
    eye = jnp.eye(gb, dtype=F32)

    def in_blocks(m):
        return jnp.einsum('kgpc,gh->kgchp', m.reshape(nblk, gb, p, c), eye).reshape(nblk, gb * c, gb * p)

    def out_blocks(m):
        return jnp.einsum('kgcp,gh->kgphc', m.reshape(nblk, gb, c, p), eye).reshape(nblk, gb * p, gb * c)

    bblk = jnp.concatenate([in_blocks(bbar_re), in_blocks(bbar_im)], axis=-1).astype(BF16)
    cblk = jnp.concatenate([out_blocks(c_re.astype(F32)), -out_blocks(c_im.astype(F32))], axis=1).astype(BF16)
    return bblk, cblk, a_re.reshape(nblk, 1, gb * p), a_im.reshape(nblk, 1, gb * p)


def _router_body(h_ref, g_ref, sh_ref, sc_ref, wr_ref, br_ref, tri_ref,
                 x_ref, mt_ref, cnt_ref, carry_ref):
    tm, d = h_ref.shape
    ne, ng, epg = N_EXPERTS, N_EXPERT_GROUPS, EXPERTS_PER_GROUP

    @pl.when(pl.program_id(0) == 0)
    def _():
        carry_ref[...] = jnp.zeros_like(carry_ref)

    hn = _rms_mod(h_ref[...], g_ref[...], sh_ref[0], sc_ref[0])
    logits = _dot_3pass(hn, wr_ref) + br_ref[...]
    lane = lax.broadcasted_iota(I32, logits.shape, 1).astype(F32)
    big = jnp.float32(1e9)
    ninf = jnp.float32(-jnp.inf)

    gmask = (lane >= ne) & (lane < ne + ng)
    gmax = jnp.max(jnp.where(gmask, logits, ninf), axis=-1, keepdims=True)
    gsum = jnp.sum(jnp.where(gmask, jnp.exp(logits - gmax), 0.0), axis=-1, keepdims=True)
    p_g = 1.0 / gsum
    gidx = jnp.min(jnp.where(gmask & (logits == gmax), lane - ne, big), axis=-1, keepdims=True)

    emask = (lane < ne) & (jnp.floor(lane / epg) == gidx)
    v1 = jnp.max(jnp.where(emask, logits, ninf), axis=-1, keepdims=True)
    i1 = jnp.min(jnp.where(emask & (logits == v1), lane, big), axis=-1, keepdims=True)
    emask2 = emask & (lane != i1)
    v2 = jnp.max(jnp.where(emask2, logits, ninf), axis=-1, keepdims=True)
    i2 = jnp.min(jnp.where(emask2 & (logits == v2), lane, big), axis=-1, keepdims=True)
    e21 = jnp.exp(v2 - v1)
    w1 = p_g / (1.0 + e21)
    w2 = p_g * e21 / (1.0 + e21)

    first_lo = i1 < i2
    e_lo = jnp.where(first_lo, i1, i2)
    e_hi = jnp.where(first_lo, i2, i1)
    w_lo = jnp.where(first_lo, w1, w2)
    w_hi = jnp.where(first_lo, w2, w1)
    bucket = gidx * (epg * epg) + (e_lo - gidx * epg) * epg + (e_hi - gidx * epg)

    lane_b = lax.broadcasted_iota(I32, (tm, N_BUCKETS), 1).astype(F32)
    onehot = (lane_b == bucket).astype(F32)
    prefix = jnp.dot(tri_ref[...], onehot.astype(BF16), preferred_element_type=F32)
    carry = carry_ref[...]
    rank = jnp.sum(onehot * (prefix + carry), axis=-1, keepdims=True) - 1.0
    new_carry = carry + prefix[tm - 1:tm, :]
    carry_ref[...] = new_carry
    cnt_ref[...] = new_carry

    mlane = lax.broadcasted_iota(I32, (tm, META_LANES), 1)
    meta = jnp.where(mlane == 0, bucket,
           jnp.where(mlane == 1, rank,
           jnp.where(mlane == 2, w_lo,
           jnp.where(mlane == 3, w_hi, 0.0))))
    x_ref[:, :d] = hn
    x_ref[:, d:] = meta
    mt_ref[0] = meta.T[:META_ROWS, :]


def _router(h2, g, shift, scale, w_r, b_r, tri, nt_per_batch, tm):
    t, d = h2.shape
    wr_hl = jnp.concatenate(_split_bf16(w_r), axis=1)
    return pl.pallas_call(
        _router_body,
        grid=(t // tm,),
        in_specs=[pl.BlockSpec((tm, d), lambda i: (i, 0)),
                  pl.BlockSpec((1, d), lambda i: (0, 0)),
                  pl.BlockSpec((1, 1, d), lambda i: (i // nt_per_batch, 0, 0)),
                  pl.BlockSpec((1, 1, d), lambda i: (i // nt_per_batch, 0, 0)),
                  pl.BlockSpec((d, 2 * LANES), lambda i: (0, 0)),
                  pl.BlockSpec((1, LANES), lambda i: (0, 0)),
                  pl.BlockSpec((tm, tm), lambda i: (0, 0))],
        out_specs=[pl.BlockSpec((tm, d + META_LANES), lambda i: (i, 0)),
                   pl.BlockSpec((1, META_ROWS, tm), lambda i: (i, 0, 0)),
                   pl.BlockSpec((1, N_BUCKETS), lambda i: (0, 0))],
        out_shape=[jax.ShapeDtypeStruct((t, d + META_LANES), F32),
                   jax.ShapeDtypeStruct((t // tm, META_ROWS, tm), F32),
                   jax.ShapeDtypeStruct((1, N_BUCKETS), F32)],
        scratch_shapes=[pltpu.VMEM((1, N_BUCKETS), F32)],
        compiler_params=_params("arbitrary"),
        name="moe_router",
    )(h2, g, shift, scale, wr_hl, b_r, tri)


def _plan_body(cnt_ref, mt_ref, pos_ref, maps_ref, start_ref, *, te, nwp):
    nb = N_BUCKETS
    epg = EXPERTS_PER_GROUP

    @pl.when(pl.program_id(0) == 0)
    def _():
        r = lax.broadcasted_iota(I32, (nb, nb), 0)
        c = lax.broadcasted_iota(I32, (nb, nb), 1)
        nt_dims = (((1,), (1,)), ((), ()))

        def column(mask, row_vals):
            row8 = jnp.broadcast_to(row_vals, (SUBLANES, nb)).astype(BF16)
            return lax.dot_general(mask.astype(BF16), row8, nt_dims, preferred_element_type=F32)[:, :1]

        cnt = cnt_ref[...]
        cnt_hi = jnp.floor(cnt / 256.0)
        cnt_lo = cnt - 256.0 * cnt_hi
        start = 256.0 * column(c < r, cnt_hi) + column(c < r, cnt_lo)
        count = 256.0 * column(c == r, cnt_hi) + column(c == r, cnt_lo)
        end = start + count
        start_ref[...] = start
        first_tile = jnp.floor(start / te)
        n_items = jnp.where(count > 0.0, jnp.floor((end - 1.0) / te) - first_tile + 1.0, 0.0)
        items8 = jnp.broadcast_to(n_items, (nb, LANES)).astype(BF16)
        item_end = jnp.dot((c <= r).astype(BF16), items8, preferred_element_type=F32)[:, :1]
        item_start = item_end - n_items
        n_total = item_end[nb - 1:nb, :]

        w = lax.broadcasted_iota(I32, (1, nwp), 1).astype(F32)
        wc = jnp.minimum(w, jnp.maximum(n_total - 1.0, 0.0))
        bucket = jnp.sum((item_end <= wc).astype(F32), axis=0, keepdims=True)
        sel = lax.broadcasted_iota(I32, (nb, nwp), 0).astype(F32) == bucket

        def pick(col):
            return jnp.sum(jnp.where(sel, col, 0.0), axis=0, keepdims=True)

        valid = (w < n_total).astype(F32)
        tile = pick(first_tile) + (wc - pick(item_start))
        row_lo = (jnp.maximum(pick(start), tile * te) - tile * te) * valid
        row_hi = (jnp.minimum(pick(end), (tile + 1.0) * te) - tile * te) * valid
        grp = jnp.floor(bucket / (epg * epg))
        within = bucket - grp * (epg * epg)
        lo = jnp.floor(within / epg)
        e_lo = grp * epg + lo
        e_hi = grp * epg + (within - lo * epg)
        row = lax.broadcasted_iota(I32, (SUBLANES, nwp), 0)
        maps = jnp.where(row == 0, tile, jnp.where(row == 1, e_lo, jnp.where(row == 2, e_hi,
               jnp.where(row == 3, valid, jnp.where(row == 4, row_lo, jnp.where(row == 5, row_hi, 0.0))))))
        maps_ref[...] = maps.astype(I32)

    n_sub, _, tm = mt_ref.shape
    rb = lax.broadcasted_iota(I32, (nb, tm), 0).astype(F32)
    for s in range(n_sub):
        bucket_row = mt_ref[s, 0:1, :]
        rank_row = mt_ref[s, 1:2, :]
        pos = jnp.sum(jnp.where(rb == bucket_row, start_ref[...], 0.0), axis=0, keepdims=True) + rank_row
        pos_ref[s] = pos.astype(I32)


def _plan(counts, meta_t, te, n_items_max):
    n_tt, _, tm = meta_t.shape
    nwp = ((n_items_max + LANES - 1) // LANES) * LANES
    body = functools.partial(_plan_body, te=te, nwp=nwp)
    n_sub = math.gcd(n_tt, PLAN_TILES_PER_STEP)
    return pl.pallas_call(
        body,
        grid=(n_tt // n_sub,),
        in_specs=[pl.BlockSpec((1, N_BUCKETS), lambda i: (0, 0)),
                  pl.BlockSpec((n_sub, META_ROWS, tm), lambda i: (i, 0, 0))],
        out_specs=[pl.BlockSpec((n_sub, 1, tm), lambda i: (i, 0, 0)),
                   pl.BlockSpec((SUBLANES, nwp), lambda i: (0, 0))],
        out_shape=[jax.ShapeDtypeStruct((n_tt, 1, tm), I32),
                   jax.ShapeDtypeStruct((SUBLANES, nwp), I32)],
        scratch_shapes=[pltpu.VMEM((N_BUCKETS, 1), F32)],
        compiler_params=_params("arbitrary"),
        name="moe_plan",
    )(counts, meta_t)


def _row_copy(src_ref, src_row, dst_ref, dst_row, sem):
    return pltpu.make_async_copy(src_ref.at[pl.ds(src_row, 1)], dst_ref.at[pl.ds(dst_row, 1)], sem)


def _dispatch_body(pos_ref, x_ref, xs_ref, sem):
    tm = x_ref.shape[0]

    def issue(grp, c):
        r0 = pl.multiple_of(grp * ROW_DMA_UNROLL, ROW_DMA_UNROLL)
        for u in range(ROW_DMA_UNROLL):
            _row_copy(x_ref, r0 + u, xs_ref, pos_ref[0, 0, r0 + u], sem).start(priority=u % 2)
        return c

    lax.fori_loop(0, tm // ROW_DMA_UNROLL, issue, 0, unroll=True)
    pltpu.make_async_copy(x_ref, xs_ref.at[pl.ds(0, tm)], sem).wait()


def _dispatch(xrow, pos3, tm):
    t, w = xrow.shape
    t_pad = t
    return pl.pallas_call(
        _dispatch_body,
        grid=(t // tm,),
        in_specs=[pl.BlockSpec((1, 1, tm), lambda i: (i, 0, 0), memory_space=pltpu.SMEM),
                  pl.BlockSpec((tm, w), lambda i: (i, 0))],
        out_specs=pl.BlockSpec(memory_space=pl.ANY),
        out_shape=jax.ShapeDtypeStruct((t_pad, w), F32),
        scratch_shapes=[pltpu.SemaphoreType.DMA(())],
        compiler_params=_params("arbitrary"),
        name="moe_dispatch",
    )(pos3, xrow)


def _expert_body(tile_ref, elo_ref, ehi_ref, valid_ref, rlo_ref, rhi_ref, x_ref,
                 w1_ref, w3_ref, w2_ref, o_ref, up_s, dn_s):
    del tile_ref
    te, d = o_ref.shape
    epg = EXPERTS_PER_GROUP
    j = pl.program_id(0)
    prev = jnp.maximum(j - 1, 0)

    @pl.when((j == 0) | (elo_ref[j] // epg != elo_ref[prev] // epg))
    def _():
        for e in range(epg):
            up_s[e, 0] = w1_ref[e].astype(BF16)
            up_s[e, 1] = w3_ref[e].astype(BF16)
            dn_s[e] = w2_ref[e].astype(BF16)

    @pl.when(valid_ref[j] == 1)
    def _():
        rows = lax.broadcasted_iota(I32, (te, 1), 0)
        live = (rows >= rlo_ref[j]) & (rows < rhi_ref[j])
        x = x_ref[:, :d].astype(BF16)

        def ffn(e, wt):
            a = jnp.dot(x, up_s[e, 0], preferred_element_type=F32)
            b = jnp.dot(x, up_s[e, 1], preferred_element_type=F32)
            mid = (a * jax.nn.sigmoid(a)) * b * wt
            return jnp.dot(mid.astype(BF16), dn_s[e], preferred_element_type=F32)

        res = (ffn(elo_ref[j] % epg, jnp.where(live, x_ref[:, d + 2:d + 3], 0.0))
               + ffn(ehi_ref[j] % epg, jnp.where(live, x_ref[:, d + 3:d + 4], 0.0)))

        @pl.when(rlo_ref[j] == 0)
        def _():
            o_ref[...] = res

        @pl.when(rlo_ref[j] != 0)
        def _():
            o_ref[...] += res


def _experts(xs, w1, w3, w2, layer, maps, n_items, te):
    t, w = xs.shape
    _, n_e, d, f = w1.shape
    epg = EXPERTS_PER_GROUP
    x_spec = pl.BlockSpec((te, w), lambda j, tl, lo, hi, v, a, b: (tl[j], 0))
    up = pl.BlockSpec((None, epg, d, f), lambda j, tl, lo, hi, v, a, b: (layer, lo[j] // epg, 0, 0),
                      pipeline_mode=pl.Buffered(1))
    dn = pl.BlockSpec((None, epg, f, d), lambda j, tl, lo, hi, v, a, b: (layer, lo[j] // epg, 0, 0),
                      pipeline_mode=pl.Buffered(1))
    return pl.pallas_call(
        _expert_body,
        grid_spec=pltpu.PrefetchScalarGridSpec(
            num_scalar_prefetch=6,
            grid=(n_items,),
            in_specs=[x_spec, up, up, dn],
            out_specs=pl.BlockSpec((te, d), lambda j, tl, lo, hi, v, a, b: (tl[j], 0)),
            scratch_shapes=[pltpu.VMEM((epg, 2, d, f), BF16), pltpu.VMEM((epg, f, d), BF16)]),
        out_shape=jax.ShapeDtypeStruct((t, d), F32),
        compiler_params=_params("arbitrary"),
        name="moe_experts",
    )(*[maps[i, :n_items] for i in range(6)], xs, w1, w3, w2)


def _combine_body(pos_ref, pos_next_ref, ys_ref, h_ref, gate_ref, o_ref, ybuf, sem, *, n_steps):
    tm = h_ref.shape[0]
    i = pl.program_id(0)
    slot = lax.rem(i, 2)

    def gather(p_ref, sl):
        def issue(grp, c):
            r0 = pl.multiple_of(grp * ROW_DMA_UNROLL, ROW_DMA_UNROLL)
            for u in range(ROW_DMA_UNROLL):
                _row_copy(ys_ref, p_ref[0, 0, r0 + u], ybuf.at[sl], r0 + u, sem.at[sl]).start(priority=u % 2)
            return c

        lax.fori_loop(0, tm // ROW_DMA_UNROLL, issue, 0, unroll=True)

    @pl.when(i == 0)
    def _():
        gather(pos_ref, 0)

    @pl.when(i + 1 < n_steps)
    def _():
        gather(pos_next_ref, 1 - slot)

    pltpu.make_async_copy(ys_ref.at[pl.ds(0, tm)], ybuf.at[slot], sem.at[slot]).wait()
    o_ref[...] = h_ref[...] + gate_ref[0] * ybuf[slot]


def _combine(ys, pos3, h2, gate, nt_per_batch, tm):
    t, d = h2.shape
    n_steps = t // tm
    body = functools.partial(_combine_body, n_steps=n_steps)
    return pl.pallas_call(
        body,
        grid=(n_steps,),
        in_specs=[pl.BlockSpec((1, 1, tm), lambda i: (i, 0, 0), memory_space=pltpu.SMEM),
                  pl.BlockSpec((1, 1, tm), lambda i: (jnp.minimum(i + 1, n_steps - 1), 0, 0),
                               memory_space=pltpu.SMEM),
                  pl.BlockSpec(memory_space=pl.ANY),
                  pl.BlockSpec((tm, d), lambda i: (i, 0)),
                  pl.BlockSpec((1, 1, d), lambda i: (i // nt_per_batch, 0, 0))],
        out_specs=pl.BlockSpec((tm, d), lambda i: (i, 0)),
        out_shape=jax.ShapeDtypeStruct((t, d), F32),
        scratch_shapes=[pltpu.VMEM((2, tm, d), F32), pltpu.SemaphoreType.DMA((2,))],
        compiler_params=_params("arbitrary"),
        name="moe_combine",
    )(pos3, pos3, ys, h2, gate)


def _moe(h2, g, shift, scale, gate, wg, bg, we, be, w1, w3, w2, layer, bsz, seq, tm, te):
    t, d = h2.shape
    nt_per_batch = seq // tm
    ne, ng = N_EXPERTS, N_EXPERT_GROUPS
    w_r = jnp.zeros((d, LANES), F32).at[:, :ne].set(we).at[:, ne:ne + ng].set(wg)
    b_r = jnp.zeros((1, LANES), F32).at[0, :ne].set(be).at[0, ne:ne + ng].set(bg)
    tri = jnp.asarray(np.tril(np.ones((tm, tm), np.float32)), BF16)
    xrow, meta_t, counts = _router(h2, g, shift, scale, w_r, b_r, tri, nt_per_batch, tm)
    n_items_max = t // te + N_PAIR_BUCKETS
    pos3, maps = _plan(counts, meta_t, te, n_items_max)
    xs = _dispatch(xrow, pos3, tm)
    ys = _experts(xs, w1, w3, w2, layer, maps, n_items_max, te)
    return _combine(ys, pos3, h2, gate, nt_per_batch, tm)


def _log_sigmoid(x):
    return jnp.minimum(x, 0.0) - jnp.log1p(jnp.exp(-jnp.abs(x)))


def _aug_tables(n_heads):
    assert n_heads * AUG_LANES_PER_HEAD <= LANES
    width = LANES
    pk = np.zeros((F_SPLIT * LANES, width), np.float32)
    pq = np.zeros((F_SPLIT * LANES, width), np.float32)
    ck = np.zeros((1, width), np.float32)
    cq = np.zeros((1, width), np.float32)
    for h in range(n_heads):
        base = h * AUG_LANES_PER_HEAD
        for j in range(F_SPLIT):
            pk[j * LANES + h, base + j] = -1.0
            pq[j * LANES + h, base + F_SPLIT + j] = 1.0
            ck[0, base + F_SPLIT + j] = 1.0
            cq[0, base + j] = 1.0
    return jnp.asarray(pk, BF16), jnp.asarray(pq, BF16), jnp.asarray(ck), jnp.asarray(cq)


def _kv_body(h_ref, g_ref, sh_ref, sc_ref, wk_ref, wvt_ref, wf_ref, fb_ref, kng_ref,
             pkq_ref, ckq_ref, k_ref, vt_ref, ka_ref, qa_ref, carry_ref):
    tm = h_ref.shape[0]

    @pl.when(pl.program_id(1) == 0)
    def _():
        carry_ref[...] = jnp.zeros_like(carry_ref)

    hn = _rms_mod(h_ref[...], g_ref[...], sh_ref[0], sc_ref[0])
    hb = hn.astype(BF16)
    k = jnp.dot(hb, wk_ref[...], preferred_element_type=F32)
    k_ref[...] = _head_rms(k, kng_ref[...]).astype(BF16)
    nt_dims = (((1,), (1,)), ((), ()))
    vt_ref[...] = lax.dot_general(wvt_ref[...], hb, nt_dims, preferred_element_type=F32).astype(BF16)
    fz = _dot_3pass(hn, wf_ref) + fb_ref[...]
    c = _log_sigmoid(fz)
    row = lax.broadcasted_iota(I32, c.shape, 0)
    shift = 1
    while shift < tm:
        c = c + jnp.where(row >= shift, pltpu.roll(c, shift, 0), 0.0)
        shift *= 2
    f = c + carry_ref[...]
    carry_ref[...] = f[tm - 1:tm, :]

    f2 = f * LOG2E
    hi = f2.astype(BF16)
    r1 = f2 - hi.astype(F32)
    mid = r1.astype(BF16)
    lo = (r1 - mid.astype(F32)).astype(BF16)
    pieces = jnp.concatenate([hi, mid, lo], axis=-1)
    aug = (jnp.dot(pieces, pkq_ref[...], preferred_element_type=F32) + ckq_ref[...]).astype(BF16)
    ka_ref[...] = aug[:, :LANES]
    qa_ref[...] = aug[:, LANES:]


def _shared_kv(h2, g, shift, scale, wk, wvt, wf, fb, kng, bsz, seq, tm):
    t, d = h2.shape
    nt = seq // tm
    aw = LANES
    pk, pq, ck, cq = _aug_tables(d // HEAD_DIM)
    pkq = jnp.concatenate([pk, pq], axis=1)
    ckq = jnp.concatenate([ck, cq], axis=1)
    wf_hl = jnp.concatenate(_split_bf16(wf), axis=1)
    row = lambda b, i: (b * nt + i, 0)
    const = lambda b, i: (0, 0)
    return pl.pallas_call(
        _kv_body,
        grid=(bsz, nt),
        in_specs=[pl.BlockSpec((tm, d), row),
                  pl.BlockSpec((1, d), const),
                  pl.BlockSpec((1, 1, d), lambda b, i: (b, 0, 0)),
                  pl.BlockSpec((1, 1, d), lambda b, i: (b, 0, 0)),
                  pl.BlockSpec((d, d), const),
                  pl.BlockSpec((d, d), const),
                  pl.BlockSpec((d, 2 * LANES), const),
                  pl.BlockSpec((1, LANES), const),
                  pl.BlockSpec((1, d), const),
                  pl.BlockSpec((F_SPLIT * LANES, 2 * aw), const),
                  pl.BlockSpec((1, 2 * aw), const)],
        out_specs=[pl.BlockSpec((tm, d), row),
                   pl.BlockSpec((d, tm), lambda b, i: (b, i)),
                   pl.BlockSpec((tm, aw), row),
                   pl.BlockSpec((tm, aw), row)],
        out_shape=[jax.ShapeDtypeStruct((t, d), BF16), jax.ShapeDtypeStruct((bsz * d, seq), BF16),
                   jax.ShapeDtypeStruct((t, aw), BF16), jax.ShapeDtypeStruct((t, aw), BF16)],
        scratch_shapes=[pltpu.VMEM((1, LANES), F32)],
        compiler_params=_params("parallel", "arbitrary"),
        name="shared_kv",
    )(h2, g, shift, scale, wk, wvt, wf_hl, fb, kng, pkq, ckq)


def _qg_body(h_ref, g_ref, sh_ref, sc_ref, wq_ref, wg_ref, qng_ref, q_ref, og_ref):
    hn = _rms_mod(h_ref[...], g_ref[...], sh_ref[0], sc_ref[0])
    hb = hn.astype(BF16)
    q = jnp.dot(hb, wq_ref[...], preferred_element_type=F32)
    q_ref[...] = (_head_rms(q, qng_ref[...]) * (HEAD_DIM ** -0.5 * LOG2E)).astype(BF16)
    og_ref[...] = jnp.dot(hb, wg_ref[...], preferred_element_type=F32).astype(BF16)


def _fox_qg(h2, g, shift, scale, wq, wg, qng, bsz, seq, tm):
    t, d = h2.shape
    nt = seq // tm
    row = lambda b, i: (b * nt + i, 0)
    const = lambda b, i: (0, 0)
    return pl.pallas_call(
        _qg_body,
        grid=(bsz, nt),
        in_specs=[pl.BlockSpec((tm, d), row),
                  pl.BlockSpec((1, d), const),
                  pl.BlockSpec((1, 1, d), lambda b, i: (b, 0, 0)),
                  pl.BlockSpec((1, 1, d), lambda b, i: (b, 0, 0)),
                  pl.BlockSpec((d, d), const),
                  pl.BlockSpec((d, d), const),
                  pl.BlockSpec((1, d), const)],
        out_specs=[pl.BlockSpec((tm, d), row), pl.BlockSpec((tm, d), row)],
        out_shape=[jax.ShapeDtypeStruct((t, d), BF16), jax.ShapeDtypeStruct((t, d), BF16)],
        compiler_params=_params("parallel", "parallel"),
        name="fox_qg",
    )(h2, g, shift, scale, wq, wg, qng)


def _attn_body(q_ref, qa_ref, k_ref, ka_ref, vt_ref, o_ref, m_ref, acc_ref, s0_ref, s1_ref, *, tq, hp):
    grp = pl.program_id(1)
    qi = pl.program_id(2)
    n_slabs = hp // 2
    lane_q = lax.broadcasted_iota(I32, (tq, LANES), 1)
    lo_q = lane_q < HEAD_DIM
    lo_v = lax.broadcasted_iota(I32, (LANES, tq), 0) < HEAD_DIM
    nt_dims = (((1,), (1,)), ((), ()))
    causal = lax.broadcasted_iota(I32, (tq, tq), 0) <= lax.broadcasted_iota(I32, (tq, tq), 1)

    qa = qa_ref[...]
    qcats = []
    for h in range(hp):
        sl, hh = h // 2, h % 2
        q2 = q_ref[:, sl * LANES:(sl + 1) * LANES]
        own = lo_q if hh == 0 else jnp.logical_not(lo_q)
        a0 = (grp * hp + h) * AUG_LANES_PER_HEAD
        own_a = (lane_q >= a0) & (lane_q < a0 + AUG_LANES_PER_HEAD)
        qcats.append(jnp.concatenate([jnp.where(own, q2, jnp.zeros_like(q2)),
                                      jnp.where(own_a, qa, jnp.zeros_like(qa))], axis=-1))

    m_ref[...] = jnp.full(m_ref.shape, NEG, F32)
    ones_r = jnp.ones((DEN_ROWS, tq), BF16)
    zeros_r = jnp.zeros((DEN_ROWS, tq), BF16)
    acc_row = lax.broadcasted_iota(I32, (LANES + 2 * DEN_ROWS, tq), 0)
    first_head_rows = (acc_row < HEAD_DIM) | ((acc_row >= LANES) & (acc_row < LANES + DEN_ROWS))
    acc_ref[...] = jnp.zeros(acc_ref.shape, F32)

    def scores(kb, dst_ref):
        s0 = pl.multiple_of(kb * tq, tq)
        for h in range(hp):
            sl = h // 2
            kcat = jnp.concatenate([k_ref[pl.ds(s0, tq), sl * LANES:(sl + 1) * LANES],
                                    ka_ref[pl.ds(s0, tq), :]], axis=-1)
            dst_ref[h] = lax.dot_general(kcat, qcats[h], nt_dims, preferred_element_type=F32)

    def consume(kb, src_ref, masked):
        s0 = pl.multiple_of(kb * tq, tq)
        for sl in range(n_slabs):
            vt = vt_ref[sl * LANES:(sl + 1) * LANES, pl.ds(s0, tq)]
            vts = (jnp.concatenate([jnp.where(lo_v, vt, jnp.zeros_like(vt)), ones_r, zeros_r], axis=0),
                   jnp.concatenate([jnp.where(lo_v, jnp.zeros_like(vt), vt), zeros_r, ones_r], axis=0))
            pv, alphas = None, []
            for hh in range(2):
                h = sl * 2 + hh
                st = src_ref[h]
                if masked:
                    st = jnp.where(causal, st, NEG)
                m_old = m_ref[h]
                m_new = jnp.maximum(m_old, jnp.max(st, axis=0, keepdims=True))
                alpha = jnp.exp2(m_old - m_new)
                p = jnp.exp2(st - m_new)
                m_ref[h] = m_new
                part = jnp.dot(vts[hh], p.astype(BF16), preferred_element_type=F32)
                pv = part if pv is None else pv + part
                alphas.append(alpha)
            acc_ref[sl] = jnp.where(first_head_rows, alphas[0], alphas[1]) * acc_ref[sl] + pv

    scores(0, s0_ref)

    def pair(j, c):
        scores(2 * j + 1, s1_ref)
        consume(2 * j, s0_ref, False)
        scores(2 * j + 2, s0_ref)
        consume(2 * j + 1, s1_ref, False)
        return c

    lax.fori_loop(0, qi // 2, pair, 0)

    @pl.when(qi % 2 == 0)
    def _():
        consume(qi, s0_ref, True)

    @pl.when(qi % 2 == 1)
    def _():
        scores(qi, s1_ref)
        consume(qi - 1, s0_ref, False)
        consume(qi, s1_ref, True)

    for sl in range(n_slabs):
        acc = acc_ref[sl]
        l2 = jnp.where(lo_v, acc[LANES:LANES + 1, :], acc[LANES + DEN_ROWS:LANES + DEN_ROWS + 1, :])
        o_ref[:, sl * LANES:(sl + 1) * LANES] = (acc[:LANES, :] / l2).T.astype(BF16)


def _fox_attention(q, qaug, k, kaug, vt, bsz, seq, tq, hp):
    t, d = q.shape
    w = hp * HEAD_DIM
    n_grp = d // w
    nq = seq // tq
    body = functools.partial(_attn_body, tq=tq, hp=hp)
    qrow = lambda b, j, i: (b * nq + i, j)
    krow = lambda b, j, i: (b, j)
    return pl.pallas_call(
        body,
        grid=(bsz, n_grp, nq),
        in_specs=[pl.BlockSpec((tq, w), qrow),
                  pl.BlockSpec((tq, LANES), lambda b, j, i: (b * nq + i, 0)),
                  pl.BlockSpec((seq, w), krow),
                  pl.BlockSpec((seq, LANES), lambda b, j, i: (b, 0)),
                  pl.BlockSpec((w, seq), lambda b, j, i: (b * n_grp + j, 0))],
        out_specs=pl.BlockSpec((tq, w), qrow),
        out_shape=jax.ShapeDtypeStruct((t, d), BF16),
        scratch_shapes=[pltpu.VMEM((hp, 1, tq), F32),
                        pltpu.VMEM((hp // 2, LANES + 2 * DEN_ROWS, tq), F32),
                        pltpu.VMEM((hp, tq, tq), F32), pltpu.VMEM((hp, tq, tq), F32)],
        compiler_params=_params("parallel", "parallel", "arbitrary"),
        name="fox_attention",
    )(q, qaug, k, kaug, vt)


def _fox_out_body(o_ref, og_ref, h_ref, w_ref, gate_ref, out_ref):
    z = o_ref[...].astype(F32) * jax.nn.sigmoid(og_ref[...].astype(F32))
    out_ref[...] = h_ref[...] + gate_ref[0] * jnp.dot(z.astype(BF16), w_ref[...], preferred_element_type=F32)


def _fox_out(o, og, h2, w_o, gate, bsz, seq, tm):
    t, d = h2.shape
    nt = seq // tm
    row = lambda b, i: (b * nt + i, 0)
    return pl.pallas_call(
        _fox_out_body,
        grid=(bsz, nt),
        in_specs=[pl.BlockSpec((tm, d), row), pl.BlockSpec((tm, d), row), pl.BlockSpec((tm, d), row),
                  pl.BlockSpec((d, d), lambda b, i: (0, 0)),
                  pl.BlockSpec((1, 1, d), lambda b, i: (b, 0, 0))],
        out_specs=pl.BlockSpec((tm, d), row),
        out_shape=jax.ShapeDtypeStruct((t, d), F32),
        compiler_params=_params("parallel", "parallel"),
        name="fox_out",
    )(o, og, h2, w_o, gate)


def _tiles(seq):
    tm = min(512, seq)
    te = min(256, seq)
    tq = min(512, seq)
    tc = min(64, seq)
    return tm, te, tq, tc


def kernel(x, c, ln_g, ada_w, ada_b, s5_w_in, s5_lambda_re, s5_lambda_im, s5_log_dt, s5_b_re, s5_b_im,
           s5_c_re, s5_c_im, s5_d, s5_w_out, kv_g, kv_ada_w, kv_ada_b, kv_w, kv_fb, k_norm_g,
           fox_w_qg, fox_q_norm_g, fox_w_o, moe_wg, moe_bg, moe_we, moe_be, moe_w1, moe_w3, moe_w2):
    bsz, seq, d = x.shape
    depth = ln_g.shape[0]
    n_a = s5_w_in.shape[0]
    n_heads = d // HEAD_DIM
    tm, te, tq, tc = _tiles(seq)

    mods = _adaln(c, ada_w.reshape(depth * 2, d, 3 * d), ada_b.reshape(depth * 2, 1, 3 * d))
    mods = mods.reshape(depth, 2, bsz, 3, 1, d)
    kv_mods = _adaln(c, kv_ada_w[None], kv_ada_b[None, None]).reshape(bsz, 2, 1, d)

    h = x.reshape(bsz * seq, d)
    k = kaug = qaug = vt = None
    for l in range(depth):
        shift, scale, gate = mods[l, 0, :, 0], mods[l, 0, :, 1], mods[l, 0, :, 2]
        g = ln_g[l, 0][None]
        if l < n_a:
            u2 = _s5_in(h, g, shift, scale, s5_w_in[l].astype(BF16), bsz, seq, tm)
            bblk, cblk, a_re, a_im = _s5_tables(s5_lambda_re[l], s5_lambda_im[l], s5_log_dt[l],
                                                s5_b_re[l], s5_b_im[l], s5_c_re[l], s5_c_im[l])
            y2 = _s5_scan(u2, bblk, cblk, a_re, a_im, bsz, seq, tc)
            h = _s5_out(y2, u2, h, s5_d[l][None], s5_w_out[l].astype(BF16), gate, bsz, seq, tm)
        else:
            j = l - n_a
            qng = jnp.tile(fox_q_norm_g[j], n_heads)[None]
            q, og = _fox_qg(h, g, shift, scale, fox_w_qg[j][:, :d].astype(BF16),
                            fox_w_qg[j][:, d:].astype(BF16), qng, bsz, seq, tm)
            o = _fox_attention(q, qaug, k, kaug, vt, bsz, seq, tq, min(ATTN_HEADS_PER_STEP, n_heads))
            h = _fox_out(o, og, h, fox_w_o[j].astype(BF16), gate, bsz, seq, tm)

        shift, scale, gate = mods[l, 1, :, 0], mods[l, 1, :, 1], mods[l, 1, :, 2]
        h = _moe(h, ln_g[l, 1][None], shift, scale, gate, moe_wg[l], moe_bg[l], moe_we[l], moe_be[l],
                 moe_w1, moe_w3, moe_w2, l, bsz, seq, tm, te)

        if l == n_a - 1:
            wf = jnp.zeros((d, LANES), F32).at[:, :n_heads].set(kv_w[:, 2 * d:])
            fb = jnp.zeros((1, LANES), F32).at[0, :n_heads].set(kv_fb)
            kng = jnp.tile(k_norm_g, n_heads)[None]
            k, vt, kaug, qaug = _shared_kv(h, kv_g[None], kv_mods[:, 0], kv_mods[:, 1],
                                           kv_w[:, :d].astype(BF16), kv_w[:, d:2 * d].T.astype(BF16),
                                           wf, fb, kng, bsz, seq, tm)
    return h.reshape(bsz, seq, d)
```

```python
import functools
import math

import numpy as np
import jax
import jax.numpy as jnp
from jax import lax
from jax.experimental import pallas as pl
from jax.experimental.pallas import tpu as pltpu

F32 = jnp.float32
BF16 = jnp.bfloat16
I32 = jnp.int32

EPS = 1e-6
NEG = -1e30
LOG2E = math.log2(math.e)
LANES = 128
SUBLANES = 8
VMEM_LIMIT_BYTES = 56 * 1024 * 1024

S5_GROUPS_PER_BLOCK = 16
N_EXPERT_GROUPS = 4
EXPERTS_PER_GROUP = 8
N_EXPERTS = N_EXPERT_GROUPS * EXPERTS_PER_GROUP
N_BUCKETS = N_EXPERT_GROUPS * EXPERTS_PER_GROUP * EXPERTS_PER_GROUP
N_PAIR_BUCKETS = N_EXPERT_GROUPS * (EXPERTS_PER_GROUP * (EXPERTS_PER_GROUP - 1) // 2)
META_LANES = LANES
META_ROWS = SUBLANES
ROW_DMA_UNROLL = 8
PLAN_TILES_PER_STEP = 8
HEAD_DIM = 64
ATTN_HEADS_PER_STEP = 8
DEN_ROWS = 16
F_SPLIT = 3
AUG_LANES_PER_HEAD = 2 * F_SPLIT


def _params(*sem):
    return pltpu.CompilerParams(dimension_semantics=sem, vmem_limit_bytes=VMEM_LIMIT_BYTES)


def _rms_mod(x, g, shift, scale):
    ms = jnp.mean(x * x, axis=-1, keepdims=True)
    y = x * lax.rsqrt(ms + EPS) * g
    return y * (1.0 + scale) + shift


def _split_bf16(w):
    hi = w.astype(BF16)
    return hi, (w - hi.astype(F32)).astype(BF16)


def _dot_3pass(x, w_hl_ref):
    n = w_hl_ref.shape[1] // 2
    x_hi, x_lo = _split_bf16(x)
    both = jnp.dot(x_hi, w_hl_ref[...], preferred_element_type=F32)
    return both[:, :n] + both[:, n:] + jnp.dot(x_lo, w_hl_ref[:, :n], preferred_element_type=F32)


def _head_rms(x, g):
    tm, d = x.shape
    lane = lax.broadcasted_iota(I32, (tm, LANES), 1)
    lo = lane < HEAD_DIM
    outs = []
    for j in range(d // LANES):
        s = x[:, j * LANES:(j + 1) * LANES]
        sq = s * s
        s_lo = jnp.sum(jnp.where(lo, sq, 0.0), axis=-1, keepdims=True)
        s_hi = jnp.sum(jnp.where(lo, 0.0, sq), axis=-1, keepdims=True)
        r = jnp.where(lo, lax.rsqrt(s_lo / HEAD_DIM + EPS), lax.rsqrt(s_hi / HEAD_DIM + EPS))
        outs.append(s * r)
    return jnp.concatenate(outs, axis=-1) * g


def _adaln_body(c_ref, w_ref, b_ref, o_ref):
    c = c_ref[...]
    s = c * jax.nn.sigmoid(c)
    o_ref[0] = jnp.dot(s, w_ref[0], preferred_element_type=F32) + b_ref[0]


def _adaln(c, w, b):
    n_sets, d, n = w.shape
    bsz = c.shape[0]
    tn = 512 if n % 512 == 0 else n
    return pl.pallas_call(
        _adaln_body,
        grid=(n_sets, n // tn),
        in_specs=[pl.BlockSpec((bsz, d), lambda s, j: (0, 0)),
                  pl.BlockSpec((1, d, tn), lambda s, j: (s, 0, j)),
                  pl.BlockSpec((1, 1, tn), lambda s, j: (s, 0, j))],
        out_specs=pl.BlockSpec((1, bsz, tn), lambda s, j: (s, 0, j)),
        out_shape=jax.ShapeDtypeStruct((n_sets, bsz, n), F32),
        compiler_params=_params("parallel", "parallel"),
        name="adaln",
    )(c, w, b)


def _s5_in_body(x_ref, g_ref, sh_ref, sc_ref, w_ref, u_ref):
    hn = _rms_mod(x_ref[...], g_ref[...], sh_ref[0], sc_ref[0])
    u_ref[...] = jnp.dot(hn.astype(BF16), w_ref[...], preferred_element_type=F32)


def _s5_in(x2, g, shift, scale, w_in, bsz, seq, tm):
    d = x2.shape[1]
    nt = seq // tm
    row = lambda b, i: (b * nt + i, 0)
    return pl.pallas_call(
        _s5_in_body,
        grid=(bsz, nt),
        in_specs=[pl.BlockSpec((tm, d), row),
                  pl.BlockSpec((1, d), lambda b, i: (0, 0)),
                  pl.BlockSpec((1, 1, d), lambda b, i: (b, 0, 0)),
                  pl.BlockSpec((1, 1, d), lambda b, i: (b, 0, 0)),
                  pl.BlockSpec((d, d), lambda b, i: (0, 0))],
        out_specs=pl.BlockSpec((tm, d), row),
        out_shape=jax.ShapeDtypeStruct((bsz * seq, d), F32),
        compiler_params=_params("parallel", "parallel"),
        name="s5_in",
    )(x2, g, shift, scale, w_in)


def _s5_scan_body(u_hbm, bb_ref, cb_ref, are_ref, aim_ref, y_hbm,
                  ubuf, ybuf, bu_ref, st_ref, sem_in, sem_out, *, tc, nblk, sw, seq, n_chunks):
    bsz = SUBLANES
    cw = S5_GROUPS_PER_BLOCK * 16
    i = pl.program_id(0)
    slot = lax.rem(i, 2)

    def in_copy(chunk, sl, b):
        return pltpu.make_async_copy(u_hbm.at[pl.ds(b * seq + chunk * tc, tc)],
                                     ubuf.at[sl, :, b, :], sem_in.at[sl])

    def out_copy(chunk, sl, b):
        return pltpu.make_async_copy(ybuf.at[sl, :, b, :],
                                     y_hbm.at[pl.ds(b * seq + chunk * tc, tc)], sem_out.at[sl])

    @pl.when(i == 0)
    def _():
        st_ref[...] = jnp.zeros_like(st_ref)
        for b in range(bsz):
            in_copy(0, 0, b).start()

    @pl.when(i + 1 < n_chunks)
    def _():
        for b in range(bsz):
            in_copy(i + 1, 1 - slot, b).start()

    for b in range(bsz):
        in_copy(i, slot, b).wait()

    @pl.when(i >= 2)
    def _():
        for b in range(bsz):
            out_copy(i - 2, slot, b).wait()

    d = ubuf.shape[-1]
    u2 = ubuf[slot].reshape(tc * bsz, d).astype(BF16)
    for k in range(nblk):
        bu_ref[:, k * 2 * sw:(k + 1) * 2 * sw] = jnp.dot(
            u2[:, k * cw:(k + 1) * cw], bb_ref[k], preferred_element_type=F32)
        re0, im0 = k * 2 * sw, k * 2 * sw + sw
        a_re = jnp.broadcast_to(are_ref[k], (SUBLANES, sw))
        a_im = jnp.broadcast_to(aim_ref[k], (SUBLANES, sw))

        def step(t, carry, re0=re0, im0=im0, a_re=a_re, a_im=a_im):
            s_re, s_im = carry
            r0 = pl.multiple_of(t * SUBLANES, SUBLANES)
            n_re = a_re * s_re - a_im * s_im + bu_ref[pl.ds(r0, SUBLANES), re0:re0 + sw]
            n_im = a_re * s_im + a_im * s_re + bu_ref[pl.ds(r0, SUBLANES), im0:im0 + sw]
            bu_ref[pl.ds(r0, SUBLANES), re0:re0 + sw] = n_re
            bu_ref[pl.ds(r0, SUBLANES), im0:im0 + sw] = n_im
            return n_re, n_im

        s_re, s_im = lax.fori_loop(
            0, tc, step, (st_ref[:, re0:re0 + sw], st_ref[:, im0:im0 + sw]), unroll=True)
        st_ref[:, re0:re0 + sw] = s_re
        st_ref[:, im0:im0 + sw] = s_im
        s2 = bu_ref[:, k * 2 * sw:(k + 1) * 2 * sw].astype(BF16)
        yk = jnp.dot(s2, cb_ref[k], preferred_element_type=F32)
        ybuf[slot, :, :, k * cw:(k + 1) * cw] = yk.reshape(tc, bsz, cw)

    for b in range(bsz):
        out_copy(i, slot, b).start()

    @pl.when(i == n_chunks - 1)
    def _():
        if n_chunks >= 2:
            for b in range(bsz):
                out_copy(i - 1, 1 - slot, b).wait()
        for b in range(bsz):
            out_copy(i, slot, b).wait()


def _s5_scan(u2, bblk, cblk, a_re, a_im, bsz, seq, tc):
    assert bsz == SUBLANES, "the scan keeps the batch on the 8 sublanes of a vreg"
    rows, d = u2.shape
    nblk, cw, sw2 = bblk.shape
    sw = sw2 // 2
    n_chunks = seq // tc
    body = functools.partial(_s5_scan_body, tc=tc, nblk=nblk, sw=sw, seq=seq, n_chunks=n_chunks)
    return pl.pallas_call(
        body,
        grid=(n_chunks,),
        in_specs=[pl.BlockSpec(memory_space=pl.ANY),
                  pl.BlockSpec((nblk, cw, sw2), lambda i: (0, 0, 0)),
                  pl.BlockSpec((nblk, sw2, cw), lambda i: (0, 0, 0)),
                  pl.BlockSpec((nblk, 1, sw), lambda i: (0, 0, 0)),
                  pl.BlockSpec((nblk, 1, sw), lambda i: (0, 0, 0))],
        out_specs=pl.BlockSpec(memory_space=pl.ANY),
        out_shape=jax.ShapeDtypeStruct((rows, d), F32),
        scratch_shapes=[pltpu.VMEM((2, tc, bsz, d), F32), pltpu.VMEM((2, tc, bsz, d), F32),
                        pltpu.VMEM((tc * bsz, nblk * sw2), F32), pltpu.VMEM((bsz, nblk * sw2), F32),
                        pltpu.SemaphoreType.DMA((2,)), pltpu.SemaphoreType.DMA((2,))],
        compiler_params=_params("arbitrary"),
        name="s5_scan",
    )(u2, bblk, cblk, a_re, a_im)


def _s5_out_body(y_ref, u_ref, h_ref, d_ref, w_ref, gate_ref, o_ref):
    z = y_ref[...] + d_ref[...] * u_ref[...]
    act = jax.nn.gelu(z)
    vg = jnp.dot(act.astype(BF16), w_ref[...], preferred_element_type=F32)
    d = z.shape[-1]
    mix = vg[:, :d] * jax.nn.sigmoid(vg[:, d:])
    o_ref[...] = h_ref[...] + gate_ref[0] * mix


def _s5_out(y2, u2, h2, d_skip, w_out, gate, bsz, seq, tm):
    d = h2.shape[1]
    nt = seq // tm
    row = lambda b, i: (b * nt + i, 0)
    return pl.pallas_call(
        _s5_out_body,
        grid=(bsz, nt),
        in_specs=[pl.BlockSpec((tm, d), row),
                  pl.BlockSpec((tm, d), row),
                  pl.BlockSpec((tm, d), row),
                  pl.BlockSpec((1, d), lambda b, i: (0, 0)),
                  pl.BlockSpec((d, 2 * d), lambda b, i: (0, 0)),
                  pl.BlockSpec((1, 1, d), lambda b, i: (b, 0, 0))],
        out_specs=pl.BlockSpec((tm, d), row),
        out_shape=jax.ShapeDtypeStruct((bsz * seq, d), F32),
        compiler_params=_params("parallel", "parallel"),
        name="s5_out",
    )(y2, u2, h2, d_skip, w_out, gate)


def _s5_tables(lam_re, lam_im, log_dt, b_re, b_im, c_re, c_im):
    dt = jnp.exp(log_dt.astype(F32))[:, None]
    lr, li = lam_re.astype(F32), lam_im.astype(F32)
    mag = jnp.exp(lr * dt)
    a_re = mag * jnp.cos(li * dt)
    a_im = mag * jnp.sin(li * dt)
    den = lr * lr + li * li
    coef_re = ((a_re - 1.0) * lr + a_im * li) / den
    coef_im = (a_im * lr - (a_re - 1.0) * li) / den
    br_, bi_ = b_re.astype(F32), b_im.astype(F32)
    bbar_re = coef_re[..., None] * br_ - coef_im[..., None] * bi_
    bbar_im = coef_re[..., None] * bi_ + coef_im[..., None] * br_
    g, p, c = bbar_re.shape
    gb = S5_GROUPS_PER_BLOCK
    nblk = g // gb
    eye = jnp.eye(gb, dtype=F32)

    def in_blocks(m):
        return jnp.einsum('kgpc,gh->kgchp', m.reshape(nblk, gb, p, c), eye).reshape(nblk, gb * c, gb * p)

    def out_blocks(m):
        return jnp.einsum('kgcp,gh->kgphc', m.reshape(nblk, gb, c, p), eye).reshape(nblk, gb * p, gb * c)

    bblk = jnp.concatenate([in_blocks(bbar_re), in_blocks(bbar_im)], axis=-1).astype(BF16)
    cblk = jnp.concatenate([out_blocks(c_re.astype(F32)), -out_blocks(c_im.astype(F32))], axis=1).astype(BF16)
    return bblk, cblk, a_re.reshape(nblk, 1, gb * p), a_im.reshape(nblk, 1, gb * p)


def _router_body(h_ref, g_ref, sh_ref, sc_ref, wr_ref, br_ref, tri_ref,
                 x_ref, mt_ref, cnt_ref, carry_ref):
    tm, d = h_ref.shape
    ne, ng, epg = N_EXPERTS, N_EXPERT_GROUPS, EXPERTS_PER_GROUP

    @pl.when(pl.program_id(0) == 0)
    def _():
        carry_ref[...] = jnp.zeros_like(carry_ref)

    hn = _rms_mod(h_ref[...], g_ref[...], sh_ref[0], sc_ref[0])
    logits = _dot_3pass(hn, wr_ref) + br_ref[...]
    lane = lax.broadcasted_iota(I32, logits.shape, 1).astype(F32)
    big = jnp.float32(1e9)
    ninf = jnp.float32(-jnp.inf)

    gmask = (lane >= ne) & (lane < ne + ng)
    gmax = jnp.max(jnp.where(gmask, logits, ninf), axis=-1, keepdims=True)
    gsum = jnp.sum(jnp.where(gmask, jnp.exp(logits - gmax), 0.0), axis=-1, keepdims=True)
    p_g = 1.0 / gsum
    gidx = jnp.min(jnp.where(gmask & (logits == gmax), lane - ne, big), axis=-1, keepdims=True)

    emask = (lane < ne) & (jnp.floor(lane / epg) == gidx)
    v1 = jnp.max(jnp.where(emask, logits, ninf), axis=-1, keepdims=True)
    i1 = jnp.min(jnp.where(emask & (logits == v1), lane, big), axis=-1, keepdims=True)
    emask2 = emask & (lane != i1)
    v2 = jnp.max(jnp.where(emask2, logits, ninf), axis=-1, keepdims=True)
    i2 = jnp.min(jnp.where(emask2 & (logits == v2), lane, big), axis=-1, keepdims=True)
    e21 = jnp.exp(v2 - v1)
    w1 = p_g / (1.0 + e21)
    w2 = p_g * e21 / (1.0 + e21)

    first_lo = i1 < i2
    e_lo = jnp.where(first_lo, i1, i2)
    e_hi = jnp.where(first_lo, i2, i1)
    w_lo = jnp.where(first_lo, w1, w2)
    w_hi = jnp.where(first_lo, w2, w1)
    bucket = gidx * (epg * epg) + (e_lo - gidx * epg) * epg + (e_hi - gidx * epg)

    lane_b = lax.broadcasted_iota(I32, (tm, N_BUCKETS), 1).astype(F32)
    onehot = (lane_b == bucket).astype(F32)
    prefix = jnp.dot(tri_ref[...], onehot.astype(BF16), preferred_element_type=F32)
    carry = carry_ref[...]
    rank = jnp.sum(onehot * (prefix + carry), axis=-1, keepdims=True) - 1.0
    new_carry = carry + prefix[tm - 1:tm, :]
    carry_ref[...] = new_carry
    cnt_ref[...] = new_carry

    mlane = lax.broadcasted_iota(I32, (tm, META_LANES), 1)
    meta = jnp.where(mlane == 0, bucket,
           jnp.where(mlane == 1, rank,
           jnp.where(mlane == 2, w_lo,
           jnp.where(mlane == 3, w_hi, 0.0))))
    x_ref[:, :d] = hn
    x_ref[:, d:] = meta
    mt_ref[0] = meta.T[:META_ROWS, :]


def _router(h2, g, shift, scale, w_r, b_r, tri, nt_per_batch, tm):
    t, d = h2.shape
    wr_hl = jnp.concatenate(_split_bf16(w_r), axis=1)
    return pl.pallas_call(
        _router_body,
        grid=(t // tm,),
        in_specs=[pl.BlockSpec((tm, d), lambda i: (i, 0)),
                  pl.BlockSpec((1, d), lambda i: (0, 0)),
                  pl.BlockSpec((1, 1, d), lambda i: (i // nt_per_batch, 0, 0)),
                  pl.BlockSpec((1, 1, d), lambda i: (i // nt_per_batch, 0, 0)),
                  pl.BlockSpec((d, 2 * LANES), lambda i: (0, 0)),
                  pl.BlockSpec((1, LANES), lambda i: (0, 0)),
                  pl.BlockSpec((tm, tm), lambda i: (0, 0))],
        out_specs=[pl.BlockSpec((tm, d + META_LANES), lambda i: (i, 0)),
                   pl.BlockSpec((1, META_ROWS, tm), lambda i: (i, 0, 0)),
                   pl.BlockSpec((1, N_BUCKETS), lambda i: (0, 0))],
        out_shape=[jax.ShapeDtypeStruct((t, d + META_LANES), F32),
                   jax.ShapeDtypeStruct((t // tm, META_ROWS, tm), F32),
                   jax.ShapeDtypeStruct((1, N_BUCKETS), F32)],
        scratch_shapes=[pltpu.VMEM((1, N_BUCKETS), F32)],
        compiler_params=_params("arbitrary"),
        name="moe_router",
    )(h2, g, shift, scale, wr_hl, b_r, tri)


def _plan_body(cnt_ref, mt_ref, pos_ref, maps_ref, start_ref, *, te, nwp):
    nb = N_BUCKETS
    epg = EXPERTS_PER_GROUP

    @pl.when(pl.program_id(0) == 0)
    def _():
        r = lax.broadcasted_iota(I32, (nb, nb), 0)
        c = lax.broadcasted_iota(I32, (nb, nb), 1)
        nt_dims = (((1,), (1,)), ((), ()))

        def column(mask, row_vals):
            row8 = jnp.broadcast_to(row_vals, (SUBLANES, nb)).astype(BF16)
            return lax.dot_general(mask.astype(BF16), row8, nt_dims, preferred_element_type=F32)[:, :1]

        cnt = cnt_ref[...]
        cnt_hi = jnp.floor(cnt / 256.0)
        cnt_lo = cnt - 256.0 * cnt_hi
        start = 256.0 * column(c < r, cnt_hi) + column(c < r, cnt_lo)
        count = 256.0 * column(c == r, cnt_hi) + column(c == r, cnt_lo)
        end = start + count
        start_ref[...] = start
        first_tile = jnp.floor(start / te)
        n_items = jnp.where(count > 0.0, jnp.floor((end - 1.0) / te) - first_tile + 1.0, 0.0)
        items8 = jnp.broadcast_to(n_items, (nb, LANES)).astype(BF16)
        item_end = jnp.dot((c <= r).astype(BF16), items8, preferred_element_type=F32)[:, :1]
        item_start = item_end - n_items
        n_total = item_end[nb - 1:nb, :]

        w = lax.broadcasted_iota(I32, (1, nwp), 1).astype(F32)
        wc = jnp.minimum(w, jnp.maximum(n_total - 1.0, 0.0))
        bucket = jnp.sum((item_end <= wc).astype(F32), axis=0, keepdims=True)
        sel = lax.broadcasted_iota(I32, (nb, nwp), 0).astype(F32) == bucket

        def pick(col):
            return jnp.sum(jnp.where(sel, col, 0.0), axis=0, keepdims=True)

        valid = (w < n_total).astype(F32)
        tile = pick(first_tile) + (wc - pick(item_start))
        row_lo = (jnp.maximum(pick(start), tile * te) - tile * te) * valid
        row_hi = (jnp.minimum(pick(end), (tile + 1.0) * te) - tile * te) * valid
        grp = jnp.floor(bucket / (epg * epg))
        within = bucket - grp * (epg * epg)
        lo = jnp.floor(within / epg)
        e_lo = grp * epg + lo
        e_hi = grp * epg + (within - lo * epg)
        row = lax.broadcasted_iota(I32, (SUBLANES, nwp), 0)
        maps = jnp.where(row == 0, tile, jnp.where(row == 1, e_lo, jnp.where(row == 2, e_hi,
               jnp.where(row == 3, valid, jnp.where(row == 4, row_lo, jnp.where(row == 5, row_hi, 0.0))))))
        maps_ref[...] = maps.astype(I32)

    n_sub, _, tm = mt_ref.shape
    rb = lax.broadcasted_iota(I32, (nb, tm), 0).astype(F32)
    for s in range(n_sub):
        bucket_row = mt_ref[s, 0:1, :]
        rank_row = mt_ref[s, 1:2, :]
        pos = jnp.sum(jnp.where(rb == bucket_row, start_ref[...], 0.0), axis=0, keepdims=True) + rank_row
        pos_ref[s] = pos.astype(I32)


def _plan(counts, meta_t, te, n_items_max):
    n_tt, _, tm = meta_t.shape
    nwp = ((n_items_max + LANES - 1) // LANES) * LANES
    body = functools.partial(_plan_body, te=te, nwp=nwp)
    n_sub = math.gcd(n_tt, PLAN_TILES_PER_STEP)
    return pl.pallas_call(
        body,
        grid=(n_tt // n_sub,),
        in_specs=[pl.BlockSpec((1, N_BUCKETS), lambda i: (0, 0)),
                  pl.BlockSpec((n_sub, META_ROWS, tm), lambda i: (i, 0, 0))],
        out_specs=[pl.BlockSpec((n_sub, 1, tm), lambda i: (i, 0, 0)),
                   pl.BlockSpec((SUBLANES, nwp), lambda i: (0, 0))],
        out_shape=[jax.ShapeDtypeStruct((n_tt, 1, tm), I32),
                   jax.ShapeDtypeStruct((SUBLANES, nwp), I32)],
        scratch_shapes=[pltpu.VMEM((N_BUCKETS, 1), F32)],
        compiler_params=_params("arbitrary"),
        name="moe_plan",
    )(counts, meta_t)


def _row_copy(src_ref, src_row, dst_ref, dst_row, sem):
    return pltpu.make_async_copy(src_ref.at[pl.ds(src_row, 1)], dst_ref.at[pl.ds(dst_row, 1)], sem)


def _dispatch_body(pos_ref, x_ref, xs_ref, sem):
    tm = x_ref.shape[0]

    def issue(grp, c):
        r0 = pl.multiple_of(grp * ROW_DMA_UNROLL, ROW_DMA_UNROLL)
        for u in range(ROW_DMA_UNROLL):
            _row_copy(x_ref, r0 + u, xs_ref, pos_ref[0, 0, r0 + u], sem).start(priority=u % 2)
        return c

    lax.fori_loop(0, tm // ROW_DMA_UNROLL, issue, 0, unroll=True)
    pltpu.make_async_copy(x_ref, xs_ref.at[pl.ds(0, tm)], sem).wait()


def _dispatch(xrow, pos3, tm):
    t, w = xrow.shape
    t_pad = t
    return pl.pallas_call(
        _dispatch_body,
        grid=(t // tm,),
        in_specs=[pl.BlockSpec((1, 1, tm), lambda i: (i, 0, 0), memory_space=pltpu.SMEM),
                  pl.BlockSpec((tm, w), lambda i: (i, 0))],
        out_specs=pl.BlockSpec(memory_space=pl.ANY),
        out_shape=jax.ShapeDtypeStruct((t_pad, w), F32),
        scratch_shapes=[pltpu.SemaphoreType.DMA(())],
        compiler_params=_params("arbitrary"),
        name="moe_dispatch",
    )(pos3, xrow)


def _expert_body(tile_ref, elo_ref, ehi_ref, valid_ref, rlo_ref, rhi_ref, x_ref,
                 w1_ref, w3_ref, w2_ref, o_ref, up_s, dn_s):
    del tile_ref
    te, d = o_ref.shape
    epg = EXPERTS_PER_GROUP
    j = pl.program_id(0)
    prev = jnp.maximum(j - 1, 0)

    @pl.when((j == 0) | (elo_ref[j] // epg != elo_ref[prev] // epg))
    def _():
        for e in range(epg):
            up_s[e, 0] = w1_ref[e].astype(BF16)
            up_s[e, 1] = w3_ref[e].astype(BF16)
            dn_s[e] = w2_ref[e].astype(BF16)

    @pl.when(valid_ref[j] == 1)
    def _():
        rows = lax.broadcasted_iota(I32, (te, 1), 0)
        live = (rows >= rlo_ref[j]) & (rows < rhi_ref[j])
        x = x_ref[:, :d].astype(BF16)

        def ffn(e, wt):
            a = jnp.dot(x, up_s[e, 0], preferred_element_type=F32)
            b = jnp.dot(x, up_s[e, 1], preferred_element_type=F32)
            mid = (a * jax.nn.sigmoid(a)) * b * wt
            return jnp.dot(mid.astype(BF16), dn_s[e], preferred_element_type=F32)

        res = (ffn(elo_ref[j] % epg, jnp.where(live, x_ref[:, d + 2:d + 3], 0.0))
               + ffn(ehi_ref[j] % epg, jnp.where(live, x_ref[:, d + 3:d + 4], 0.0)))

        @pl.when(rlo_ref[j] == 0)
        def _():
            o_ref[...] = res

        @pl.when(rlo_ref[j] != 0)
        def _():
            o_ref[...] += res


def _experts(xs, w1, w3, w2, layer, maps, n_items, te):
    t, w = xs.shape
    _, n_e, d, f = w1.shape
    epg = EXPERTS_PER_GROUP
    x_spec = pl.BlockSpec((te, w), lambda j, tl, lo, hi, v, a, b: (tl[j], 0))
    up = pl.BlockSpec((None, epg, d, f), lambda j, tl, lo, hi, v, a, b: (layer, lo[j] // epg, 0, 0),
                      pipeline_mode=pl.Buffered(1))
    dn = pl.BlockSpec((None, epg, f, d), lambda j, tl, lo, hi, v, a, b: (layer, lo[j] // epg, 0, 0),
                      pipeline_mode=pl.Buffered(1))
    return pl.pallas_call(
        _expert_body,
        grid_spec=pltpu.PrefetchScalarGridSpec(
            num_scalar_prefetch=6,
            grid=(n_items,),
            in_specs=[x_spec, up, up, dn],
            out_specs=pl.BlockSpec((te, d), lambda j, tl, lo, hi, v, a, b: (tl[j], 0)),
            scratch_shapes=[pltpu.VMEM((epg, 2, d, f), BF16), pltpu.VMEM((epg, f, d), BF16)]),
        out_shape=jax.ShapeDtypeStruct((t, d), F32),
        compiler_params=_params("arbitrary"),
        name="moe_experts",
    )(*[maps[i, :n_items] for i in range(6)], xs, w1, w3, w2)


def _combine_body(pos_ref, pos_next_ref, ys_ref, h_ref, gate_ref, o_ref, ybuf, sem, *, n_steps):
    tm = h_ref.shape[0]
    i = pl.program_id(0)
    slot = lax.rem(i, 2)

    def gather(p_ref, sl):
        def issue(grp, c):
            r0 = pl.multiple_of(grp * ROW_DMA_UNROLL, ROW_DMA_UNROLL)
            for u in range(ROW_DMA_UNROLL):
                _row_copy(ys_ref, p_ref[0, 0, r0 + u], ybuf.at[sl], r0 + u, sem.at[sl]).start(priority=u % 2)
            return c

        lax.fori_loop(0, tm // ROW_DMA_UNROLL, issue, 0, unroll=True)

    @pl.when(i == 0)
    def _():
        gather(pos_ref, 0)

    @pl.when(i + 1 < n_steps)
    def _():
        gather(pos_next_ref, 1 - slot)

    pltpu.make_async_copy(ys_ref.at[pl.ds(0, tm)], ybuf.at[slot], sem.at[slot]).wait()
    o_ref[...] = h_ref[...] + gate_ref[0] * ybuf[slot]


def _combine(ys, pos3, h2, gate, nt_per_batch, tm):
    t, d = h2.shape
    n_steps = t // tm
    body = functools.partial(_combine_body, n_steps=n_steps)
    return pl.pallas_call(
        body,
        grid=(n_steps,),
        in_specs=[pl.BlockSpec((1, 1, tm), lambda i: (i, 0, 0), memory_space=pltpu.SMEM),
                  pl.BlockSpec((1, 1, tm), lambda i: (jnp.minimum(i + 1, n_steps - 1), 0, 0),
                               memory_space=pltpu.SMEM),
                  pl.BlockSpec(memory_space=pl.ANY),
                  pl.BlockSpec((tm, d), lambda i: (i, 0)),
                  pl.BlockSpec((1, 1, d), lambda i: (i // nt_per_batch, 0, 0))],
        out_specs=pl.BlockSpec((tm, d), lambda i: (i, 0)),
        out_shape=jax.ShapeDtypeStruct((t, d), F32),
        scratch_shapes=[pltpu.VMEM((2, tm, d), F32), pltpu.SemaphoreType.DMA((2,))],
        compiler_params=_params("arbitrary"),
        name="moe_combine",
    )(pos3, pos3, ys, h2, gate)


def _moe(h2, g, shift, scale, gate, wg, bg, we, be, w1, w3, w2, layer, bsz, seq, tm, te):
    t, d = h2.shape
    nt_per_batch = seq // tm
    ne, ng = N_EXPERTS, N_EXPERT_GROUPS
    w_r = jnp.zeros((d, LANES), F32).at[:, :ne].set(we).at[:, ne:ne + ng].set(wg)
    b_r = jnp.zeros((1, LANES), F32).at[0, :ne].set(be).at[0, ne:ne + ng].set(bg)
    tri = jnp.asarray(np.tril(np.ones((tm, tm), np.float32)), BF16)
    xrow, meta_t, counts = _router(h2, g, shift, scale, w_r, b_r, tri, nt_per_batch, tm)
    n_items_max = t // te + N_PAIR_BUCKETS
    pos3, maps = _plan(counts, meta_t, te, n_items_max)
    xs = _dispatch(xrow, pos3, tm)
    ys = _experts(xs, w1, w3, w2, layer, maps, n_items_max, te)
    return _combine(ys, pos3, h2, gate, nt_per_batch, tm)


def _log_sigmoid(x):
    return jnp.minimum(x, 0.0) - jnp.log1p(jnp.exp(-jnp.abs(x)))


def _aug_tables(n_heads):
    assert n_heads * AUG_LANES_PER_HEAD <= LANES
    width = LANES
    pk = np.zeros((F_SPLIT * LANES, width), np.float32)
    pq = np.zeros((F_SPLIT * LANES, width), np.float32)
    ck = np.zeros((1, width), np.float32)
    cq = np.zeros((1, width), np.float32)
    for h in range(n_heads):
        base = h * AUG_LANES_PER_HEAD
        for j in range(F_SPLIT):
            pk[j * LANES + h, base + j] = -1.0
            pq[j * LANES + h, base + F_SPLIT + j] = 1.0
            ck[0, base + F_SPLIT + j] = 1.0
            cq[0, base + j] = 1.0
    return jnp.asarray(pk, BF16), jnp.asarray(pq, BF16), jnp.asarray(ck), jnp.asarray(cq)


def _kv_body(h_ref, g_ref, sh_ref, sc_ref, wk_ref, wvt_ref, wf_ref, fb_ref, kng_ref,
             pkq_ref, ckq_ref, k_ref, vt_ref, ka_ref, qa_ref, carry_ref):
    tm = h_ref.shape[0]

    @pl.when(pl.program_id(1) == 0)
    def _():
        carry_ref[...] = jnp.zeros_like(carry_ref)

    hn = _rms_mod(h_ref[...], g_ref[...], sh_ref[0], sc_ref[0])
    hb = hn.astype(BF16)
    k = jnp.dot(hb, wk_ref[...], preferred_element_type=F32)
    k_ref[...] = _head_rms(k, kng_ref[...]).astype(BF16)
    nt_dims = (((1,), (1,)), ((), ()))
    vt_ref[...] = lax.dot_general(wvt_ref[...], hb, nt_dims, preferred_element_type=F32).astype(BF16)
    fz = _dot_3pass(hn, wf_ref) + fb_ref[...]
    c = _log_sigmoid(fz)
    row = lax.broadcasted_iota(I32, c.shape, 0)
    shift = 1
    while shift < tm:
        c = c + jnp.where(row >= shift, pltpu.roll(c, shift, 0), 0.0)
        shift *= 2
    f = c + carry_ref[...]
    carry_ref[...] = f[tm - 1:tm, :]

    f2 = f * LOG2E
    hi = f2.astype(BF16)
    r1 = f2 - hi.astype(F32)
    mid = r1.astype(BF16)
    lo = (r1 - mid.astype(F32)).astype(BF16)
    pieces = jnp.concatenate([hi, mid, lo], axis=-1)
    aug = (jnp.dot(pieces, pkq_ref[...], preferred_element_type=F32) + ckq_ref[...]).astype(BF16)
    ka_ref[...] = aug[:, :LANES]
    qa_ref[...] = aug[:, LANES:]


def _shared_kv(h2, g, shift, scale, wk, wvt, wf, fb, kng, bsz, seq, tm):
    t, d = h2.shape
    nt = seq // tm
    aw = LANES
    pk, pq, ck, cq = _aug_tables(d // HEAD_DIM)
    pkq = jnp.concatenate([pk, pq], axis=1)
    ckq = jnp.concatenate([ck, cq], axis=1)
    wf_hl = jnp.concatenate(_split_bf16(wf), axis=1)
    row = lambda b, i: (b * nt + i, 0)
    const = lambda b, i: (0, 0)
    return pl.pallas_call(
        _kv_body,
        grid=(bsz, nt),
        in_specs=[pl.BlockSpec((tm, d), row),
                  pl.BlockSpec((1, d), const),
                  pl.BlockSpec((1, 1, d), lambda b, i: (b, 0, 0)),
                  pl.BlockSpec((1, 1, d), lambda b, i: (b, 0, 0)),
                  pl.BlockSpec((d, d), const),
                  pl.BlockSpec((d, d), const),
                  pl.BlockSpec((d, 2 * LANES), const),
                  pl.BlockSpec((1, LANES), const),
                  pl.BlockSpec((1, d), const),
                  pl.BlockSpec((F_SPLIT * LANES, 2 * aw), const),
                  pl.BlockSpec((1, 2 * aw), const)],
        out_specs=[pl.BlockSpec((tm, d), row),
                   pl.BlockSpec((d, tm), lambda b, i: (b, i)),
                   pl.BlockSpec((tm, aw), row),
                   pl.BlockSpec((tm, aw), row)],
        out_shape=[jax.ShapeDtypeStruct((t, d), BF16), jax.ShapeDtypeStruct((bsz * d, seq), BF16),
                   jax.ShapeDtypeStruct((t, aw), BF16), jax.ShapeDtypeStruct((t, aw), BF16)],
        scratch_shapes=[pltpu.VMEM((1, LANES), F32)],
        compiler_params=_params("parallel", "arbitrary"),
        name="shared_kv",
    )(h2, g, shift, scale, wk, wvt, wf_hl, fb, kng, pkq, ckq)


def _qg_body(h_ref, g_ref, sh_ref, sc_ref, wq_ref, wg_ref, qng_ref, q_ref, og_ref):
    hn = _rms_mod(h_ref[...], g_ref[...], sh_ref[0], sc_ref[0])
    hb = hn.astype(BF16)
    q = jnp.dot(hb, wq_ref[...], preferred_element_type=F32)
    q_ref[...] = (_head_rms(q, qng_ref[...]) * (HEAD_DIM ** -0.5 * LOG2E)).astype(BF16)
    og_ref[...] = jnp.dot(hb, wg_ref[...], preferred_element_type=F32).astype(BF16)


def _fox_qg(h2, g, shift, scale, wq, wg, qng, bsz, seq, tm):
    t, d = h2.shape
    nt = seq // tm
    row = lambda b, i: (b * nt + i, 0)
    const = lambda b, i: (0, 0)
    return pl.pallas_call(
        _qg_body,
        grid=(bsz, nt),
        in_specs=[pl.BlockSpec((tm, d), row),
                  pl.BlockSpec((1, d), const),
                  pl.BlockSpec((1, 1, d), lambda b, i: (b, 0, 0)),
                  pl.BlockSpec((1, 1, d), lambda b, i: (b, 0, 0)),
                  pl.BlockSpec((d, d), const),
                  pl.BlockSpec((d, d), const),
                  pl.BlockSpec((1, d), const)],
        out_specs=[pl.BlockSpec((tm, d), row), pl.BlockSpec((tm, d), row)],
        out_shape=[jax.ShapeDtypeStruct((t, d), BF16), jax.ShapeDtypeStruct((t, d), BF16)],
        compiler_params=_params("parallel", "parallel"),
        name="fox_qg",
    )(h2, g, shift, scale, wq, wg, qng)


def _attn_body(q_ref, qa_ref, k_ref, ka_ref, vt_ref, o_ref, m_ref, acc_ref, s0_ref, s1_ref, *, tq, hp):
    grp = pl.program_id(1)
    qi = pl.program_id(2)
    n_slabs = hp // 2
    lane_q = lax.broadcasted_iota(I32, (tq, LANES), 1)
    lo_q = lane_q < HEAD_DIM
    lo_v = lax.broadcasted_iota(I32, (LANES, tq), 0) < HEAD_DIM
    nt_dims = (((1,), (1,)), ((), ()))
    causal = lax.broadcasted_iota(I32, (tq, tq), 0) <= lax.broadcasted_iota(I32, (tq, tq), 1)

    qa = qa_ref[...]
    qcats = []
    for h in range(hp):
        sl, hh = h // 2, h % 2
        q2 = q_ref[:, sl * LANES:(sl + 1) * LANES]
        own = lo_q if hh == 0 else jnp.logical_not(lo_q)
        a0 = (grp * hp + h) * AUG_LANES_PER_HEAD
        own_a = (lane_q >= a0) & (lane_q < a0 + AUG_LANES_PER_HEAD)
        qcats.append(jnp.concatenate([jnp.where(own, q2, jnp.zeros_like(q2)),
                                      jnp.where(own_a, qa, jnp.zeros_like(qa))], axis=-1))

    m_ref[...] = jnp.full(m_ref.shape, NEG, F32)
    ones_r = jnp.ones((DEN_ROWS, tq), BF16)
    zeros_r = jnp.zeros((DEN_ROWS, tq), BF16)
    acc_row = lax.broadcasted_iota(I32, (LANES + 2 * DEN_ROWS, tq), 0)
    first_head_rows = (acc_row < HEAD_DIM) | ((acc_row >= LANES) & (acc_row < LANES + DEN_ROWS))
    acc_ref[...] = jnp.zeros(acc_ref.shape, F32)

    def scores(kb, dst_ref):
        s0 = pl.multiple_of(kb * tq, tq)
        for h in range(hp):
            sl = h // 2
            kcat = jnp.concatenate([k_ref[pl.ds(s0, tq), sl * LANES:(sl + 1) * LANES],
                                    ka_ref[pl.ds(s0, tq), :]], axis=-1)
            dst_ref[h] = lax.dot_general(kcat, qcats[h], nt_dims, preferred_element_type=F32)

    def consume(kb, src_ref, masked):
        s0 = pl.multiple_of(kb * tq, tq)
        for sl in range(n_slabs):
            vt = vt_ref[sl * LANES:(sl + 1) * LANES, pl.ds(s0, tq)]
            vts = (jnp.concatenate([jnp.where(lo_v, vt, jnp.zeros_like(vt)), ones_r, zeros_r], axis=0),
                   jnp.concatenate([jnp.where(lo_v, jnp.zeros_like(vt), vt), zeros_r, ones_r], axis=0))
            pv, alphas = None, []
            for hh in range(2):
                h = sl * 2 + hh
                parts, alpha_halves = [], []
                for c0 in range(0, tq, tq // 2):
                    cs = slice(c0, c0 + tq // 2)
                    st = src_ref[h, :, cs]
                    if masked:
                        st = jnp.where(causal[:, cs], st, NEG)
                    m_old = m_ref[h, :, cs]
                    m_new = jnp.maximum(m_old, jnp.max(st, axis=0, keepdims=True))
                    alpha_halves.append(jnp.exp2(m_old - m_new))
                    p = jnp.exp2(st - m_new)
                    m_ref[h, :, cs] = m_new
                    parts.append(jnp.dot(vts[hh], p.astype(BF16), preferred_element_type=F32))
                part = jnp.concatenate(parts, axis=1)
                alpha = jnp.concatenate(alpha_halves, axis=1)
                pv = part if pv is None else pv + part
                alphas.append(alpha)
            acc_ref[sl] = jnp.where(first_head_rows, alphas[0], alphas[1]) * acc_ref[sl] + pv

    scores(0, s0_ref)

    def pair(j, c):
        scores(2 * j + 1, s1_ref)
        consume(2 * j, s0_ref, False)
        scores(2 * j + 2, s0_ref)
        consume(2 * j + 1, s1_ref, False)
        return c

    lax.fori_loop(0, qi // 2, pair, 0)

    @pl.when(qi % 2 == 0)
    def _():
        consume(qi, s0_ref, True)

    @pl.when(qi % 2 == 1)
    def _():
        scores(qi, s1_ref)
        consume(qi - 1, s0_ref, False)
        consume(qi, s1_ref, True)

    for sl in range(n_slabs):
        acc = acc_ref[sl]
        l2 = jnp.where(lo_v, acc[LANES:LANES + 1, :], acc[LANES + DEN_ROWS:LANES + DEN_ROWS + 1, :])
        o_ref[:, sl * LANES:(sl + 1) * LANES] = (acc[:LANES, :] / l2).T.astype(BF16)


def _fox_attention(q, qaug, k, kaug, vt, bsz, seq, tq, hp):
    t, d = q.shape
    w = hp * HEAD_DIM
    n_grp = d // w
    nq = seq // tq
    body = functools.partial(_attn_body, tq=tq, hp=hp)
    qrow = lambda b, j, i: (b * nq + i, j)
    krow = lambda b, j, i: (b, j)
    return pl.pallas_call(
        body,
        grid=(bsz, n_grp, nq),
        in_specs=[pl.BlockSpec((tq, w), qrow),
                  pl.BlockSpec((tq, LANES), lambda b, j, i: (b * nq + i, 0)),
                  pl.BlockSpec((seq, w), krow),
                  pl.BlockSpec((seq, LANES), lambda b, j, i: (b, 0)),
                  pl.BlockSpec((w, seq), lambda b, j, i: (b * n_grp + j, 0))],
        out_specs=pl.BlockSpec((tq, w), qrow),
        out_shape=jax.ShapeDtypeStruct((t, d), BF16),
        scratch_shapes=[pltpu.VMEM((hp, 1, tq), F32),
                        pltpu.VMEM((hp // 2, LANES + 2 * DEN_ROWS, tq), F32),
                        pltpu.VMEM((hp, tq, tq), F32), pltpu.VMEM((hp, tq, tq), F32)],
        compiler_params=_params("parallel", "parallel", "arbitrary"),
        name="fox_attention",
    )(q, qaug, k, kaug, vt)


def _fox_out_body(o_ref, og_ref, h_ref, w_ref, gate_ref, out_ref):
    z = o_ref[...].astype(F32) * jax.nn.sigmoid(og_ref[...].astype(F32))
    out_ref[...] = h_ref[...] + gate_ref[0] * jnp.dot(z.astype(BF16), w_ref[...], preferred_element_type=F32)


def _fox_out(o, og, h2, w_o, gate, bsz, seq, tm):
    t, d = h2.shape
    nt = seq // tm
    row = lambda b, i: (b * nt + i, 0)
    return pl.pallas_call(
        _fox_out_body,
        grid=(bsz, nt),
        in_specs=[pl.BlockSpec((tm, d), row), pl.BlockSpec((tm, d), row), pl.BlockSpec((tm, d), row),
                  pl.BlockSpec((d, d), lambda b, i: (0, 0)),
                  pl.BlockSpec((1, 1, d), lambda b, i: (b, 0, 0))],
        out_specs=pl.BlockSpec((tm, d), row),
        out_shape=jax.ShapeDtypeStruct((t, d), F32),
        compiler_params=_params("parallel", "parallel"),
        name="fox_out",
    )(o, og, h2, w_o, gate)


def _tiles(seq):
    tm = min(512, seq)
    te = min(256, seq)
    tq = min(512, seq)
    tc = min(64, seq)
    return tm, te, tq, tc


def kernel(x, c, ln_g, ada_w, ada_b, s5_w_in, s5_lambda_re, s5_lambda_im, s5_log_dt, s5_b_re, s5_b_im,
           s5_c_re, s5_c_im, s5_d, s5_w_out, kv_g, kv_ada_w, kv_ada_b, kv_w, kv_fb, k_norm_g,
           fox_w_qg, fox_q_norm_g, fox_w_o, moe_wg, moe_bg, moe_we, moe_be, moe_w1, moe_w3, moe_w2):
    bsz, seq, d = x.shape
    depth = ln_g.shape[0]
    n_a = s5_w_in.shape[0]
    n_heads = d // HEAD_DIM
    tm, te, tq, tc = _tiles(seq)

    mods = _adaln(c, ada_w.reshape(depth * 2, d, 3 * d), ada_b.reshape(depth * 2, 1, 3 * d))
    mods = mods.reshape(depth, 2, bsz, 3, 1, d)
    kv_mods = _adaln(c, kv_ada_w[None], kv_ada_b[None, None]).reshape(bsz, 2, 1, d)

    h = x.reshape(bsz * seq, d)
    k = kaug = qaug = vt = None
    for l in range(depth):
        shift, scale, gate = mods[l, 0, :, 0], mods[l, 0, :, 1], mods[l, 0, :, 2]
        g = ln_g[l, 0][None]
        if l < n_a:
            u2 = _s5_in(h, g, shift, scale, s5_w_in[l].astype(BF16), bsz, seq, tm)
            bblk, cblk, a_re, a_im = _s5_tables(s5_lambda_re[l], s5_lambda_im[l], s5_log_dt[l],
                                                s5_b_re[l], s5_b_im[l], s5_c_re[l], s5_c_im[l])
            y2 = _s5_scan(u2, bblk, cblk, a_re, a_im, bsz, seq, tc)
            h = _s5_out(y2, u2, h, s5_d[l][None], s5_w_out[l].astype(BF16), gate, bsz, seq, tm)
        else:
            j = l - n_a
            qng = jnp.tile(fox_q_norm_g[j], n_heads)[None]
            q, og = _fox_qg(h, g, shift, scale, fox_w_qg[j][:, :d].astype(BF16),
                            fox_w_qg[j][:, d:].astype(BF16), qng, bsz, seq, tm)
            o = _fox_attention(q, qaug, k, kaug, vt, bsz, seq, tq, min(ATTN_HEADS_PER_STEP, n_heads))
            h = _fox_out(o, og, h, fox_w_o[j].astype(BF16), gate, bsz, seq, tm)

        shift, scale, gate = mods[l, 1, :, 0], mods[l, 1, :, 1], mods[l, 1, :, 2]
        h = _moe(h, ln_g[l, 1][None], shift, scale, gate, moe_wg[l], moe_bg[l], moe_we[l], moe_be[l],
                 moe_w1, moe_w3, moe_w2, l, bsz, seq, tm, te)

        if l == n_a - 1:
            wf = jnp.zeros((d, LANES), F32).at[:, :n_heads].set(kv_w[:, 2 * d:])
            fb = jnp.zeros((1, LANES), F32).at[0, :n_heads].set(kv_fb)
            kng = jnp.tile(k_norm_g, n_heads)[None]
            k, vt, kaug, qaug = _shared_kv(h, kv_g[None], kv_mods[:, 0], kv_mods[:, 1],
                                           kv_w[:, :d].astype(BF16), kv_w[:, d:2 * d].T.astype(BF16),
                                           wf, fb, kng, bsz, seq, tm)
    return h.reshape(bsz, seq, d)
```
